```python
import jax, jax.numpy as jnp
from jax import lax
import numpy as np

D_MODEL = 1024
BATCH = 4
SEQ = 8192
DEPTH = 1
DEC_BATCH = 128
DEC_SEQ = 4
PAST_LEN = 16384
PAGE_SIZE = 128

N_HEADS = 8
N_KV_HEADS = 2
HEAD_DIM = 64
Q_GROUP = N_HEADS // N_KV_HEADS
WINDOW = 128
ATTN_WIDTH = N_HEADS * HEAD_DIM
KV_WIDTH = N_KV_HEADS * HEAD_DIM
SGU_GROUPS = 4
SGU_WIDTH = D_MODEL // 2
SGU_GROUP_DIM = SGU_WIDTH // SGU_GROUPS
CHUNK = 128
N_GROUPS = 4
EXPERTS_PER_GROUP = 4
N_EXPERTS = N_GROUPS * EXPERTS_PER_GROUP
TOP_K = 2
D_EXPERT = 256
D_IN = ATTN_WIDTH + 2 * KV_WIDTH + 2 * SGU_WIDTH + 2 * D_MODEL
EPS = 1e-6
NEG_INF = -1e30

kernel_name = "hybrid_swa_sink_sgu_hmoe_step"


def rms_norm(x, g):
    xf = x.astype(jnp.float32)
    y = xf * lax.rsqrt(jnp.mean(xf * xf, axis=-1, keepdims=True) + EPS)
    return (y * g.astype(jnp.float32)).astype(x.dtype)


def layer_norm(x, g, b):
    xf = x.astype(jnp.float32)
    mu = jnp.mean(xf, axis=-1, keepdims=True)
    xc = xf - mu
    y = xc * lax.rsqrt(jnp.mean(xc * xc, axis=-1, keepdims=True) + EPS)
    return (y * g.astype(jnp.float32) + b.astype(jnp.float32)).astype(x.dtype)


def input_branches(x, norm1_g, w_in, q_norm_g, k_norm_g, ln_v_g, ln_v_b):
    b, s = x.shape[0], x.shape[1]
    h = rms_norm(x, norm1_g)
    z = jnp.einsum("bsd,de->bse", h, w_in)
    widths = [ATTN_WIDTH, KV_WIDTH, KV_WIDTH, SGU_WIDTH, SGU_WIDTH, D_MODEL]
    cuts = [int(c) for c in np.cumsum(widths)]
    q, k, v, u, vs, ga, gb = jnp.split(z, cuts, axis=-1)
    q = rms_norm(q.reshape(b, s, N_KV_HEADS, Q_GROUP, HEAD_DIM), q_norm_g)
    k = rms_norm(k.reshape(b, s, N_KV_HEADS, HEAD_DIM), k_norm_g)
    v = v.reshape(b, s, N_KV_HEADS, HEAD_DIM)
    u = jax.nn.gelu(u)
    vs = layer_norm(jax.nn.gelu(vs), ln_v_g, ln_v_b).reshape(b, s, SGU_GROUPS, SGU_GROUP_DIM)
    return q, k, v, u, vs, ga, gb


def window_mask(n_q, n_s):
    t = jnp.arange(n_q)[:, None]
    s = jnp.arange(n_s)[None, :]
    return (s > t) & (s <= t + WINDOW)


def sink_probs(scores, mask, sinks):
    sc = jnp.where(mask, scores, NEG_INF)
    sink = sinks.astype(jnp.float32).reshape(N_KV_HEADS, Q_GROUP, 1, 1)
    m = jnp.maximum(jnp.max(sc, axis=-1, keepdims=True), sink)
    p = jnp.exp(sc - m)
    denom = jnp.sum(p, axis=-1, keepdims=True) + jnp.exp(sink - m)
    return p / denom


def attn_prompt(q, k, v, sinks):
    b, s = q.shape[0], q.shape[1]
    nb = s // WINDOW
    qb = q.reshape(b, nb, WINDOW, N_KV_HEADS, Q_GROUP, HEAD_DIM)
    kb = k.reshape(b, nb, WINDOW, N_KV_HEADS, HEAD_DIM)
    vb = v.reshape(b, nb, WINDOW, N_KV_HEADS, HEAD_DIM)
    pad = ((0, 0), (1, 0), (0, 0), (0, 0), (0, 0))
    kk = jnp.concatenate([jnp.pad(kb, pad)[:, :-1], kb], axis=2)
    vv = jnp.concatenate([jnp.pad(vb, pad)[:, :-1], vb], axis=2)
    scores = jnp.einsum("bnqkgd,bnskd->bnkgqs", qb, kk,
                        preferred_element_type=jnp.float32) * (HEAD_DIM ** -0.5)
    blk_ok = (jnp.arange(nb)[:, None] > 0) | (jnp.arange(2 * WINDOW)[None, :] >= WINDOW)
    mask = window_mask(WINDOW, 2 * WINDOW)[None, None, :, :] & blk_ok.reshape(nb, 1, 1, 2 * WINDOW)[None]
    mask = mask.reshape(1, nb, 1, 1, WINDOW, 2 * WINDOW)
    p = sink_probs(scores, mask, sinks).astype(vv.dtype)
    o = jnp.einsum("bnkgqs,bnskd->bnqkgd", p, vv).reshape(b, s, ATTN_WIDTH)
    return o, k[:, s - WINDOW:], v[:, s - WINDOW:]


def attn_sample(q, k, v, k_buf, v_buf, sinks):
    b, t = q.shape[0], q.shape[1]
    kk = jnp.concatenate([k_buf, k], axis=1)
    vv = jnp.concatenate([v_buf, v], axis=1)
    scores = jnp.einsum("btkgd,bskd->bkgts", q, kk,
                        preferred_element_type=jnp.float32) * (HEAD_DIM ** -0.5)
    mask = window_mask(t, WINDOW + t)
    p = sink_probs(scores, mask, sinks).astype(vv.dtype)
    o = jnp.einsum("bkgts,bskd->btkgd", p, vv).reshape(b, t, ATTN_WIDTH)
    return o, kk[:, t:], vv[:, t:]


def causal_spatial(w_spatial):
    return w_spatial * jnp.tril(jnp.ones((CHUNK, CHUNK), w_spatial.dtype))[None]


def sgu_prompt(u, vs, w_spatial, b_spatial):
    b, s = u.shape[0], u.shape[1]
    nc = s // CHUNK
    vc = vs.reshape(b, nc, CHUNK, SGU_GROUPS, SGU_GROUP_DIM)
    mixed = jnp.einsum("hij,bnjhc->bnihc", causal_spatial(w_spatial), vc) + b_spatial.T[:, :, None]
    return u * mixed.reshape(b, s, SGU_WIDTH)


def sgu_sample(u, vs, w_spatial, b_spatial):
    b, t = u.shape[0], u.shape[1]
    w = causal_spatial(w_spatial)[:, :t, :t]
    mixed = jnp.einsum("hij,bjhc->bihc", w, vs) + b_spatial[:, :t].T[:, :, None]
    return u * mixed.reshape(b, t, SGU_WIDTH), vs


def merge_branches(oa, ob, ga, gb, w_branch_a, w_branch_b, w_out):
    ya = jnp.einsum("bse,ed->bsd", oa, w_branch_a)
    yb = jnp.einsum("bse,ed->bsd", ob, w_branch_b)
    h = jax.nn.sigmoid(ga) * ya + jax.nn.sigmoid(gb) * yb
    return jnp.einsum("bsd,de->bse", h, w_out)


def hier_moe(h, w_rg, b_rg, w_re, b_re, w_gate, w_up, w_down):
    n = h.shape[0]
    gl = (h @ w_rg + b_rg).astype(jnp.float32)
    gidx = jnp.argmax(gl, axis=-1)
    gw = jnp.take_along_axis(jax.nn.softmax(gl, axis=-1), gidx[:, None], axis=1)
    el = (h @ w_re + b_re).astype(jnp.float32).reshape(n, N_GROUPS, EXPERTS_PER_GROUP)
    el_sel = jnp.take_along_axis(el, gidx[:, None, None], axis=1)[:, 0]
    top_l, top_i = lax.top_k(el_sel, TOP_K)
    top_w = jax.nn.softmax(top_l, axis=-1) * gw
    eidx = gidx[:, None] * EXPERTS_PER_GROUP + top_i
    combine = jnp.sum(jax.nn.one_hot(eidx, N_EXPERTS, dtype=jnp.float32) * top_w[..., None], axis=1)
    a = jnp.einsum("nd,edf->nef", h, w_gate)
    bu = jnp.einsum("nd,edf->nef", h, w_up)
    hid = jax.nn.silu(a) * bu * combine[..., None].astype(h.dtype)
    return jnp.einsum("nef,efd->nd", hid, w_down)


def moe_ffn(x, norm2_g, w_rg, b_rg, w_re, b_re, w_gate, w_up, w_down):
    h = rms_norm(x, norm2_g).reshape(-1, D_MODEL)
    return hier_moe(h, w_rg, b_rg, w_re, b_re, w_gate, w_up, w_down).reshape(x.shape)


def setup_inputs(seed: int = 0) -> dict:
    key = jax.random.key(seed)
    ks = jax.random.split(key, 26)

    def nrm(k, shape, scale):
        return jax.random.normal(k, shape, jnp.float32) * scale

    cache_shape = (DEPTH, DEC_BATCH, WINDOW, N_KV_HEADS, HEAD_DIM)
    return {
        "x_prompt": nrm(ks[0], (BATCH, SEQ, D_MODEL), 1.0),
        "x_sample": nrm(ks[1], (DEC_BATCH, DEC_SEQ, D_MODEL), 1.0),
        "cache_k_win": nrm(ks[2], cache_shape, 1.0),
        "cache_v_win": nrm(ks[3], cache_shape, 1.0),
        "norm1_g": 1.0 + nrm(ks[4], (DEPTH, D_MODEL), 0.02),
        "w_in": nrm(ks[5], (DEPTH, D_MODEL, D_IN), D_MODEL ** -0.5),
        "q_norm_g": 1.0 + nrm(ks[6], (DEPTH, HEAD_DIM), 0.02),
        "k_norm_g": 1.0 + nrm(ks[7], (DEPTH, HEAD_DIM), 0.02),
        "attn_sinks": nrm(ks[8], (DEPTH, N_HEADS), 0.5),
        "ln_v_g": 1.0 + nrm(ks[9], (DEPTH, SGU_WIDTH), 0.02),
        "ln_v_b": nrm(ks[10], (DEPTH, SGU_WIDTH), 0.02),
        "w_spatial": nrm(ks[11], (DEPTH, SGU_GROUPS, CHUNK, CHUNK), 0.5 * CHUNK ** -0.5),
        "b_spatial": 1.0 + nrm(ks[12], (DEPTH, SGU_GROUPS, CHUNK), 0.1),
        "w_branch_a": nrm(ks[13], (DEPTH, ATTN_WIDTH, D_MODEL), ATTN_WIDTH ** -0.5),
        "w_branch_b": nrm(ks[14], (DEPTH, SGU_WIDTH, D_MODEL), SGU_WIDTH ** -0.5),
        "w_out": nrm(ks[15], (DEPTH, D_MODEL, D_MODEL), D_MODEL ** -0.5),
        "norm2_g": 1.0 + nrm(ks[16], (DEPTH, D_MODEL), 0.02),
        "w_router_group": nrm(ks[17], (DEPTH, D_MODEL, N_GROUPS), D_MODEL ** -0.5),
        "b_router_group": nrm(ks[18], (DEPTH, N_GROUPS), 0.01),
        "w_router_expert": nrm(ks[19], (DEPTH, D_MODEL, N_EXPERTS), D_MODEL ** -0.5),
        "b_router_expert": nrm(ks[20], (DEPTH, N_EXPERTS), 0.01),
        "w_exp_gate": nrm(ks[21], (DEPTH, N_EXPERTS, D_MODEL, D_EXPERT), D_MODEL ** -0.5),
        "w_exp_up": nrm(ks[22], (DEPTH, N_EXPERTS, D_MODEL, D_EXPERT), D_MODEL ** -0.5),
        "w_exp_down": nrm(ks[23], (DEPTH, N_EXPERTS, D_EXPERT, D_MODEL), D_EXPERT ** -0.5),
    }


def reference(x_prompt, x_sample, cache_k_win, cache_v_win, norm1_g, w_in, q_norm_g, k_norm_g,
              attn_sinks, ln_v_g, ln_v_b, w_spatial, b_spatial, w_branch_a, w_branch_b, w_out,
              norm2_g, w_router_group, b_router_group, w_router_expert, b_router_expert,
              w_exp_gate, w_exp_up, w_exp_down):
    yp, ys = x_prompt, x_sample
    kp_list, vp_list, ks_list, vs_list, sgu_list = [], [], [], [], []
    for l in range(DEPTH):
        q, k, v, u, vsg, ga, gb = input_branches(yp, norm1_g[l], w_in[l], q_norm_g[l], k_norm_g[l],
                                                 ln_v_g[l], ln_v_b[l])
        oa, k_new, v_new = attn_prompt(q, k, v, attn_sinks[l])
        ob = sgu_prompt(u, vsg, w_spatial[l], b_spatial[l])
        yp = yp + merge_branches(oa, ob, ga, gb, w_branch_a[l], w_branch_b[l], w_out[l])
        yp = yp + moe_ffn(yp, norm2_g[l], w_router_group[l], b_router_group[l], w_router_expert[l],
                          b_router_expert[l], w_exp_gate[l], w_exp_up[l], w_exp_down[l])
        kp_list.append(k_new)
        vp_list.append(v_new)
        q, k, v, u, vsg, ga, gb = input_branches(ys, norm1_g[l], w_in[l], q_norm_g[l], k_norm_g[l],
                                                 ln_v_g[l], ln_v_b[l])
        oa, k_buf, v_buf = attn_sample(q, k, v, cache_k_win[l], cache_v_win[l], attn_sinks[l])
        ob, v_open = sgu_sample(u, vsg, w_spatial[l], b_spatial[l])
        ys = ys + merge_branches(oa, ob, ga, gb, w_branch_a[l], w_branch_b[l], w_out[l])
        ys = ys + moe_ffn(ys, norm2_g[l], w_router_group[l], b_router_group[l], w_router_expert[l],
                          b_router_expert[l], w_exp_gate[l], w_exp_up[l], w_exp_down[l])
        ks_list.append(k_buf)
        vs_list.append(v_buf)
        sgu_list.append(v_open)
    new_k_prompt = jnp.stack(kp_list)
    new_v_prompt = jnp.stack(vp_list)
    new_k_sample = jnp.stack(ks_list)
    new_v_sample = jnp.stack(vs_list)
    new_vsgu_sample = jnp.stack(sgu_list)
    return (yp, ys, new_k_prompt, new_v_prompt, new_k_sample, new_v_sample, new_vsgu_sample)
```

```python
import functools

import numpy as np
import jax
import jax.numpy as jnp
from jax import lax
from jax.experimental import pallas as pl
from jax.experimental.pallas import tpu as pltpu

F32 = jnp.float32
BF16 = jnp.bfloat16

D_MODEL = 1024
N_HEADS = 8
N_KV_HEADS = 2
HEAD_DIM = 64
WINDOW = 128
ATTN_WIDTH = N_HEADS * HEAD_DIM
KV_WIDTH = N_KV_HEADS * HEAD_DIM
SGU_GROUPS = 4
SGU_WIDTH = D_MODEL // 2
SGU_GROUP_DIM = SGU_WIDTH // SGU_GROUPS
CHUNK = 128
N_GROUPS = 4
EXPERTS_PER_GROUP = 4
N_EXPERTS = N_GROUPS * EXPERTS_PER_GROUP
D_EXPERT = 256
GROUP_HIDDEN = EXPERTS_PER_GROUP * D_EXPERT
D_IN = ATTN_WIDTH + 2 * KV_WIDTH + 2 * SGU_WIDTH + 2 * D_MODEL
EPS = 1e-6
NEG_INF = -1e30

C_Q = 0
C_K = C_Q + ATTN_WIDTH
C_V = C_K + KV_WIDTH
C_U = C_V + KV_WIDTH
C_VS = C_U + SGU_WIDTH
C_GA = C_VS + SGU_WIDTH
C_GB = C_GA + D_MODEL

LANES = 128
ROUTER_LANES = LANES
EXPERT_LANE0 = N_GROUPS
PROMPT_TILE = 512
MOE_TILE = 512
SAMPLE_BATCH_TILE = 8
VMEM_LIMIT = 56 * 1024 * 1024

_SQRT_2_OVER_PI = np.sqrt(2.0 / np.pi).astype(np.float32)


def _dot(a, b):
    return jnp.dot(a, b, preferred_element_type=F32)


def _dot_nt(a, b):
    return lax.dot_general(a, b, (((1,), (1,)), ((), ())), preferred_element_type=F32)


def _gelu(x):
    cdf = 0.5 * (1.0 + jnp.tanh(_SQRT_2_OVER_PI * (x + 0.044715 * (x * x * x))))
    return x * cdf


def _rms(x, g):
    return x * lax.rsqrt(jnp.mean(x * x, axis=-1, keepdims=True) + EPS) * g


def _head_rms(x, blockdiag, g):
    ms = _dot((x * x).astype(BF16), blockdiag)
    return x * lax.rsqrt(ms + EPS) * g


def _input_branches(x, g1, win_ref, gq, gk, bq, bk, lng, lnb):
    h = _rms(x, g1).astype(BF16)
    qn = _head_rms(_dot(h, win_ref[:, C_Q:C_K]), bq, gq)
    kn = _head_rms(_dot(h, win_ref[:, C_K:C_V]), bk, gk)
    v = _dot(h, win_ref[:, C_V:C_U])
    gu = _gelu(_dot(h, win_ref[:, C_U:C_VS]))
    gv = _gelu(_dot(h, win_ref[:, C_VS:C_GA]))
    mu = jnp.mean(gv, axis=-1, keepdims=True)
    xc = gv - mu
    vsn = xc * lax.rsqrt(jnp.mean(xc * xc, axis=-1, keepdims=True) + EPS) * lng + lnb
    return h, qn, kn, v, gu, vsn


def _head_variants(prev, prev_rot, cur, cur_rot):
    a = jnp.concatenate([prev, cur], axis=0)
    r = jnp.concatenate([prev_rot, cur_rot], axis=0)
    lo = lax.broadcasted_iota(jnp.int32, a.shape, 1) < HEAD_DIM
    zero = jnp.zeros_like(a)
    kv0 = jnp.concatenate([jnp.where(lo, a, zero), jnp.where(lo, zero, r)], axis=0)
    kv1 = jnp.concatenate([jnp.where(lo, r, zero), jnp.where(lo, zero, a)], axis=0)
    return kv0, kv1


def _lane_slabs(x):
    return [x[:, p * LANES:(p + 1) * LANES] for p in range(x.shape[1] // LANES)]


def _attend(q_slabs, kcat, vcat, bias, sinks_ref):
    n_keys = bias.shape[1] // 2
    outs = []
    for p in range(N_HEADS // 2):
        kv = p // (N_HEADS // 2 // N_KV_HEADS)
        s = _dot_nt(q_slabs[p].astype(BF16), kcat[kv]) + bias
        probs = []
        for par in range(2):
            sh = s[:, par * n_keys:(par + 1) * n_keys]
            sink = sinks_ref[2 * p + par]
            m = jnp.maximum(jnp.max(sh, axis=-1, keepdims=True), sink)
            e = jnp.exp(sh - m)
            den = jnp.sum(e, axis=-1, keepdims=True) + jnp.exp(sink - m)
            probs.append((e / den).astype(BF16))
        outs.append(_dot(jnp.concatenate(probs, axis=1), vcat[kv]))
    return outs


def _merge(x, h, oa, ob, win_ref, wa_ref, wb_ref, wout_ref):
    ya = _dot(oa.astype(BF16), wa_ref[...])
    yb = _dot(ob.astype(BF16), wb_ref[...])
    ga = _dot(h, win_ref[:, C_GA:C_GB])
    gb = _dot(h, win_ref[:, C_GB:D_IN])
    hm = jax.nn.sigmoid(ga) * ya + jax.nn.sigmoid(gb) * yb
    return x + _dot(hm.astype(BF16), wout_ref[...])


def _prompt_kernel(x_ref, g1_ref, win_ref, gq_ref, gk_ref, bq_ref, bk_ref, sinks_ref, lng_ref,
                   lnb_ref, wsp_ref, bsp_ref, wa_ref, wb_ref, wout_ref, bias_ref,
                   y_ref, kwin_ref, vwin_ref,
                   kprev, kprev_rot, vprev, vprev_rot):
    j = pl.program_id(1)

    @pl.when(j == 0)
    def _():
        for r in (kprev, kprev_rot, vprev, vprev_rot):
            r[...] = jnp.zeros_like(r)

    x = x_ref[0]
    h, qn, kn, v, gu, vsn = _input_branches(
        x, g1_ref[...], win_ref, gq_ref[...], gk_ref[...], bq_ref[...], bk_ref[...],
        lng_ref[...], lnb_ref[...])

    kb = kn.astype(BF16)
    kr = pltpu.roll(kn, HEAD_DIM, 1).astype(BF16)
    vb = v.astype(BF16)
    vr = pltpu.roll(v, HEAD_DIM, 1).astype(BF16)
    vsb = vsn.astype(BF16)

    n_blocks = x.shape[0] // WINDOW
    oa_blocks, ob_blocks = [], []
    pk, pkr, pv, pvr = kprev[...], kprev_rot[...], vprev[...], vprev_rot[...]
    for i in range(n_blocks):
        rows = slice(i * WINDOW, (i + 1) * WINDOW)
        ck, ckr, cv, cvr = kb[rows], kr[rows], vb[rows], vr[rows]
        kcat = _head_variants(pk, pkr, ck, ckr)
        vcat = _head_variants(pv, pvr, cv, cvr)
        if i == 0:
            bias = bias_ref[jnp.where(j == 0, 1, 0)]
        else:
            bias = bias_ref[0]
        oa_blocks.append(jnp.concatenate(
            _attend(_lane_slabs(qn[rows]), kcat, vcat, bias, sinks_ref), axis=1))
        mixed = jnp.concatenate(
            [_dot(wsp_ref[g], vsb[rows, g * SGU_GROUP_DIM:(g + 1) * SGU_GROUP_DIM])
             for g in range(SGU_GROUPS)], axis=1) + bsp_ref[...]
        ob_blocks.append(gu[rows] * mixed)
        pk, pkr, pv, pvr = ck, ckr, cv, cvr

    kprev[...] = pk
    kprev_rot[...] = pkr
    vprev[...] = pv
    vprev_rot[...] = pvr

    @pl.when(j == pl.num_programs(1) - 1)
    def _():
        kwin_ref[0] = kn[(n_blocks - 1) * WINDOW:]
        vwin_ref[0] = v[(n_blocks - 1) * WINDOW:]

    oa = jnp.concatenate(oa_blocks, axis=0)
    ob = jnp.concatenate(ob_blocks, axis=0)
    y_ref[0] = _merge(x, h, oa, ob, win_ref, wa_ref, wb_ref, wout_ref)


def _const_spec(shape, single_buffer=True):
    nd = len(shape)
    mode = pl.Buffered(1) if single_buffer else None
    return pl.BlockSpec(shape, lambda *_: (0,) * nd, pipeline_mode=mode)


def _smem_spec():
    return pl.BlockSpec(memory_space=pltpu.SMEM)


def _prompt_mixer(x, p):
    b, s, d = x.shape
    ts = PROMPT_TILE
    assert s % ts == 0 and ts % WINDOW == 0
    in_specs = [
        pl.BlockSpec((1, ts, d), lambda bi, j: (bi, j, 0)),
        _const_spec((1, d)), _const_spec((d, D_IN)),
        _const_spec((1, ATTN_WIDTH)), _const_spec((1, KV_WIDTH)),
        _const_spec((ATTN_WIDTH, ATTN_WIDTH)), _const_spec((KV_WIDTH, KV_WIDTH)),
        _smem_spec(),
        _const_spec((1, SGU_WIDTH)), _const_spec((1, SGU_WIDTH)),
        _const_spec((SGU_GROUPS, CHUNK, CHUNK)), _const_spec((CHUNK, SGU_WIDTH)),
        _const_spec((ATTN_WIDTH, d)), _const_spec((SGU_WIDTH, d)), _const_spec((d, d)),
        _const_spec((2, WINDOW, 4 * WINDOW)),
    ]
    out_specs = [
        pl.BlockSpec((1, ts, d), lambda bi, j: (bi, j, 0)),
        pl.BlockSpec((1, WINDOW, KV_WIDTH), lambda bi, j: (bi, 0, 0)),
        pl.BlockSpec((1, WINDOW, KV_WIDTH), lambda bi, j: (bi, 0, 0)),
    ]
    out_shape = [
        jax.ShapeDtypeStruct((b, s, d), F32),
        jax.ShapeDtypeStruct((b, WINDOW, KV_WIDTH), F32),
        jax.ShapeDtypeStruct((b, WINDOW, KV_WIDTH), F32),
    ]
    return pl.pallas_call(
        _prompt_kernel,
        grid=(b, s // ts),
        in_specs=in_specs, out_specs=out_specs, out_shape=out_shape,
        scratch_shapes=[pltpu.VMEM((WINDOW, KV_WIDTH), BF16)] * 4,
        compiler_params=pltpu.CompilerParams(
            dimension_semantics=("arbitrary", "arbitrary"), vmem_limit_bytes=VMEM_LIMIT),
        name="prompt_mixer",
    )(x, p["g1"], p["w_in"], p["gq"], p["gk"], p["bq"], p["bk"], p["sinks"], p["lng"], p["lnb"],
      p["wsp"], p["bsp"], p["wa"], p["wb"], p["wout"], p["bias_prompt"])


def _sample_in_kernel(x_ref, g1_ref, win_ref, gq_ref, gk_ref, bq_ref, bk_ref, lng_ref, lnb_ref,
                      coef_ref, sbias_ref,
                      q_ref, k_ref, v_ref, ob_ref, vs_ref, sga_ref, sgb_ref):
    x = x_ref[...]
    h, qn, kn, v, gu, vsn = _input_branches(
        x, g1_ref[...], win_ref, gq_ref[...], gk_ref[...], bq_ref[...], bk_ref[...],
        lng_ref[...], lnb_ref[...])
    for p, slab in enumerate(_lane_slabs(qn)):
        q_ref[p] = slab
    k_ref[...] = kn
    v_ref[...] = v
    vs_ref[...] = vsn
    sga_ref[...] = jax.nn.sigmoid(_dot(h, win_ref[:, C_GA:C_GB]))
    sgb_ref[...] = jax.nn.sigmoid(_dot(h, win_ref[:, C_GB:D_IN]))
    nb = x.shape[0] // 4
    for t in range(4):
        mixed = sbias_ref[t:t + 1, :]
        for jj in range(t + 1):
            mixed = mixed + coef_ref[4 * t + jj:4 * t + jj + 1, :] * vsn[jj * nb:(jj + 1) * nb]
        ob_ref[t * nb:(t + 1) * nb, :] = gu[t * nb:(t + 1) * nb] * mixed


def _sample_in(x, p):
    n, d = x.shape
    shapes = [(ATTN_WIDTH // LANES, n, LANES), (n, KV_WIDTH), (n, KV_WIDTH), (n, SGU_WIDTH),
              (n, SGU_WIDTH), (n, d), (n, d)]
    return pl.pallas_call(
        _sample_in_kernel,
        grid=(1,),
        in_specs=[_const_spec((n, d)), _const_spec((1, d)), _const_spec((d, D_IN)),
                  _const_spec((1, ATTN_WIDTH)), _const_spec((1, KV_WIDTH)),
                  _const_spec((ATTN_WIDTH, ATTN_WIDTH)), _const_spec((KV_WIDTH, KV_WIDTH)),
                  _const_spec((1, SGU_WIDTH)), _const_spec((1, SGU_WIDTH)),
                  _const_spec((16, SGU_WIDTH)), _const_spec((4, SGU_WIDTH))],
        out_specs=[_const_spec(sh, False) for sh in shapes],
        out_shape=[jax.ShapeDtypeStruct(sh, F32) for sh in shapes],
        compiler_params=pltpu.CompilerParams(
            dimension_semantics=("arbitrary",), vmem_limit_bytes=VMEM_LIMIT),
        name="sample_in",
    )(x, p["g1"], p["w_in"], p["gq"], p["gk"], p["bq"], p["bk"], p["lng"], p["lnb"],
      p["coef_s"], p["bias_sgu_s"])


def _sample_attn_kernel(q_ref, k_ref, v_ref, ck_ref, cv_ref, sinks_ref, bias_ref,
                        oa_ref, nk_ref, nv_ref):
    i = pl.program_id(0)
    n_slabs = q_ref.shape[0]
    half_rows = q_ref.shape[1] // 8
    pad = jnp.zeros((WINDOW - 8, KV_WIDTH), BF16)
    row_half = lax.broadcasted_iota(jnp.int32, (8, LANES), 0) % 2

    def body(bb, carry):
        b_lo = i * SAMPLE_BATCH_TILE + bb
        rows = pl.ds(b_lo, 8, stride=half_rows)
        q8 = [q_ref[p, rows, :] for p in range(n_slabs)]
        k8 = k_ref[rows, :]
        v8 = v_ref[rows, :]
        ck_new = jnp.concatenate([k8.astype(BF16), pad], axis=0)
        ckr_new = jnp.concatenate([pltpu.roll(k8, HEAD_DIM, 1).astype(BF16), pad], axis=0)
        cv_new = jnp.concatenate([v8.astype(BF16), pad], axis=0)
        cvr_new = jnp.concatenate([pltpu.roll(v8, HEAD_DIM, 1).astype(BF16), pad], axis=0)
        o8 = None
        for hf in range(2):
            kc = ck_ref[hf, bb]
            vc = cv_ref[hf, bb]
            kcat = _head_variants(kc.astype(BF16), pltpu.roll(kc, HEAD_DIM, 1).astype(BF16),
                                  ck_new, ckr_new)
            vcat = _head_variants(vc.astype(BF16), pltpu.roll(vc, HEAD_DIM, 1).astype(BF16),
                                  cv_new, cvr_new)
            o_hf = _attend(q8, kcat, vcat, bias_ref[hf], sinks_ref)
            o8 = o_hf if hf == 0 else [jnp.where(row_half == 0, a, b) for a, b in zip(o8, o_hf)]
            nk_ref[hf, bb] = pltpu.roll(kc, WINDOW - 4, 0)
            nv_ref[hf, bb] = pltpu.roll(vc, WINDOW - 4, 0)
            for t in range(4):
                nk_ref[hf, bb, WINDOW - 4 + t:WINDOW - 3 + t, :] = k8[2 * t + hf:2 * t + hf + 1, :]
                nv_ref[hf, bb, WINDOW - 4 + t:WINDOW - 3 + t, :] = v8[2 * t + hf:2 * t + hf + 1, :]
        for p in range(n_slabs):
            oa_ref[p, rows, :] = o8[p]
        return carry

    lax.fori_loop(0, SAMPLE_BATCH_TILE, body, 0)


def _sample_attn(q, k, v, ck, cv, p):
    n_slabs, n, _ = q.shape
    nb = n // 4
    half = nb // 2
    bt = SAMPLE_BATCH_TILE
    assert half % bt == 0
    ck4 = ck.reshape(2, half, WINDOW, KV_WIDTH)
    cv4 = cv.reshape(2, half, WINDOW, KV_WIDTH)
    cache_spec = pl.BlockSpec((2, bt, WINDOW, KV_WIDTH), lambda i: (0, i, 0, 0))
    return pl.pallas_call(
        _sample_attn_kernel,
        grid=(half // bt,),
        in_specs=[_const_spec((n_slabs, n, LANES)), _const_spec((n, KV_WIDTH)),
                  _const_spec((n, KV_WIDTH)), cache_spec, cache_spec, _smem_spec(),
                  _const_spec((2, 8, 4 * WINDOW))],
        out_specs=[_const_spec((n_slabs, n, LANES), False), cache_spec, cache_spec],
        out_shape=[jax.ShapeDtypeStruct((n_slabs, n, LANES), F32),
                   jax.ShapeDtypeStruct(ck4.shape, F32), jax.ShapeDtypeStruct(cv4.shape, F32)],
        compiler_params=pltpu.CompilerParams(
            dimension_semantics=("arbitrary",), vmem_limit_bytes=VMEM_LIMIT),
        name="sample_attn",
    )(q, k, v, ck4, cv4, p["sinks"], p["bias_sample"])


def _sample_merge_kernel(x_ref, oa_ref, ob_ref, sga_ref, sgb_ref, wa_ref, wb_ref, wout_ref, y_ref):
    oa = jnp.concatenate([oa_ref[p] for p in range(oa_ref.shape[0])], axis=1)
    ya = _dot(oa.astype(BF16), wa_ref[...])
    yb = _dot(ob_ref[...].astype(BF16), wb_ref[...])
    hm = sga_ref[...] * ya + sgb_ref[...] * yb
    y_ref[...] = x_ref[...] + _dot(hm.astype(BF16), wout_ref[...])


def _sample_merge(x, oa, ob, sga, sgb, p):
    n, d = x.shape
    return pl.pallas_call(
        _sample_merge_kernel,
        grid=(1,),
        in_specs=[_const_spec((n, d)), _const_spec(oa.shape), _const_spec((n, SGU_WIDTH)),
                  _const_spec((n, d)), _const_spec((n, d)),
                  _const_spec((ATTN_WIDTH, d)), _const_spec((SGU_WIDTH, d)), _const_spec((d, d))],
        out_specs=_const_spec((n, d), False),
        out_shape=jax.ShapeDtypeStruct((n, d), F32),
        compiler_params=pltpu.CompilerParams(
            dimension_semantics=("arbitrary",), vmem_limit_bytes=VMEM_LIMIT),
        name="sample_merge",
    )(x, oa, ob, sga, sgb, p["wa"], p["wb"], p["wout"])


def _route(logits):
    lane = lax.broadcasted_iota(jnp.int32, logits.shape, 1).astype(F32)
    far = float(ROUTER_LANES)
    glm = jnp.where(lane < N_GROUPS, logits, NEG_INF)
    gmax = jnp.max(glm, axis=-1, keepdims=True)
    gidx = jnp.min(jnp.where(glm == gmax, lane, far), axis=-1, keepdims=True)
    gw = 1.0 / jnp.sum(jnp.exp(glm - gmax), axis=-1, keepdims=True)
    first = EXPERT_LANE0 + EXPERTS_PER_GROUP * gidx
    sel = (lane >= first) & (lane < first + EXPERTS_PER_GROUP)
    el = jnp.where(sel, logits, NEG_INF)
    t1 = jnp.max(el, axis=-1, keepdims=True)
    i1 = jnp.min(jnp.where(el == t1, lane, far), axis=-1, keepdims=True)
    el2 = jnp.where(lane == i1, NEG_INF, el)
    t2 = jnp.max(el2, axis=-1, keepdims=True)
    i2 = jnp.min(jnp.where(el2 == t2, lane, far), axis=-1, keepdims=True)
    e2 = jnp.exp(t2 - t1)
    den = 1.0 + e2
    w1 = (1.0 / den) * gw
    w2 = (e2 / den) * gw
    return jnp.where(lane == i1, w1, 0.0) + jnp.where(lane == i2, w2, 0.0)


def _split_bf16(x):
    hi = x.astype(BF16)
    lo = (x - hi.astype(F32)).astype(BF16)
    return hi, lo


def _moe_kernel(y_ref, g2_ref, wrh_ref, wrl_ref, br_ref, wg_ref, wu_ref, wd_ref, ex_ref, o_ref):
    y = y_ref[...]
    hf = _rms(y, g2_ref[...])
    hb, hl = _split_bf16(hf)
    logits = _dot(hb, wrh_ref[...]) + (_dot(hb, wrl_ref[...]) + _dot(hl, wrh_ref[...])) + br_ref[...]
    cw = _route(logits)
    lane = lax.broadcasted_iota(jnp.int32, cw.shape, 1)
    acc = jnp.zeros_like(y)
    for g in range(N_GROUPS):
        lane0 = EXPERT_LANE0 + EXPERTS_PER_GROUP * g
        cwg = jnp.where((lane >= lane0) & (lane < lane0 + EXPERTS_PER_GROUP), cw, 0.0)
        ch, cl = _split_bf16(cwg)
        cexp = _dot(ch, ex_ref[...]) + _dot(cl, ex_ref[...])
        a = _dot(hb, wg_ref[g])
        u = _dot(hb, wu_ref[g])
        hid = (a * jax.nn.sigmoid(a)) * u * cexp
        acc = acc + _dot(hid.astype(BF16), wd_ref[g])
    o_ref[...] = y + acc


def _moe(y, p):
    n, d = y.shape
    tm = min(MOE_TILE, n)
    assert n % tm == 0
    row_spec = pl.BlockSpec((tm, d), lambda i: (i, 0))
    one = pl.Buffered(1)
    wspec = pl.BlockSpec((N_GROUPS, d, GROUP_HIDDEN), lambda i: (0, 0, 0), pipeline_mode=one)
    wdspec = pl.BlockSpec((N_GROUPS, GROUP_HIDDEN, d), lambda i: (0, 0, 0), pipeline_mode=one)
    return pl.pallas_call(
        _moe_kernel,
        grid=(n // tm,),
        in_specs=[row_spec, _const_spec((1, d)), _const_spec((d, ROUTER_LANES)),
                  _const_spec((d, ROUTER_LANES)), _const_spec((1, ROUTER_LANES)),
                  wspec, wspec, wdspec, _const_spec((ROUTER_LANES, GROUP_HIDDEN))],
        out_specs=row_spec,
        out_shape=jax.ShapeDtypeStruct((n, d), F32),
        compiler_params=pltpu.CompilerParams(
            dimension_semantics=("arbitrary",), vmem_limit_bytes=VMEM_LIMIT),
        name="expert_mixer",
    )(y, p["g2"], p["wr_hi"], p["wr_lo"], p["br"], p["wg"], p["wu"], p["wd"], p["expand"])


def _window_bias(n_q, q_tok, key_prev_ok, key_cur_ok):
    half = np.concatenate([key_prev_ok, key_cur_ok], axis=1)
    ok = np.concatenate([half, half], axis=1)
    return np.where(ok, 0.0, NEG_INF).astype(np.float32)


def _prompt_bias():
    t = np.arange(WINDOW)[:, None]
    s = np.arange(WINDOW)[None, :]
    prev_ok = s > t
    cur_ok = s <= t
    normal = _window_bias(WINDOW, t, prev_ok, cur_ok)
    first = _window_bias(WINDOW, t, np.zeros_like(prev_ok), cur_ok)
    return np.stack([normal, first])


def _sample_bias():
    out = []
    r = np.arange(8)[:, None]
    t = r // 2
    s = np.arange(WINDOW)[None, :]
    prev_ok = s > t
    for hf in range(2):
        c = s
        cur_ok = (c < 8) & (c % 2 == hf) & (c // 2 <= t)
        out.append(_window_bias(8, t, np.broadcast_to(prev_ok, (8, WINDOW)), cur_ok))
    return np.stack(out)


def _block_diag_mean(width):
    idx = np.arange(width) // HEAD_DIM
    return (idx[:, None] == idx[None, :]).astype(np.float32) / HEAD_DIM


def _expand_matrix():
    ex = np.zeros((ROUTER_LANES, GROUP_HIDDEN), np.float32)
    for g in range(N_GROUPS):
        for e in range(EXPERTS_PER_GROUP):
            ex[EXPERT_LANE0 + EXPERTS_PER_GROUP * g + e, e * D_EXPERT:(e + 1) * D_EXPERT] = 1.0
    return ex


def _prepare(norm1_g, w_in, q_norm_g, k_norm_g, attn_sinks, ln_v_g, ln_v_b, w_spatial, b_spatial,
             w_branch_a, w_branch_b, w_out, norm2_g, w_router_group, b_router_group,
             w_router_expert, b_router_expert, w_exp_gate, w_exp_up, w_exp_down):
    p = {}
    p["g1"] = norm1_g.reshape(1, D_MODEL)
    p["w_in"] = w_in.astype(BF16)
    p["gq"] = (jnp.tile(q_norm_g, N_HEADS) * (HEAD_DIM ** -0.5)).reshape(1, ATTN_WIDTH)
    p["gk"] = jnp.tile(k_norm_g, N_KV_HEADS).reshape(1, KV_WIDTH)
    p["bq"] = jnp.asarray(_block_diag_mean(ATTN_WIDTH), BF16)
    p["bk"] = jnp.asarray(_block_diag_mean(KV_WIDTH), BF16)
    p["sinks"] = attn_sinks.astype(F32)
    p["lng"] = ln_v_g.reshape(1, SGU_WIDTH)
    p["lnb"] = ln_v_b.reshape(1, SGU_WIDTH)
    tril = jnp.tril(jnp.ones((CHUNK, CHUNK), F32))
    wsp = w_spatial * tril[None]
    p["wsp"] = wsp.astype(BF16)
    p["bsp"] = jnp.repeat(b_spatial.T, SGU_GROUP_DIM, axis=1)
    w4 = wsp[:, :4, :4]
    p["coef_s"] = jnp.repeat(jnp.transpose(w4, (1, 2, 0)).reshape(16, SGU_GROUPS),
                             SGU_GROUP_DIM, axis=1)
    p["bias_sgu_s"] = jnp.repeat(b_spatial[:, :4].T, SGU_GROUP_DIM, axis=1)
    p["wa"] = w_branch_a.astype(BF16)
    p["wb"] = w_branch_b.astype(BF16)
    p["wout"] = w_out.astype(BF16)
    p["g2"] = norm2_g.reshape(1, D_MODEL)
    wr = jnp.concatenate(
        [w_router_group, w_router_expert,
         jnp.zeros((D_MODEL, ROUTER_LANES - N_GROUPS - N_EXPERTS), F32)], axis=1)
    p["wr_hi"], p["wr_lo"] = _split_bf16(wr)
    p["br"] = jnp.concatenate(
        [b_router_group, b_router_expert,
         jnp.zeros((ROUTER_LANES - N_GROUPS - N_EXPERTS,), F32)]).reshape(1, ROUTER_LANES)

    def pack_in(w):
        w = w.reshape(N_GROUPS, EXPERTS_PER_GROUP, D_MODEL, D_EXPERT)
        return jnp.transpose(w, (0, 2, 1, 3)).reshape(N_GROUPS, D_MODEL, GROUP_HIDDEN).astype(BF16)

    p["wg"] = pack_in(w_exp_gate)
    p["wu"] = pack_in(w_exp_up)
    p["wd"] = w_exp_down.reshape(N_GROUPS, GROUP_HIDDEN, D_MODEL).astype(BF16)
    p["expand"] = jnp.asarray(_expand_matrix(), BF16)
    p["bias_prompt"] = jnp.asarray(_prompt_bias())
    p["bias_sample"] = jnp.asarray(_sample_bias())
    return p


def kernel(x_prompt, x_sample, cache_k_win, cache_v_win, norm1_g, w_in, q_norm_g, k_norm_g, attn_sinks, ln_v_g, ln_v_b, w_spatial, b_spatial, w_branch_a, w_branch_b, w_out, norm2_g, w_router_group, b_router_group, w_router_expert, b_router_expert, w_exp_gate, w_exp_up, w_exp_down):
    depth = norm1_g.shape[0]
    assert depth == 1
    batch, seq, d = x_prompt.shape
    dec_batch, dec_seq, _ = x_sample.shape
    assert dec_seq == 4 and d == D_MODEL
    p = _prepare(*(a[0] for a in (
        norm1_g, w_in, q_norm_g, k_norm_g, attn_sinks, ln_v_g, ln_v_b, w_spatial, b_spatial,
        w_branch_a, w_branch_b, w_out, norm2_g, w_router_group, b_router_group,
        w_router_expert, b_router_expert, w_exp_gate, w_exp_up, w_exp_down)))

    y1p, kwin, vwin = _prompt_mixer(x_prompt, p)
    yp = _moe(y1p.reshape(batch * seq, d), p).reshape(batch, seq, d)

    xs = jnp.transpose(x_sample, (1, 0, 2)).reshape(dec_seq * dec_batch, d)
    q, k, v, ob, vs, sga, sgb = _sample_in(xs, p)
    ck = cache_k_win[0].reshape(dec_batch, WINDOW, KV_WIDTH)
    cv = cache_v_win[0].reshape(dec_batch, WINDOW, KV_WIDTH)
    oa, nk, nv = _sample_attn(q, k, v, ck, cv, p)
    y1s = _sample_merge(xs, oa, ob, sga, sgb, p)
    ys = _moe(y1s, p)
    ys = jnp.transpose(ys.reshape(dec_seq, dec_batch, d), (1, 0, 2))
    vs_out = jnp.transpose(vs.reshape(dec_seq, dec_batch, SGU_GROUPS, SGU_GROUP_DIM), (1, 0, 2, 3))

    kv_shape = (WINDOW, N_KV_HEADS, HEAD_DIM)
    return (yp, ys,
            kwin.reshape(1, batch, *kv_shape), vwin.reshape(1, batch, *kv_shape),
            nk.reshape(1, dec_batch, *kv_shape), nv.reshape(1, dec_batch, *kv_shape),
            vs_out[None])
```

```python
import functools

import numpy as np
import jax
import jax.numpy as jnp
from jax import lax
from jax.experimental import pallas as pl
from jax.experimental.pallas import tpu as pltpu

F32 = jnp.float32
BF16 = jnp.bfloat16

D_MODEL = 1024
N_HEADS = 8
N_KV_HEADS = 2
HEAD_DIM = 64
WINDOW = 128
ATTN_WIDTH = N_HEADS * HEAD_DIM
KV_WIDTH = N_KV_HEADS * HEAD_DIM
SGU_GROUPS = 4
SGU_WIDTH = D_MODEL // 2
SGU_GROUP_DIM = SGU_WIDTH // SGU_GROUPS
CHUNK = 128
N_GROUPS = 4
EXPERTS_PER_GROUP = 4
N_EXPERTS = N_GROUPS * EXPERTS_PER_GROUP
D_EXPERT = 256
GROUP_HIDDEN = EXPERTS_PER_GROUP * D_EXPERT
D_IN = ATTN_WIDTH + 2 * KV_WIDTH + 2 * SGU_WIDTH + 2 * D_MODEL
EPS = 1e-6
NEG_INF = -1e30

C_Q = 0
C_K = C_Q + ATTN_WIDTH
C_V = C_K + KV_WIDTH
C_U = C_V + KV_WIDTH
C_VS = C_U + SGU_WIDTH
C_GA = C_VS + SGU_WIDTH
C_GB = C_GA + D_MODEL

LANES = 128
ROUTER_LANES = LANES
EXPERT_LANE0 = N_GROUPS
PROMPT_TILE = 512
MOE_TILE = 512
MOE_CHUNK = 128
MOE_MAX_CHUNKS = (MOE_TILE + N_GROUPS * (MOE_CHUNK - 1)) // MOE_CHUNK
MOE_SLOTS = MOE_MAX_CHUNKS * MOE_CHUNK
SAMPLE_BATCH_TILE = 8
VMEM_LIMIT = 56 * 1024 * 1024

_SQRT_2_OVER_PI = np.sqrt(2.0 / np.pi).astype(np.float32)


def _dot(a, b):
    return jnp.dot(a, b, preferred_element_type=F32)


def _dot_nt(a, b):
    return lax.dot_general(a, b, (((1,), (1,)), ((), ())), preferred_element_type=F32)


def _gelu(x):
    cdf = 0.5 * (1.0 + jnp.tanh(_SQRT_2_OVER_PI * (x + 0.044715 * (x * x * x))))
    return x * cdf


def _rms(x, g):
    return x * lax.rsqrt(jnp.mean(x * x, axis=-1, keepdims=True) + EPS) * g


def _head_rms(x, blockdiag, g):
    ms = _dot((x * x).astype(BF16), blockdiag)
    return x * lax.rsqrt(ms + EPS) * g


def _input_branches(x, g1, win_ref, gq, gk, bq, bk, lng, lnb):
    h = _rms(x, g1).astype(BF16)
    qn = _head_rms(_dot(h, win_ref[:, C_Q:C_K]), bq, gq)
    kn = _head_rms(_dot(h, win_ref[:, C_K:C_V]), bk, gk)
    v = _dot(h, win_ref[:, C_V:C_U])
    gu = _gelu(_dot(h, win_ref[:, C_U:C_VS]))
    gv = _gelu(_dot(h, win_ref[:, C_VS:C_GA]))
    mu = jnp.mean(gv, axis=-1, keepdims=True)
    xc = gv - mu
    vsn = xc * lax.rsqrt(jnp.mean(xc * xc, axis=-1, keepdims=True) + EPS) * lng + lnb
    return h, qn, kn, v, gu, vsn


def _head_variants(prev, prev_rot, cur, cur_rot):
    a = jnp.concatenate([prev, cur], axis=0)
    r = jnp.concatenate([prev_rot, cur_rot], axis=0)
    lo = lax.broadcasted_iota(jnp.int32, a.shape, 1) < HEAD_DIM
    zero = jnp.zeros_like(a)
    kv0 = jnp.concatenate([jnp.where(lo, a, zero), jnp.where(lo, zero, r)], axis=0)
    kv1 = jnp.concatenate([jnp.where(lo, r, zero), jnp.where(lo, zero, a)], axis=0)
    return kv0, kv1


def _lane_slabs(x):
    return [x[:, p * LANES:(p + 1) * LANES] for p in range(x.shape[1] // LANES)]


def _attend(q_slabs, kcat, vcat, bias, sinks_ref):
    n_keys = bias.shape[1] // 2
    outs = []
    for p in range(N_HEADS // 2):
        kv = p // (N_HEADS // 2 // N_KV_HEADS)
        s = _dot_nt(q_slabs[p].astype(BF16), kcat[kv]) + bias
        probs = []
        for par in range(2):
            sh = s[:, par * n_keys:(par + 1) * n_keys]
            sink = sinks_ref[2 * p + par]
            m = jnp.maximum(jnp.max(sh, axis=-1, keepdims=True), sink)
            e = jnp.exp(sh - m)
            den = jnp.sum(e, axis=-1, keepdims=True) + jnp.exp(sink - m)
            probs.append((e / den).astype(BF16))
        outs.append(_dot(jnp.concatenate(probs, axis=1), vcat[kv]))
    return outs


def _merge(x, h, oa, ob, win_ref, wa_ref, wb_ref, wout_ref):
    ya = _dot(oa.astype(BF16), wa_ref[...])
    yb = _dot(ob.astype(BF16), wb_ref[...])
    ga = _dot(h, win_ref[:, C_GA:C_GB])
    gb = _dot(h, win_ref[:, C_GB:D_IN])
    hm = jax.nn.sigmoid(ga) * ya + jax.nn.sigmoid(gb) * yb
    return x + _dot(hm.astype(BF16), wout_ref[...])


def _prompt_kernel(x_ref, g1_ref, win_ref, gq_ref, gk_ref, bq_ref, bk_ref, sinks_ref, lng_ref,
                   lnb_ref, wsp_ref, bsp_ref, wa_ref, wb_ref, wout_ref, bias_ref,
                   y_ref, kwin_ref, vwin_ref,
                   kprev, kprev_rot, vprev, vprev_rot):
    j = pl.program_id(1)

    @pl.when(j == 0)
    def _():
        for r in (kprev, kprev_rot, vprev, vprev_rot):
            r[...] = jnp.zeros_like(r)

    x = x_ref[0]
    h, qn, kn, v, gu, vsn = _input_branches(
        x, g1_ref[...], win_ref, gq_ref[...], gk_ref[...], bq_ref[...], bk_ref[...],
        lng_ref[...], lnb_ref[...])

    kb = kn.astype(BF16)
    kr = pltpu.roll(kn, HEAD_DIM, 1).astype(BF16)
    vb = v.astype(BF16)
    vr = pltpu.roll(v, HEAD_DIM, 1).astype(BF16)
    vsb = vsn.astype(BF16)

    n_blocks = x.shape[0] // WINDOW
    oa_blocks, ob_blocks = [], []
    pk, pkr, pv, pvr = kprev[...], kprev_rot[...], vprev[...], vprev_rot[...]
    for i in range(n_blocks):
        rows = slice(i * WINDOW, (i + 1) * WINDOW)
        ck, ckr, cv, cvr = kb[rows], kr[rows], vb[rows], vr[rows]
        kcat = _head_variants(pk, pkr, ck, ckr)
        vcat = _head_variants(pv, pvr, cv, cvr)
        if i == 0:
            bias = bias_ref[jnp.where(j == 0, 1, 0)]
        else:
            bias = bias_ref[0]
        oa_blocks.append(jnp.concatenate(
            _attend(_lane_slabs(qn[rows]), kcat, vcat, bias, sinks_ref), axis=1))
        mixed = jnp.concatenate(
            [_dot(wsp_ref[g], vsb[rows, g * SGU_GROUP_DIM:(g + 1) * SGU_GROUP_DIM])
             for g in range(SGU_GROUPS)], axis=1) + bsp_ref[...]
        ob_blocks.append(gu[rows] * mixed)
        pk, pkr, pv, pvr = ck, ckr, cv, cvr

    kprev[...] = pk
    kprev_rot[...] = pkr
    vprev[...] = pv
    vprev_rot[...] = pvr

    @pl.when(j == pl.num_programs(1) - 1)
    def _():
        kwin_ref[0] = kn[(n_blocks - 1) * WINDOW:]
        vwin_ref[0] = v[(n_blocks - 1) * WINDOW:]

    oa = jnp.concatenate(oa_blocks, axis=0)
    ob = jnp.concatenate(ob_blocks, axis=0)
    y_ref[0] = _merge(x, h, oa, ob, win_ref, wa_ref, wb_ref, wout_ref)


def _const_spec(shape, single_buffer=True):
    nd = len(shape)
    mode = pl.Buffered(1) if single_buffer else None
    return pl.BlockSpec(shape, lambda *_: (0,) * nd, pipeline_mode=mode)


def _smem_spec():
    return pl.BlockSpec(memory_space=pltpu.SMEM)


def _prompt_mixer(x, p):
    b, s, d = x.shape
    ts = PROMPT_TILE
    assert s % ts == 0 and ts % WINDOW == 0
    in_specs = [
        pl.BlockSpec((1, ts, d), lambda bi, j: (bi, j, 0)),
        _const_spec((1, d)), _const_spec((d, D_IN)),
        _const_spec((1, ATTN_WIDTH)), _const_spec((1, KV_WIDTH)),
        _const_spec((ATTN_WIDTH, ATTN_WIDTH)), _const_spec((KV_WIDTH, KV_WIDTH)),
        _smem_spec(),
        _const_spec((1, SGU_WIDTH)), _const_spec((1, SGU_WIDTH)),
        _const_spec((SGU_GROUPS, CHUNK, CHUNK)), _const_spec((CHUNK, SGU_WIDTH)),
        _const_spec((ATTN_WIDTH, d)), _const_spec((SGU_WIDTH, d)), _const_spec((d, d)),
        _const_spec((2, WINDOW, 4 * WINDOW)),
    ]
    out_specs = [
        pl.BlockSpec((1, ts, d), lambda bi, j: (bi, j, 0)),
        pl.BlockSpec((1, WINDOW, KV_WIDTH), lambda bi, j: (bi, 0, 0)),
        pl.BlockSpec((1, WINDOW, KV_WIDTH), lambda bi, j: (bi, 0, 0)),
    ]
    out_shape = [
        jax.ShapeDtypeStruct((b, s, d), F32),
        jax.ShapeDtypeStruct((b, WINDOW, KV_WIDTH), F32),
        jax.ShapeDtypeStruct((b, WINDOW, KV_WIDTH), F32),
    ]
    return pl.pallas_call(
        _prompt_kernel,
        grid=(b, s // ts),
        in_specs=in_specs, out_specs=out_specs, out_shape=out_shape,
        scratch_shapes=[pltpu.VMEM((WINDOW, KV_WIDTH), BF16)] * 4,
        compiler_params=pltpu.CompilerParams(
            dimension_semantics=("arbitrary", "arbitrary"), vmem_limit_bytes=VMEM_LIMIT),
        name="prompt_mixer",
    )(x, p["g1"], p["w_in"], p["gq"], p["gk"], p["bq"], p["bk"], p["sinks"], p["lng"], p["lnb"],
      p["wsp"], p["bsp"], p["wa"], p["wb"], p["wout"], p["bias_prompt"])


def _sample_in_kernel(x_ref, g1_ref, win_ref, gq_ref, gk_ref, bq_ref, bk_ref, lng_ref, lnb_ref,
                      coef_ref, sbias_ref,
                      q_ref, k_ref, v_ref, ob_ref, vs_ref, sga_ref, sgb_ref):
    x = x_ref[...]
    h, qn, kn, v, gu, vsn = _input_branches(
        x, g1_ref[...], win_ref, gq_ref[...], gk_ref[...], bq_ref[...], bk_ref[...],
        lng_ref[...], lnb_ref[...])
    for p, slab in enumerate(_lane_slabs(qn)):
        q_ref[p] = slab
    k_ref[...] = kn
    v_ref[...] = v
    vs_ref[...] = vsn
    sga_ref[...] = jax.nn.sigmoid(_dot(h, win_ref[:, C_GA:C_GB]))
    sgb_ref[...] = jax.nn.sigmoid(_dot(h, win_ref[:, C_GB:D_IN]))
    nb = x.shape[0] // 4
    for t in range(4):
        mixed = sbias_ref[t:t + 1, :]
        for jj in range(t + 1):
            mixed = mixed + coef_ref[4 * t + jj:4 * t + jj + 1, :] * vsn[jj * nb:(jj + 1) * nb]
        ob_ref[t * nb:(t + 1) * nb, :] = gu[t * nb:(t + 1) * nb] * mixed


def _sample_in(x, p):
    n, d = x.shape
    shapes = [(ATTN_WIDTH // LANES, n, LANES), (n, KV_WIDTH), (n, KV_WIDTH), (n, SGU_WIDTH),
              (n, SGU_WIDTH), (n, d), (n, d)]
    return pl.pallas_call(
        _sample_in_kernel,
        grid=(1,),
        in_specs=[_const_spec((n, d)), _const_spec((1, d)), _const_spec((d, D_IN)),
                  _const_spec((1, ATTN_WIDTH)), _const_spec((1, KV_WIDTH)),
                  _const_spec((ATTN_WIDTH, ATTN_WIDTH)), _const_spec((KV_WIDTH, KV_WIDTH)),
                  _const_spec((1, SGU_WIDTH)), _const_spec((1, SGU_WIDTH)),
                  _const_spec((16, SGU_WIDTH)), _const_spec((4, SGU_WIDTH))],
        out_specs=[_const_spec(sh, False) for sh in shapes],
        out_shape=[jax.ShapeDtypeStruct(sh, F32) for sh in shapes],
        compiler_params=pltpu.CompilerParams(
            dimension_semantics=("arbitrary",), vmem_limit_bytes=VMEM_LIMIT),
        name="sample_in",
    )(x, p["g1"], p["w_in"], p["gq"], p["gk"], p["bq"], p["bk"], p["lng"], p["lnb"],
      p["coef_s"], p["bias_sgu_s"])


def _sample_attn_kernel(q_ref, k_ref, v_ref, ck_ref, cv_ref, sinks_ref, bias_ref,
                        oa_ref, nk_ref, nv_ref):
    i = pl.program_id(0)
    n_slabs = q_ref.shape[0]
    half_rows = q_ref.shape[1] // 8
    pad = jnp.zeros((WINDOW - 8, KV_WIDTH), BF16)
    row_half = lax.broadcasted_iota(jnp.int32, (8, LANES), 0) % 2

    def body(bb, carry):
        b_lo = i * SAMPLE_BATCH_TILE + bb
        rows = pl.ds(b_lo, 8, stride=half_rows)
        q8 = [q_ref[p, rows, :] for p in range(n_slabs)]
        k8 = k_ref[rows, :]
        v8 = v_ref[rows, :]
        ck_new = jnp.concatenate([k8.astype(BF16), pad], axis=0)
        ckr_new = jnp.concatenate([pltpu.roll(k8, HEAD_DIM, 1).astype(BF16), pad], axis=0)
        cv_new = jnp.concatenate([v8.astype(BF16), pad], axis=0)
        cvr_new = jnp.concatenate([pltpu.roll(v8, HEAD_DIM, 1).astype(BF16), pad], axis=0)
        o8 = None
        for hf in range(2):
            kc = ck_ref[hf, bb]
            vc = cv_ref[hf, bb]
            kcat = _head_variants(kc.astype(BF16), pltpu.roll(kc, HEAD_DIM, 1).astype(BF16),
                                  ck_new, ckr_new)
            vcat = _head_variants(vc.astype(BF16), pltpu.roll(vc, HEAD_DIM, 1).astype(BF16),
                                  cv_new, cvr_new)
            o_hf = _attend(q8, kcat, vcat, bias_ref[hf], sinks_ref)
            o8 = o_hf if hf == 0 else [jnp.where(row_half == 0, a, b) for a, b in zip(o8, o_hf)]
            nk_ref[hf, bb] = pltpu.roll(kc, WINDOW - 4, 0)
            nv_ref[hf, bb] = pltpu.roll(vc, WINDOW - 4, 0)
            for t in range(4):
                nk_ref[hf, bb, WINDOW - 4 + t:WINDOW - 3 + t, :] = k8[2 * t + hf:2 * t + hf + 1, :]
                nv_ref[hf, bb, WINDOW - 4 + t:WINDOW - 3 + t, :] = v8[2 * t + hf:2 * t + hf + 1, :]
        for p in range(n_slabs):
            oa_ref[p, rows, :] = o8[p]
        return carry

    lax.fori_loop(0, SAMPLE_BATCH_TILE, body, 0)


def _sample_attn(q, k, v, ck, cv, p):
    n_slabs, n, _ = q.shape
    nb = n // 4
    half = nb // 2
    bt = SAMPLE_BATCH_TILE
    assert half % bt == 0
    ck4 = ck.reshape(2, half, WINDOW, KV_WIDTH)
    cv4 = cv.reshape(2, half, WINDOW, KV_WIDTH)
    cache_spec = pl.BlockSpec((2, bt, WINDOW, KV_WIDTH), lambda i: (0, i, 0, 0))
    return pl.pallas_call(
        _sample_attn_kernel,
        grid=(half // bt,),
        in_specs=[_const_spec((n_slabs, n, LANES)), _const_spec((n, KV_WIDTH)),
                  _const_spec((n, KV_WIDTH)), cache_spec, cache_spec, _smem_spec(),
                  _const_spec((2, 8, 4 * WINDOW))],
        out_specs=[_const_spec((n_slabs, n, LANES), False), cache_spec, cache_spec],
        out_shape=[jax.ShapeDtypeStruct((n_slabs, n, LANES), F32),
                   jax.ShapeDtypeStruct(ck4.shape, F32), jax.ShapeDtypeStruct(cv4.shape, F32)],
        compiler_params=pltpu.CompilerParams(
            dimension_semantics=("arbitrary",), vmem_limit_bytes=VMEM_LIMIT),
        name="sample_attn",
    )(q, k, v, ck4, cv4, p["sinks"], p["bias_sample"])


def _sample_merge_kernel(x_ref, oa_ref, ob_ref, sga_ref, sgb_ref, wa_ref, wb_ref, wout_ref, y_ref):
    oa = jnp.concatenate([oa_ref[p] for p in range(oa_ref.shape[0])], axis=1)
    ya = _dot(oa.astype(BF16), wa_ref[...])
    yb = _dot(ob_ref[...].astype(BF16), wb_ref[...])
    hm = sga_ref[...] * ya + sgb_ref[...] * yb
    y_ref[...] = x_ref[...] + _dot(hm.astype(BF16), wout_ref[...])


def _sample_merge(x, oa, ob, sga, sgb, p):
    n, d = x.shape
    return pl.pallas_call(
        _sample_merge_kernel,
        grid=(1,),
        in_specs=[_const_spec((n, d)), _const_spec(oa.shape), _const_spec((n, SGU_WIDTH)),
                  _const_spec((n, d)), _const_spec((n, d)),
                  _const_spec((ATTN_WIDTH, d)), _const_spec((SGU_WIDTH, d)), _const_spec((d, d))],
        out_specs=_const_spec((n, d), False),
        out_shape=jax.ShapeDtypeStruct((n, d), F32),
        compiler_params=pltpu.CompilerParams(
            dimension_semantics=("arbitrary",), vmem_limit_bytes=VMEM_LIMIT),
        name="sample_merge",
    )(x, oa, ob, sga, sgb, p["wa"], p["wb"], p["wout"])


def _route(logits):
    lane = lax.broadcasted_iota(jnp.int32, logits.shape, 1).astype(F32)
    far = float(ROUTER_LANES)
    glm = jnp.where(lane < N_GROUPS, logits, NEG_INF)
    gmax = jnp.max(glm, axis=-1, keepdims=True)
    gidx = jnp.min(jnp.where(glm == gmax, lane, far), axis=-1, keepdims=True)
    gw = 1.0 / jnp.sum(jnp.exp(glm - gmax), axis=-1, keepdims=True)
    first = EXPERT_LANE0 + EXPERTS_PER_GROUP * gidx
    sel = (lane >= first) & (lane < first + EXPERTS_PER_GROUP)
    el = jnp.where(sel, logits, NEG_INF)
    t1 = jnp.max(el, axis=-1, keepdims=True)
    i1 = jnp.min(jnp.where(el == t1, lane, far), axis=-1, keepdims=True)
    el2 = jnp.where(lane == i1, NEG_INF, el)
    t2 = jnp.max(el2, axis=-1, keepdims=True)
    i2 = jnp.min(jnp.where(el2 == t2, lane, far), axis=-1, keepdims=True)
    e2 = jnp.exp(t2 - t1)
    den = 1.0 + e2
    w1 = (1.0 / den) * gw
    w2 = (e2 / den) * gw
    return gidx, jnp.where(lane == i1, w1, 0.0) + jnp.where(lane == i2, w2, 0.0)


def _split_bf16(x):
    hi = x.astype(BF16)
    lo = (x - hi.astype(F32)).astype(BF16)
    return hi, lo


def _moe_kernel(y_ref, g2_ref, wrh_ref, wrl_ref, br_ref, wg_ref, wu_ref, wd_ref, ex_ref,
                ltri_ref, eye_ref, o_ref, hs_ref, cws_ref, os_ref):
    tm = y_ref.shape[0]

    @pl.when(pl.program_id(0) == 0)
    def _():
        os_ref[...] = jnp.zeros_like(os_ref)

    y = y_ref[...]
    hf = _rms(y, g2_ref[...])
    hb, hl = _split_bf16(hf)
    logits = _dot(hb, wrh_ref[...]) + (_dot(hb, wrl_ref[...]) + _dot(hl, wrh_ref[...])) + br_ref[...]
    gidx, cw = _route(logits)

    lane = lax.broadcasted_iota(jnp.int32, (tm, ROUTER_LANES), 1).astype(F32)
    onehot = jnp.where(lane == gidx, 1.0, 0.0)
    before = _dot(ltri_ref[...], onehot.astype(BF16))
    count = jnp.sum(onehot, axis=0, keepdims=True)
    n_chunks = jnp.floor((count + (MOE_CHUNK - 1)) * (1.0 / MOE_CHUNK))
    lane1 = lane[0:1]
    ends = []
    start_vec = jnp.zeros_like(n_chunks)
    end = jnp.zeros((1, 1), F32)
    for g in range(N_GROUPS):
        start_vec = start_vec + jnp.where(lane1 == g, end * MOE_CHUNK, 0.0)
        end = end + jnp.sum(jnp.where(lane1 == g, n_chunks, 0.0), axis=-1, keepdims=True)
        ends.append(end[0, 0].astype(jnp.int32))
    pos = jnp.sum(onehot * (before + start_vec), axis=-1, keepdims=True)

    slot_l = lax.broadcasted_iota(jnp.int32, (tm, MOE_SLOTS), 1).astype(F32)
    unsort = jnp.where(slot_l == pos, 1.0, 0.0).astype(BF16)
    pos_hi = jnp.floor(pos * (1.0 / MOE_CHUNK))
    pos_lo = pos - pos_hi * MOE_CHUNK
    pos_cols = jnp.where(lane == 0.0, pos_hi, jnp.where(lane == 1.0, pos_lo, 0.0)).astype(BF16)
    pos_t = _dot_nt(eye_ref[...], pos_cols)
    pos_row = pos_t[0:1] * MOE_CHUNK + pos_t[1:2]
    slot_s = lax.broadcasted_iota(jnp.int32, (MOE_SLOTS, tm), 0).astype(F32)
    sort = jnp.where(slot_s == pos_row, 1.0, 0.0).astype(BF16)

    ch, cl = _split_bf16(cw)
    sorted_rows = _dot(sort, jnp.concatenate([hb, ch, cl], axis=1))
    hs_ref[...] = sorted_rows[:, :D_MODEL].astype(BF16)
    cws_ref[...] = (sorted_rows[:, D_MODEL:D_MODEL + ROUTER_LANES]
                    + sorted_rows[:, D_MODEL + ROUTER_LANES:])

    def chunk(c, carry):
        g = ((c >= ends[0]).astype(jnp.int32) + (c >= ends[1]).astype(jnp.int32)
             + (c >= ends[2]).astype(jnp.int32))
        rows = pl.ds(pl.multiple_of(c * MOE_CHUNK, MOE_CHUNK), MOE_CHUNK)
        hsc = hs_ref[rows, :]
        wh, wl = _split_bf16(cws_ref[rows, :])
        cexp = _dot(wh, ex_ref[...]) + _dot(wl, ex_ref[...])
        a = _dot(hsc, wg_ref[g])
        u = _dot(hsc, wu_ref[g])
        hid = (a * jax.nn.sigmoid(a)) * u * cexp
        os_ref[rows, :] = _dot(hid.astype(BF16), wd_ref[g]).astype(BF16)
        return carry

    lax.fori_loop(0, ends[N_GROUPS - 1], chunk, 0)
    o_ref[...] = y + _dot(unsort, os_ref[...])


def _moe(y, p):
    n, d = y.shape
    tm = MOE_TILE
    assert n % tm == 0
    row_spec = pl.BlockSpec((tm, d), lambda i: (i, 0))
    return pl.pallas_call(
        _moe_kernel,
        grid=(n // tm,),
        in_specs=[row_spec, _const_spec((1, d)), _const_spec((d, ROUTER_LANES)),
                  _const_spec((d, ROUTER_LANES)), _const_spec((1, ROUTER_LANES)),
                  _const_spec((N_GROUPS, d, GROUP_HIDDEN)), _const_spec((N_GROUPS, d, GROUP_HIDDEN)),
                  _const_spec((N_GROUPS, GROUP_HIDDEN, d)),
                  _const_spec((ROUTER_LANES, GROUP_HIDDEN)),
                  _const_spec((tm, tm)), _const_spec((LANES, LANES))],
        out_specs=row_spec,
        out_shape=jax.ShapeDtypeStruct((n, d), F32),
        scratch_shapes=[pltpu.VMEM((MOE_SLOTS, d), BF16), pltpu.VMEM((MOE_SLOTS, ROUTER_LANES), F32),
                        pltpu.VMEM((MOE_SLOTS, d), BF16)],
        compiler_params=pltpu.CompilerParams(
            dimension_semantics=("arbitrary",), vmem_limit_bytes=VMEM_LIMIT),
        name="expert_mixer",
    )(y, p["g2"], p["wr_hi"], p["wr_lo"], p["br"], p["wg"], p["wu"], p["wd"], p["expand"],
      p["ltri"], p["eye"])


def _window_bias(n_q, q_tok, key_prev_ok, key_cur_ok):
    half = np.concatenate([key_prev_ok, key_cur_ok], axis=1)
    ok = np.concatenate([half, half], axis=1)
    return np.where(ok, 0.0, NEG_INF).astype(np.float32)


def _prompt_bias():
    t = np.arange(WINDOW)[:, None]
    s = np.arange(WINDOW)[None, :]
    prev_ok = s > t
    cur_ok = s <= t
    normal = _window_bias(WINDOW, t, prev_ok, cur_ok)
    first = _window_bias(WINDOW, t, np.zeros_like(prev_ok), cur_ok)
    return np.stack([normal, first])


def _sample_bias():
    out = []
    r = np.arange(8)[:, None]
    t = r // 2
    s = np.arange(WINDOW)[None, :]
    prev_ok = s > t
    for hf in range(2):
        c = s
        cur_ok = (c < 8) & (c % 2 == hf) & (c // 2 <= t)
        out.append(_window_bias(8, t, np.broadcast_to(prev_ok, (8, WINDOW)), cur_ok))
    return np.stack(out)


def _block_diag_mean(width):
    idx = np.arange(width) // HEAD_DIM
    return (idx[:, None] == idx[None, :]).astype(np.float32) / HEAD_DIM


def _expand_matrix():
    ex = np.zeros((ROUTER_LANES, GROUP_HIDDEN), np.float32)
    for g in range(N_GROUPS):
        for e in range(EXPERTS_PER_GROUP):
            ex[EXPERT_LANE0 + EXPERTS_PER_GROUP * g + e, e * D_EXPERT:(e + 1) * D_EXPERT] = 1.0
    return ex


def _prepare(norm1_g, w_in, q_norm_g, k_norm_g, attn_sinks, ln_v_g, ln_v_b, w_spatial, b_spatial,
             w_branch_a, w_branch_b, w_out, norm2_g, w_router_group, b_router_group,
             w_router_expert, b_router_expert, w_exp_gate, w_exp_up, w_exp_down):
    p = {}
    p["g1"] = norm1_g.reshape(1, D_MODEL)
    p["w_in"] = w_in.astype(BF16)
    p["gq"] = (jnp.tile(q_norm_g, N_HEADS) * (HEAD_DIM ** -0.5)).reshape(1, ATTN_WIDTH)
    p["gk"] = jnp.tile(k_norm_g, N_KV_HEADS).reshape(1, KV_WIDTH)
    p["bq"] = jnp.asarray(_block_diag_mean(ATTN_WIDTH), BF16)
    p["bk"] = jnp.asarray(_block_diag_mean(KV_WIDTH), BF16)
    p["sinks"] = attn_sinks.astype(F32)
    p["lng"] = ln_v_g.reshape(1, SGU_WIDTH)
    p["lnb"] = ln_v_b.reshape(1, SGU_WIDTH)
    tril = jnp.tril(jnp.ones((CHUNK, CHUNK), F32))
    wsp = w_spatial * tril[None]
    p["wsp"] = wsp.astype(BF16)
    p["bsp"] = jnp.repeat(b_spatial.T, SGU_GROUP_DIM, axis=1)
    w4 = wsp[:, :4, :4]
    p["coef_s"] = jnp.repeat(jnp.transpose(w4, (1, 2, 0)).reshape(16, SGU_GROUPS),
                             SGU_GROUP_DIM, axis=1)
    p["bias_sgu_s"] = jnp.repeat(b_spatial[:, :4].T, SGU_GROUP_DIM, axis=1)
    p["wa"] = w_branch_a.astype(BF16)
    p["wb"] = w_branch_b.astype(BF16)
    p["wout"] = w_out.astype(BF16)
    p["g2"] = norm2_g.reshape(1, D_MODEL)
    wr = jnp.concatenate(
        [w_router_group, w_router_expert,
         jnp.zeros((D_MODEL, ROUTER_LANES - N_GROUPS - N_EXPERTS), F32)], axis=1)
    p["wr_hi"], p["wr_lo"] = _split_bf16(wr)
    p["br"] = jnp.concatenate(
        [b_router_group, b_router_expert,
         jnp.zeros((ROUTER_LANES - N_GROUPS - N_EXPERTS,), F32)]).reshape(1, ROUTER_LANES)

    def pack_in(w):
        w = w.reshape(N_GROUPS, EXPERTS_PER_GROUP, D_MODEL, D_EXPERT)
        return jnp.transpose(w, (0, 2, 1, 3)).reshape(N_GROUPS, D_MODEL, GROUP_HIDDEN).astype(BF16)

    p["wg"] = pack_in(w_exp_gate)
    p["wu"] = pack_in(w_exp_up)
    p["wd"] = w_exp_down.reshape(N_GROUPS, GROUP_HIDDEN, D_MODEL).astype(BF16)
    p["expand"] = jnp.asarray(_expand_matrix(), BF16)
    p["ltri"] = jnp.asarray(np.tril(np.ones((MOE_TILE, MOE_TILE), np.float32), -1), BF16)
    p["eye"] = jnp.asarray(np.eye(LANES, dtype=np.float32), BF16)
    p["bias_prompt"] = jnp.asarray(_prompt_bias())
    p["bias_sample"] = jnp.asarray(_sample_bias())
    return p


def kernel(x_prompt, x_sample, cache_k_win, cache_v_win, norm1_g, w_in, q_norm_g, k_norm_g, attn_sinks, ln_v_g, ln_v_b, w_spatial, b_spatial, w_branch_a, w_branch_b, w_out, norm2_g, w_router_group, b_router_group, w_router_expert, b_router_expert, w_exp_gate, w_exp_up, w_exp_down):
    depth = norm1_g.shape[0]
    assert depth == 1
    batch, seq, d = x_prompt.shape
    dec_batch, dec_seq, _ = x_sample.shape
    assert dec_seq == 4 and d == D_MODEL
    p = _prepare(*(a[0] for a in (
        norm1_g, w_in, q_norm_g, k_norm_g, attn_sinks, ln_v_g, ln_v_b, w_spatial, b_spatial,
        w_branch_a, w_branch_b, w_out, norm2_g, w_router_group, b_router_group,
        w_router_expert, b_router_expert, w_exp_gate, w_exp_up, w_exp_down)))

    y1p, kwin, vwin = _prompt_mixer(x_prompt, p)
    yp = _moe(y1p.reshape(batch * seq, d), p).reshape(batch, seq, d)

    xs = jnp.transpose(x_sample, (1, 0, 2)).reshape(dec_seq * dec_batch, d)
    q, k, v, ob, vs, sga, sgb = _sample_in(xs, p)
    ck = cache_k_win[0].reshape(dec_batch, WINDOW, KV_WIDTH)
    cv = cache_v_win[0].reshape(dec_batch, WINDOW, KV_WIDTH)
    oa, nk, nv = _sample_attn(q, k, v, ck, cv, p)
    y1s = _sample_merge(xs, oa, ob, sga, sgb, p)
    ys = _moe(y1s, p)
    ys = jnp.transpose(ys.reshape(dec_seq, dec_batch, d), (1, 0, 2))
    vs_out = jnp.transpose(vs.reshape(dec_seq, dec_batch, SGU_GROUPS, SGU_GROUP_DIM), (1, 0, 2, 3))

    kv_shape = (WINDOW, N_KV_HEADS, HEAD_DIM)
    return (yp, ys,
            kwin.reshape(1, batch, *kv_shape), vwin.reshape(1, batch, *kv_shape),
            nk.reshape(1, dec_batch, *kv_shape), nv.reshape(1, dec_batch, *kv_shape),
            vs_out[None])
```

```python
import functools

import numpy as np
import jax
import jax.numpy as jnp
from jax import lax
from jax.experimental import pallas as pl
from jax.experimental.pallas import tpu as pltpu

F32 = jnp.float32
BF16 = jnp.bfloat16

D_MODEL = 1024
N_HEADS = 8
N_KV_HEADS = 2
HEAD_DIM = 64
WINDOW = 128
ATTN_WIDTH = N_HEADS * HEAD_DIM
KV_WIDTH = N_KV_HEADS * HEAD_DIM
SGU_GROUPS = 4
SGU_WIDTH = D_MODEL // 2
SGU_GROUP_DIM = SGU_WIDTH // SGU_GROUPS
CHUNK = 128
N_GROUPS = 4
EXPERTS_PER_GROUP = 4
N_EXPERTS = N_GROUPS * EXPERTS_PER_GROUP
D_EXPERT = 256
GROUP_HIDDEN = EXPERTS_PER_GROUP * D_EXPERT
D_IN = ATTN_WIDTH + 2 * KV_WIDTH + 2 * SGU_WIDTH + 2 * D_MODEL
EPS = 1e-6
NEG_INF = -1e30

C_Q = 0
C_K = C_Q + ATTN_WIDTH
C_V = C_K + KV_WIDTH
C_U = C_V + KV_WIDTH
C_VS = C_U + SGU_WIDTH
C_GA = C_VS + SGU_WIDTH
C_GB = C_GA + D_MODEL

LANES = 128
ROUTER_LANES = LANES
EXPERT_LANE0 = N_GROUPS
PROMPT_TILE = 512
MOE_TILE = 512
MOE_CHUNK = 128
MOE_MAX_CHUNKS = (MOE_TILE + N_GROUPS * (MOE_CHUNK - 1)) // MOE_CHUNK
MOE_SLOTS = MOE_MAX_CHUNKS * MOE_CHUNK
SAMPLE_BATCH_TILE = 8
VMEM_LIMIT = 56 * 1024 * 1024

_SQRT_2_OVER_PI = np.sqrt(2.0 / np.pi).astype(np.float32)


def _dot(a, b):
    return jnp.dot(a, b, preferred_element_type=F32)


def _dot_nt(a, b):
    return lax.dot_general(a, b, (((1,), (1,)), ((), ())), preferred_element_type=F32)


def _gelu(x):
    cdf = 0.5 * (1.0 + jnp.tanh(_SQRT_2_OVER_PI * (x + 0.044715 * (x * x * x))))
    return x * cdf


def _rms(x, g):
    return x * lax.rsqrt(jnp.mean(x * x, axis=-1, keepdims=True) + EPS) * g


def _head_rms(x, blockdiag, g):
    ms = _dot((x * x).astype(BF16), blockdiag)
    return x * lax.rsqrt(ms + EPS) * g


def _input_branches(x, g1, win_ref, gq, gk, bq, bk, lng, lnb):
    h = _rms(x, g1).astype(BF16)
    qn = _head_rms(_dot(h, win_ref[:, C_Q:C_K]), bq, gq)
    kn = _head_rms(_dot(h, win_ref[:, C_K:C_V]), bk, gk)
    v = _dot(h, win_ref[:, C_V:C_U])
    gu = _gelu(_dot(h, win_ref[:, C_U:C_VS]))
    gv = _gelu(_dot(h, win_ref[:, C_VS:C_GA]))
    mu = jnp.mean(gv, axis=-1, keepdims=True)
    xc = gv - mu
    vsn = xc * lax.rsqrt(jnp.mean(xc * xc, axis=-1, keepdims=True) + EPS) * lng + lnb
    return h, qn, kn, v, gu, vsn


def _head_variants(prev, prev_rot, cur, cur_rot):
    a = jnp.concatenate([prev, cur], axis=0)
    r = jnp.concatenate([prev_rot, cur_rot], axis=0)
    lo = lax.broadcasted_iota(jnp.int32, a.shape, 1) < HEAD_DIM
    zero = jnp.zeros_like(a)
    kv0 = jnp.concatenate([jnp.where(lo, a, zero), jnp.where(lo, zero, r)], axis=0)
    kv1 = jnp.concatenate([jnp.where(lo, r, zero), jnp.where(lo, zero, a)], axis=0)
    return kv0, kv1


def _lane_slabs(x):
    return [x[:, p * LANES:(p + 1) * LANES] for p in range(x.shape[1] // LANES)]


def _attend(q_slabs, kcat, vcat, bias, sinks_ref):
    n_keys = bias.shape[1] // 2
    outs = []
    for p in range(N_HEADS // 2):
        kv = p // (N_HEADS // 2 // N_KV_HEADS)
        s = _dot_nt(q_slabs[p].astype(BF16), kcat[kv]) + bias
        probs = []
        for par in range(2):
            sh = s[:, par * n_keys:(par + 1) * n_keys]
            sink = sinks_ref[2 * p + par]
            m = jnp.maximum(jnp.max(sh, axis=-1, keepdims=True), sink)
            e = jnp.exp(sh - m)
            den = jnp.sum(e, axis=-1, keepdims=True) + jnp.exp(sink - m)
            probs.append((e / den).astype(BF16))
        outs.append(_dot(jnp.concatenate(probs, axis=1), vcat[kv]))
    return outs


def _merge(x, h, oa, ob, win_ref, wa_ref, wb_ref, wout_ref):
    ya = _dot(oa.astype(BF16), wa_ref[...])
    yb = _dot(ob.astype(BF16), wb_ref[...])
    ga = _dot(h, win_ref[:, C_GA:C_GB])
    gb = _dot(h, win_ref[:, C_GB:D_IN])
    hm = jax.nn.sigmoid(ga) * ya + jax.nn.sigmoid(gb) * yb
    return x + _dot(hm.astype(BF16), wout_ref[...])


def _prompt_kernel(x_ref, g1_ref, win_ref, gq_ref, gk_ref, bq_ref, bk_ref, sinks_ref, lng_ref,
                   lnb_ref, wsp_ref, bsp_ref, wa_ref, wb_ref, wout_ref, bias_ref,
                   y_ref, kwin_ref, vwin_ref,
                   kprev, kprev_rot, vprev, vprev_rot):
    j = pl.program_id(1)

    @pl.when(j == 0)
    def _():
        for r in (kprev, kprev_rot, vprev, vprev_rot):
            r[...] = jnp.zeros_like(r)

    x = x_ref[0]
    h, qn, kn, v, gu, vsn = _input_branches(
        x, g1_ref[...], win_ref, gq_ref[...], gk_ref[...], bq_ref[...], bk_ref[...],
        lng_ref[...], lnb_ref[...])

    kb = kn.astype(BF16)
    kr = pltpu.roll(kn, HEAD_DIM, 1).astype(BF16)
    vb = v.astype(BF16)
    vr = pltpu.roll(v, HEAD_DIM, 1).astype(BF16)
    vsb = vsn.astype(BF16)

    n_blocks = x.shape[0] // WINDOW
    oa_blocks, ob_blocks = [], []
    pk, pkr, pv, pvr = kprev[...], kprev_rot[...], vprev[...], vprev_rot[...]
    for i in range(n_blocks):
        rows = slice(i * WINDOW, (i + 1) * WINDOW)
        ck, ckr, cv, cvr = kb[rows], kr[rows], vb[rows], vr[rows]
        kcat = _head_variants(pk, pkr, ck, ckr)
        vcat = _head_variants(pv, pvr, cv, cvr)
        if i == 0:
            bias = bias_ref[jnp.where(j == 0, 1, 0)]
        else:
            bias = bias_ref[0]
        oa_blocks.append(jnp.concatenate(
            _attend(_lane_slabs(qn[rows]), kcat, vcat, bias, sinks_ref), axis=1))
        mixed = jnp.concatenate(
            [_dot(wsp_ref[g], vsb[rows, g * SGU_GROUP_DIM:(g + 1) * SGU_GROUP_DIM])
             for g in range(SGU_GROUPS)], axis=1) + bsp_ref[...]
        ob_blocks.append(gu[rows] * mixed)
        pk, pkr, pv, pvr = ck, ckr, cv, cvr

    kprev[...] = pk
    kprev_rot[...] = pkr
    vprev[...] = pv
    vprev_rot[...] = pvr

    @pl.when(j == pl.num_programs(1) - 1)
    def _():
        kwin_ref[0] = kn[(n_blocks - 1) * WINDOW:]
        vwin_ref[0] = v[(n_blocks - 1) * WINDOW:]

    oa = jnp.concatenate(oa_blocks, axis=0)
    ob = jnp.concatenate(ob_blocks, axis=0)
    y_ref[0] = _merge(x, h, oa, ob, win_ref, wa_ref, wb_ref, wout_ref)


def _const_spec(shape, single_buffer=True):
    nd = len(shape)
    mode = pl.Buffered(1) if single_buffer else None
    return pl.BlockSpec(shape, lambda *_: (0,) * nd, pipeline_mode=mode)


def _smem_spec():
    return pl.BlockSpec(memory_space=pltpu.SMEM)


def _prompt_mixer(x, p):
    b, s, d = x.shape
    ts = PROMPT_TILE
    assert s % ts == 0 and ts % WINDOW == 0
    in_specs = [
        pl.BlockSpec((1, ts, d), lambda bi, j: (bi, j, 0)),
        _const_spec((1, d)), _const_spec((d, D_IN)),
        _const_spec((1, ATTN_WIDTH)), _const_spec((1, KV_WIDTH)),
        _const_spec((ATTN_WIDTH, ATTN_WIDTH)), _const_spec((KV_WIDTH, KV_WIDTH)),
        _smem_spec(),
        _const_spec((1, SGU_WIDTH)), _const_spec((1, SGU_WIDTH)),
        _const_spec((SGU_GROUPS, CHUNK, CHUNK)), _const_spec((CHUNK, SGU_WIDTH)),
        _const_spec((ATTN_WIDTH, d)), _const_spec((SGU_WIDTH, d)), _const_spec((d, d)),
        _const_spec((2, WINDOW, 4 * WINDOW)),
    ]
    out_specs = [
        pl.BlockSpec((1, ts, d), lambda bi, j: (bi, j, 0)),
        pl.BlockSpec((1, WINDOW, KV_WIDTH), lambda bi, j: (bi, 0, 0)),
        pl.BlockSpec((1, WINDOW, KV_WIDTH), lambda bi, j: (bi, 0, 0)),
    ]
    out_shape = [
        jax.ShapeDtypeStruct((b, s, d), F32),
        jax.ShapeDtypeStruct((b, WINDOW, KV_WIDTH), F32),
        jax.ShapeDtypeStruct((b, WINDOW, KV_WIDTH), F32),
    ]
    return pl.pallas_call(
        _prompt_kernel,
        grid=(b, s // ts),
        in_specs=in_specs, out_specs=out_specs, out_shape=out_shape,
        scratch_shapes=[pltpu.VMEM((WINDOW, KV_WIDTH), BF16)] * 4,
        compiler_params=pltpu.CompilerParams(
            dimension_semantics=("arbitrary", "arbitrary"), vmem_limit_bytes=VMEM_LIMIT),
        name="prompt_mixer",
    )(x, p["g1"], p["w_in"], p["gq"], p["gk"], p["bq"], p["bk"], p["sinks"], p["lng"], p["lnb"],
      p["wsp"], p["bsp"], p["wa"], p["wb"], p["wout"], p["bias_prompt"])


def _sample_in_kernel(x_ref, g1_ref, win_ref, gq_ref, gk_ref, bq_ref, bk_ref, lng_ref, lnb_ref,
                      coef_ref, sbias_ref,
                      q_ref, k_ref, v_ref, ob_ref, vs_ref, sga_ref, sgb_ref):
    x = x_ref[...]
    h, qn, kn, v, gu, vsn = _input_branches(
        x, g1_ref[...], win_ref, gq_ref[...], gk_ref[...], bq_ref[...], bk_ref[...],
        lng_ref[...], lnb_ref[...])
    for p, slab in enumerate(_lane_slabs(qn)):
        q_ref[p] = slab
    k_ref[...] = kn
    v_ref[...] = v
    vs_ref[...] = vsn
    sga_ref[...] = jax.nn.sigmoid(_dot(h, win_ref[:, C_GA:C_GB]))
    sgb_ref[...] = jax.nn.sigmoid(_dot(h, win_ref[:, C_GB:D_IN]))
    nb = x.shape[0] // 4
    for t in range(4):
        mixed = sbias_ref[t:t + 1, :]
        for jj in range(t + 1):
            mixed = mixed + coef_ref[4 * t + jj:4 * t + jj + 1, :] * vsn[jj * nb:(jj + 1) * nb]
        ob_ref[t * nb:(t + 1) * nb, :] = gu[t * nb:(t + 1) * nb] * mixed


def _sample_in(x, p):
    n, d = x.shape
    shapes = [(ATTN_WIDTH // LANES, n, LANES), (n, KV_WIDTH), (n, KV_WIDTH), (n, SGU_WIDTH),
              (n, SGU_WIDTH), (n, d), (n, d)]
    return pl.pallas_call(
        _sample_in_kernel,
        grid=(1,),
        in_specs=[_const_spec((n, d)), _const_spec((1, d)), _const_spec((d, D_IN)),
                  _const_spec((1, ATTN_WIDTH)), _const_spec((1, KV_WIDTH)),
                  _const_spec((ATTN_WIDTH, ATTN_WIDTH)), _const_spec((KV_WIDTH, KV_WIDTH)),
                  _const_spec((1, SGU_WIDTH)), _const_spec((1, SGU_WIDTH)),
                  _const_spec((16, SGU_WIDTH)), _const_spec((4, SGU_WIDTH))],
        out_specs=[_const_spec(sh, False) for sh in shapes],
        out_shape=[jax.ShapeDtypeStruct(sh, F32) for sh in shapes],
        compiler_params=pltpu.CompilerParams(
            dimension_semantics=("arbitrary",), vmem_limit_bytes=VMEM_LIMIT),
        name="sample_in",
    )(x, p["g1"], p["w_in"], p["gq"], p["gk"], p["bq"], p["bk"], p["lng"], p["lnb"],
      p["coef_s"], p["bias_sgu_s"])


def _sample_attn_kernel(q_ref, k_ref, v_ref, ck_ref, cv_ref, sinks_ref, bias_ref,
                        oa_ref, nk_ref, nv_ref):
    i = pl.program_id(0)
    n_slabs = q_ref.shape[0]
    half_rows = q_ref.shape[1] // 8
    pad = jnp.zeros((WINDOW - 8, KV_WIDTH), BF16)
    row_half = lax.broadcasted_iota(jnp.int32, (8, LANES), 0) % 2

    def body(bb, carry):
        b_lo = i * SAMPLE_BATCH_TILE + bb
        rows = pl.ds(b_lo, 8, stride=half_rows)
        q8 = [q_ref[p, rows, :] for p in range(n_slabs)]
        k8 = k_ref[rows, :]
        v8 = v_ref[rows, :]
        ck_new = jnp.concatenate([k8.astype(BF16), pad], axis=0)
        ckr_new = jnp.concatenate([pltpu.roll(k8, HEAD_DIM, 1).astype(BF16), pad], axis=0)
        cv_new = jnp.concatenate([v8.astype(BF16), pad], axis=0)
        cvr_new = jnp.concatenate([pltpu.roll(v8, HEAD_DIM, 1).astype(BF16), pad], axis=0)
        o8 = None
        for hf in range(2):
            kc = ck_ref[hf, bb]
            vc = cv_ref[hf, bb]
            kcat = _head_variants(kc.astype(BF16), pltpu.roll(kc, HEAD_DIM, 1).astype(BF16),
                                  ck_new, ckr_new)
            vcat = _head_variants(vc.astype(BF16), pltpu.roll(vc, HEAD_DIM, 1).astype(BF16),
                                  cv_new, cvr_new)
            o_hf = _attend(q8, kcat, vcat, bias_ref[hf], sinks_ref)
            o8 = o_hf if hf == 0 else [jnp.where(row_half == 0, a, b) for a, b in zip(o8, o_hf)]
            nk_ref[hf, bb] = pltpu.roll(kc, WINDOW - 4, 0)
            nv_ref[hf, bb] = pltpu.roll(vc, WINDOW - 4, 0)
            for t in range(4):
                nk_ref[hf, bb, WINDOW - 4 + t:WINDOW - 3 + t, :] = k8[2 * t + hf:2 * t + hf + 1, :]
                nv_ref[hf, bb, WINDOW - 4 + t:WINDOW - 3 + t, :] = v8[2 * t + hf:2 * t + hf + 1, :]
        for p in range(n_slabs):
            oa_ref[p, rows, :] = o8[p]
        return carry

    lax.fori_loop(0, SAMPLE_BATCH_TILE, body, 0, unroll=True)


def _sample_attn(q, k, v, ck, cv, p):
    n_slabs, n, _ = q.shape
    nb = n // 4
    half = nb // 2
    bt = SAMPLE_BATCH_TILE
    assert half % bt == 0
    ck4 = ck.reshape(2, half, WINDOW, KV_WIDTH)
    cv4 = cv.reshape(2, half, WINDOW, KV_WIDTH)
    cache_spec = pl.BlockSpec((2, bt, WINDOW, KV_WIDTH), lambda i: (0, i, 0, 0))
    return pl.pallas_call(
        _sample_attn_kernel,
        grid=(half // bt,),
        in_specs=[_const_spec((n_slabs, n, LANES)), _const_spec((n, KV_WIDTH)),
                  _const_spec((n, KV_WIDTH)), cache_spec, cache_spec, _smem_spec(),
                  _const_spec((2, 8, 4 * WINDOW))],
        out_specs=[_const_spec((n_slabs, n, LANES), False), cache_spec, cache_spec],
        out_shape=[jax.ShapeDtypeStruct((n_slabs, n, LANES), F32),
                   jax.ShapeDtypeStruct(ck4.shape, F32), jax.ShapeDtypeStruct(cv4.shape, F32)],
        compiler_params=pltpu.CompilerParams(
            dimension_semantics=("arbitrary",), vmem_limit_bytes=VMEM_LIMIT),
        name="sample_attn",
    )(q, k, v, ck4, cv4, p["sinks"], p["bias_sample"])


def _sample_merge_kernel(x_ref, oa_ref, ob_ref, sga_ref, sgb_ref, wa_ref, wb_ref, wout_ref, y_ref):
    oa = jnp.concatenate([oa_ref[p] for p in range(oa_ref.shape[0])], axis=1)
    ya = _dot(oa.astype(BF16), wa_ref[...])
    yb = _dot(ob_ref[...].astype(BF16), wb_ref[...])
    hm = sga_ref[...] * ya + sgb_ref[...] * yb
    y_ref[...] = x_ref[...] + _dot(hm.astype(BF16), wout_ref[...])


def _sample_merge(x, oa, ob, sga, sgb, p):
    n, d = x.shape
    return pl.pallas_call(
        _sample_merge_kernel,
        grid=(1,),
        in_specs=[_const_spec((n, d)), _const_spec(oa.shape), _const_spec((n, SGU_WIDTH)),
                  _const_spec((n, d)), _const_spec((n, d)),
                  _const_spec((ATTN_WIDTH, d)), _const_spec((SGU_WIDTH, d)), _const_spec((d, d))],
        out_specs=_const_spec((n, d), False),
        out_shape=jax.ShapeDtypeStruct((n, d), F32),
        compiler_params=pltpu.CompilerParams(
            dimension_semantics=("arbitrary",), vmem_limit_bytes=VMEM_LIMIT),
        name="sample_merge",
    )(x, oa, ob, sga, sgb, p["wa"], p["wb"], p["wout"])


def _route(logits):
    lane = lax.broadcasted_iota(jnp.int32, logits.shape, 1).astype(F32)
    far = float(ROUTER_LANES)
    glm = jnp.where(lane < N_GROUPS, logits, NEG_INF)
    gmax = jnp.max(glm, axis=-1, keepdims=True)
    gidx = jnp.min(jnp.where(glm == gmax, lane, far), axis=-1, keepdims=True)
    gw = 1.0 / jnp.sum(jnp.exp(glm - gmax), axis=-1, keepdims=True)
    first = EXPERT_LANE0 + EXPERTS_PER_GROUP * gidx
    sel = (lane >= first) & (lane < first + EXPERTS_PER_GROUP)
    el = jnp.where(sel, logits, NEG_INF)
    t1 = jnp.max(el, axis=-1, keepdims=True)
    i1 = jnp.min(jnp.where(el == t1, lane, far), axis=-1, keepdims=True)
    el2 = jnp.where(lane == i1, NEG_INF, el)
    t2 = jnp.max(el2, axis=-1, keepdims=True)
    i2 = jnp.min(jnp.where(el2 == t2, lane, far), axis=-1, keepdims=True)
    e2 = jnp.exp(t2 - t1)
    den = 1.0 + e2
    w1 = (1.0 / den) * gw
    w2 = (e2 / den) * gw
    return gidx, jnp.where(lane == i1, w1, 0.0) + jnp.where(lane == i2, w2, 0.0)


def _split_bf16(x):
    hi = x.astype(BF16)
    lo = (x - hi.astype(F32)).astype(BF16)
    return hi, lo


def _moe_kernel(y_ref, g2_ref, wrh_ref, wrl_ref, br_ref, wg_ref, wu_ref, wd_ref, ex_ref,
                ltri_ref, eye_ref, o_ref, hs_ref, cws_ref, os_ref):
    tm = y_ref.shape[0]

    @pl.when(pl.program_id(0) == 0)
    def _():
        os_ref[...] = jnp.zeros_like(os_ref)

    y = y_ref[...]
    hf = _rms(y, g2_ref[...])
    hb, hl = _split_bf16(hf)
    logits = _dot(hb, wrh_ref[...]) + (_dot(hb, wrl_ref[...]) + _dot(hl, wrh_ref[...])) + br_ref[...]
    gidx, cw = _route(logits)

    lane = lax.broadcasted_iota(jnp.int32, (tm, ROUTER_LANES), 1).astype(F32)
    onehot = jnp.where(lane == gidx, 1.0, 0.0)
    before = _dot(ltri_ref[...], onehot.astype(BF16))
    count = jnp.sum(onehot, axis=0, keepdims=True)
    n_chunks = jnp.floor((count + (MOE_CHUNK - 1)) * (1.0 / MOE_CHUNK))
    lane1 = lane[0:1]
    ends = []
    start_vec = jnp.zeros_like(n_chunks)
    end = jnp.zeros((1, 1), F32)
    for g in range(N_GROUPS):
        start_vec = start_vec + jnp.where(lane1 == g, end * MOE_CHUNK, 0.0)
        end = end + jnp.sum(jnp.where(lane1 == g, n_chunks, 0.0), axis=-1, keepdims=True)
        ends.append(end[0, 0].astype(jnp.int32))
    pos = jnp.sum(onehot * (before + start_vec), axis=-1, keepdims=True)

    slot_l = lax.broadcasted_iota(jnp.int32, (tm, MOE_SLOTS), 1).astype(F32)
    unsort = jnp.where(slot_l == pos, 1.0, 0.0).astype(BF16)
    pos_hi = jnp.floor(pos * (1.0 / MOE_CHUNK))
    pos_lo = pos - pos_hi * MOE_CHUNK
    pos_cols = jnp.where(lane == 0.0, pos_hi, jnp.where(lane == 1.0, pos_lo, 0.0)).astype(BF16)
    pos_t = _dot_nt(eye_ref[...], pos_cols)
    pos_row = pos_t[0:1] * MOE_CHUNK + pos_t[1:2]
    slot_s = lax.broadcasted_iota(jnp.int32, (MOE_SLOTS, tm), 0).astype(F32)
    sort = jnp.where(slot_s == pos_row, 1.0, 0.0).astype(BF16)

    ch, cl = _split_bf16(cw)
    sorted_rows = _dot(sort, jnp.concatenate([hb, ch, cl], axis=1))
    hs_ref[...] = sorted_rows[:, :D_MODEL].astype(BF16)
    cws_ref[...] = (sorted_rows[:, D_MODEL:D_MODEL + ROUTER_LANES]
                    + sorted_rows[:, D_MODEL + ROUTER_LANES:])

    def chunk(c, carry):
        g = ((c >= ends[0]).astype(jnp.int32) + (c >= ends[1]).astype(jnp.int32)
             + (c >= ends[2]).astype(jnp.int32))
        rows = pl.ds(pl.multiple_of(c * MOE_CHUNK, MOE_CHUNK), MOE_CHUNK)
        hsc = hs_ref[rows, :]
        wh, wl = _split_bf16(cws_ref[rows, :])
        cexp = _dot(wh, ex_ref[...]) + _dot(wl, ex_ref[...])
        experts = [EXPERTS_PER_GROUP * g + e for e in range(EXPERTS_PER_GROUP)]
        a = jnp.concatenate([_dot(hsc, wg_ref[e]) for e in experts], axis=1)
        u = jnp.concatenate([_dot(hsc, wu_ref[e]) for e in experts], axis=1)
        hid = (a * jax.nn.sigmoid(a)) * u * cexp
        os_ref[rows, :] = _dot(hid.astype(BF16), wd_ref[g]).astype(BF16)
        return carry

    lax.fori_loop(0, ends[N_GROUPS - 1], chunk, 0)
    o_ref[...] = y + _dot(unsort, os_ref[...])


def _moe(y, p):
    n, d = y.shape
    tm = MOE_TILE
    assert n % tm == 0
    row_spec = pl.BlockSpec((tm, d), lambda i: (i, 0))
    return pl.pallas_call(
        _moe_kernel,
        grid=(n // tm,),
        in_specs=[row_spec, _const_spec((1, d)), _const_spec((d, ROUTER_LANES)),
                  _const_spec((d, ROUTER_LANES)), _const_spec((1, ROUTER_LANES)),
                  _const_spec((N_EXPERTS, d, D_EXPERT)), _const_spec((N_EXPERTS, d, D_EXPERT)),
                  _const_spec((N_GROUPS, GROUP_HIDDEN, d)),
                  _const_spec((ROUTER_LANES, GROUP_HIDDEN)),
                  _const_spec((tm, tm)), _const_spec((LANES, LANES))],
        out_specs=row_spec,
        out_shape=jax.ShapeDtypeStruct((n, d), F32),
        scratch_shapes=[pltpu.VMEM((MOE_SLOTS, d), BF16), pltpu.VMEM((MOE_SLOTS, ROUTER_LANES), F32),
                        pltpu.VMEM((MOE_SLOTS, d), BF16)],
        compiler_params=pltpu.CompilerParams(
            dimension_semantics=("arbitrary",), vmem_limit_bytes=VMEM_LIMIT),
        name="expert_mixer",
    )(y, p["g2"], p["wr_hi"], p["wr_lo"], p["br"], p["wg"], p["wu"], p["wd"], p["expand"],
      p["ltri"], p["eye"])


def _window_bias(n_q, q_tok, key_prev_ok, key_cur_ok):
    half = np.concatenate([key_prev_ok, key_cur_ok], axis=1)
    ok = np.concatenate([half, half], axis=1)
    return np.where(ok, 0.0, NEG_INF).astype(np.float32)


def _prompt_bias():
    t = np.arange(WINDOW)[:, None]
    s = np.arange(WINDOW)[None, :]
    prev_ok = s > t
    cur_ok = s <= t
    normal = _window_bias(WINDOW, t, prev_ok, cur_ok)
    first = _window_bias(WINDOW, t, np.zeros_like(prev_ok), cur_ok)
    return np.stack([normal, first])


def _sample_bias():
    out = []
    r = np.arange(8)[:, None]
    t = r // 2
    s = np.arange(WINDOW)[None, :]
    prev_ok = s > t
    for hf in range(2):
        c = s
        cur_ok = (c < 8) & (c % 2 == hf) & (c // 2 <= t)
        out.append(_window_bias(8, t, np.broadcast_to(prev_ok, (8, WINDOW)), cur_ok))
    return np.stack(out)


def _block_diag_mean(width):
    idx = np.arange(width) // HEAD_DIM
    return (idx[:, None] == idx[None, :]).astype(np.float32) / HEAD_DIM


def _expand_matrix():
    ex = np.zeros((ROUTER_LANES, GROUP_HIDDEN), np.float32)
    for g in range(N_GROUPS):
        for e in range(EXPERTS_PER_GROUP):
            ex[EXPERT_LANE0 + EXPERTS_PER_GROUP * g + e, e * D_EXPERT:(e + 1) * D_EXPERT] = 1.0
    return ex


def _prepare(norm1_g, w_in, q_norm_g, k_norm_g, attn_sinks, ln_v_g, ln_v_b, w_spatial, b_spatial,
             w_branch_a, w_branch_b, w_out, norm2_g, w_router_group, b_router_group,
             w_router_expert, b_router_expert, w_exp_gate, w_exp_up, w_exp_down):
    p = {}
    p["g1"] = norm1_g.reshape(1, D_MODEL)
    p["w_in"] = w_in.astype(BF16)
    p["gq"] = (jnp.tile(q_norm_g, N_HEADS) * (HEAD_DIM ** -0.5)).reshape(1, ATTN_WIDTH)
    p["gk"] = jnp.tile(k_norm_g, N_KV_HEADS).reshape(1, KV_WIDTH)
    p["bq"] = jnp.asarray(_block_diag_mean(ATTN_WIDTH), BF16)
    p["bk"] = jnp.asarray(_block_diag_mean(KV_WIDTH), BF16)
    p["sinks"] = attn_sinks.astype(F32)
    p["lng"] = ln_v_g.reshape(1, SGU_WIDTH)
    p["lnb"] = ln_v_b.reshape(1, SGU_WIDTH)
    tril = jnp.tril(jnp.ones((CHUNK, CHUNK), F32))
    wsp = w_spatial * tril[None]
    p["wsp"] = wsp.astype(BF16)
    p["bsp"] = jnp.repeat(b_spatial.T, SGU_GROUP_DIM, axis=1)
    w4 = wsp[:, :4, :4]
    p["coef_s"] = jnp.repeat(jnp.transpose(w4, (1, 2, 0)).reshape(16, SGU_GROUPS),
                             SGU_GROUP_DIM, axis=1)
    p["bias_sgu_s"] = jnp.repeat(b_spatial[:, :4].T, SGU_GROUP_DIM, axis=1)
    p["wa"] = w_branch_a.astype(BF16)
    p["wb"] = w_branch_b.astype(BF16)
    p["wout"] = w_out.astype(BF16)
    p["g2"] = norm2_g.reshape(1, D_MODEL)
    wr = jnp.concatenate(
        [w_router_group, w_router_expert,
         jnp.zeros((D_MODEL, ROUTER_LANES - N_GROUPS - N_EXPERTS), F32)], axis=1)
    p["wr_hi"], p["wr_lo"] = _split_bf16(wr)
    p["br"] = jnp.concatenate(
        [b_router_group, b_router_expert,
         jnp.zeros((ROUTER_LANES - N_GROUPS - N_EXPERTS,), F32)]).reshape(1, ROUTER_LANES)
    p["wg"] = w_exp_gate.astype(BF16)
    p["wu"] = w_exp_up.astype(BF16)
    p["wd"] = w_exp_down.reshape(N_GROUPS, GROUP_HIDDEN, D_MODEL).astype(BF16)
    p["expand"] = jnp.asarray(_expand_matrix(), BF16)
    p["ltri"] = jnp.asarray(np.tril(np.ones((MOE_TILE, MOE_TILE), np.float32), -1), BF16)
    p["eye"] = jnp.asarray(np.eye(LANES, dtype=np.float32), BF16)
    p["bias_prompt"] = jnp.asarray(_prompt_bias())
    p["bias_sample"] = jnp.asarray(_sample_bias())
    return p


def kernel(x_prompt, x_sample, cache_k_win, cache_v_win, norm1_g, w_in, q_norm_g, k_norm_g, attn_sinks, ln_v_g, ln_v_b, w_spatial, b_spatial, w_branch_a, w_branch_b, w_out, norm2_g, w_router_group, b_router_group, w_router_expert, b_router_expert, w_exp_gate, w_exp_up, w_exp_down):
    depth = norm1_g.shape[0]
    assert depth == 1
    batch, seq, d = x_prompt.shape
    dec_batch, dec_seq, _ = x_sample.shape
    assert dec_seq == 4 and d == D_MODEL
    p = _prepare(*(a[0] for a in (
        norm1_g, w_in, q_norm_g, k_norm_g, attn_sinks, ln_v_g, ln_v_b, w_spatial, b_spatial,
        w_branch_a, w_branch_b, w_out, norm2_g, w_router_group, b_router_group,
        w_router_expert, b_router_expert, w_exp_gate, w_exp_up, w_exp_down)))

    y1p, kwin, vwin = _prompt_mixer(x_prompt, p)
    yp = _moe(y1p.reshape(batch * seq, d), p).reshape(batch, seq, d)

    xs = jnp.transpose(x_sample, (1, 0, 2)).reshape(dec_seq * dec_batch, d)
    q, k, v, ob, vs, sga, sgb = _sample_in(xs, p)
    ck = cache_k_win[0].reshape(dec_batch, WINDOW, KV_WIDTH)
    cv = cache_v_win[0].reshape(dec_batch, WINDOW, KV_WIDTH)
    oa, nk, nv = _sample_attn(q, k, v, ck, cv, p)
    y1s = _sample_merge(xs, oa, ob, sga, sgb, p)
    ys = _moe(y1s, p)
    ys = jnp.transpose(ys.reshape(dec_seq, dec_batch, d), (1, 0, 2))
    vs_out = jnp.transpose(vs.reshape(dec_seq, dec_batch, SGU_GROUPS, SGU_GROUP_DIM), (1, 0, 2, 3))

    kv_shape = (WINDOW, N_KV_HEADS, HEAD_DIM)
    return (yp, ys,
            kwin.reshape(1, batch, *kv_shape), vwin.reshape(1, batch, *kv_shape),
            nk.reshape(1, dec_batch, *kv_shape), nv.reshape(1, dec_batch, *kv_shape),
            vs_out[None])
```

```python
import functools

import numpy as np
import jax
import jax.numpy as jnp
from jax import lax
from jax.experimental import pallas as pl
from jax.experimental.pallas import tpu as pltpu

F32 = jnp.float32
BF16 = jnp.bfloat16

D_MODEL = 1024
N_HEADS = 8
N_KV_HEADS = 2
HEAD_DIM = 64
WINDOW = 128
ATTN_WIDTH = N_HEADS * HEAD_DIM
KV_WIDTH = N_KV_HEADS * HEAD_DIM
SGU_GROUPS = 4
SGU_WIDTH = D_MODEL // 2
SGU_GROUP_DIM = SGU_WIDTH // SGU_GROUPS
CHUNK = 128
N_GROUPS = 4
EXPERTS_PER_GROUP = 4
N_EXPERTS = N_GROUPS * EXPERTS_PER_GROUP
D_EXPERT = 256
GROUP_HIDDEN = EXPERTS_PER_GROUP * D_EXPERT
D_IN = ATTN_WIDTH + 2 * KV_WIDTH + 2 * SGU_WIDTH + 2 * D_MODEL
EPS = 1e-6
NEG_INF = -1e30

C_Q = 0
C_K = C_Q + ATTN_WIDTH
C_V = C_K + KV_WIDTH
C_U = C_V + KV_WIDTH
C_VS = C_U + SGU_WIDTH
C_GA = C_VS + SGU_WIDTH
C_GB = C_GA + D_MODEL

LANES = 128
ROUTER_LANES = LANES
EXPERT_LANE0 = N_GROUPS
PROMPT_TILE = 512
MOE_TILE = 512
MOE_CHUNK = 128
MOE_MAX_CHUNKS = (MOE_TILE + N_GROUPS * (MOE_CHUNK - 1)) // MOE_CHUNK
MOE_SLOTS = MOE_MAX_CHUNKS * MOE_CHUNK
SAMPLE_BATCH_TILE = 8
VMEM_LIMIT = 56 * 1024 * 1024

_SQRT_2_OVER_PI = np.sqrt(2.0 / np.pi).astype(np.float32)


def _dot(a, b):
    return jnp.dot(a, b, preferred_element_type=F32)


def _dot_nt(a, b):
    return lax.dot_general(a, b, (((1,), (1,)), ((), ())), preferred_element_type=F32)


def _gelu(x):
    cdf = 0.5 * (1.0 + jnp.tanh(_SQRT_2_OVER_PI * (x + 0.044715 * (x * x * x))))
    return x * cdf


def _rms(x, g):
    return x * lax.rsqrt(jnp.mean(x * x, axis=-1, keepdims=True) + EPS) * g


def _head_rms(x, blockdiag, g):
    ms = _dot((x * x).astype(BF16), blockdiag)
    return x * lax.rsqrt(ms + EPS) * g


def _qkv(x, g1, win_ref, gq, gk, bq, bk):
    h = _rms(x, g1).astype(BF16)
    qn = _head_rms(_dot(h, win_ref[:, C_Q:C_K]), bq, gq)
    kn = _head_rms(_dot(h, win_ref[:, C_K:C_V]), bk, gk)
    v = _dot(h, win_ref[:, C_V:C_U])
    return h, qn, kn, v


def _sgu_inputs(h, win_ref, lng, lnb):
    gu = _gelu(_dot(h, win_ref[:, C_U:C_VS]))
    gv = _gelu(_dot(h, win_ref[:, C_VS:C_GA]))
    mu = jnp.mean(gv, axis=-1, keepdims=True)
    xc = gv - mu
    vsn = xc * lax.rsqrt(jnp.mean(xc * xc, axis=-1, keepdims=True) + EPS) * lng + lnb
    return gu, vsn


def _head_variants(prev, prev_rot, cur, cur_rot):
    a = jnp.concatenate([prev, cur], axis=0)
    r = jnp.concatenate([prev_rot, cur_rot], axis=0)
    lo = lax.broadcasted_iota(jnp.int32, a.shape, 1) < HEAD_DIM
    zero = jnp.zeros_like(a)
    kv0 = jnp.concatenate([jnp.where(lo, a, zero), jnp.where(lo, zero, r)], axis=0)
    kv1 = jnp.concatenate([jnp.where(lo, r, zero), jnp.where(lo, zero, a)], axis=0)
    return kv0, kv1


def _lane_slabs(x):
    return [x[:, p * LANES:(p + 1) * LANES] for p in range(x.shape[1] // LANES)]


def _scores(q_slabs, kcat, bias):
    slabs_per_kv = len(q_slabs) // N_KV_HEADS
    return [_dot_nt(q.astype(BF16), kcat[p // slabs_per_kv]) + bias for p, q in enumerate(q_slabs)]


def _sink_softmax(scores, sinks_ref):
    out = []
    for p, s in enumerate(scores):
        n_keys = s.shape[1] // 2
        probs = []
        for par in range(2):
            sh = s[:, par * n_keys:(par + 1) * n_keys]
            sink = sinks_ref[2 * p + par]
            m = jnp.maximum(jnp.max(sh, axis=-1, keepdims=True), sink)
            e = jnp.exp(sh - m)
            den = jnp.sum(e, axis=-1, keepdims=True) + jnp.exp(sink - m)
            probs.append((e / den).astype(BF16))
        out.append(jnp.concatenate(probs, axis=1))
    return out


def _weighted_values(probs, vcat):
    slabs_per_kv = len(probs) // N_KV_HEADS
    return [_dot(pr, vcat[p // slabs_per_kv]) for p, pr in enumerate(probs)]


def _merge(x, h, oa, ob, win_ref, wa_ref, wb_ref, wout_ref):
    ya = _dot(oa.astype(BF16), wa_ref[...])
    yb = _dot(ob.astype(BF16), wb_ref[...])
    ga = _dot(h, win_ref[:, C_GA:C_GB])
    gb = _dot(h, win_ref[:, C_GB:D_IN])
    hm = jax.nn.sigmoid(ga) * ya + jax.nn.sigmoid(gb) * yb
    return x + _dot(hm.astype(BF16), wout_ref[...])


def _prompt_kernel(x_ref, g1_ref, win_ref, gq_ref, gk_ref, bq_ref, bk_ref, sinks_ref, lng_ref,
                   lnb_ref, wsp_ref, bsp_ref, wa_ref, wb_ref, wout_ref, bias_ref,
                   y_ref, kwin_ref, vwin_ref,
                   kprev, kprev_rot, vprev, vprev_rot):
    j = pl.program_id(1)

    @pl.when(j == 0)
    def _():
        for r in (kprev, kprev_rot, vprev, vprev_rot):
            r[...] = jnp.zeros_like(r)

    x = x_ref[0]
    h, qn, kn, v = _qkv(x, g1_ref[...], win_ref, gq_ref[...], gk_ref[...], bq_ref[...], bk_ref[...])

    kb = kn.astype(BF16)
    kr = pltpu.roll(kn, HEAD_DIM, 1).astype(BF16)
    vb = v.astype(BF16)
    vr = pltpu.roll(v, HEAD_DIM, 1).astype(BF16)

    n_blocks = x.shape[0] // WINDOW
    blocks = [slice(i * WINDOW, (i + 1) * WINDOW) for i in range(n_blocks)]
    scores, vcats = [], []
    pk, pkr, pv, pvr = kprev[...], kprev_rot[...], vprev[...], vprev_rot[...]
    for i, rows in enumerate(blocks):
        ck, ckr, cv, cvr = kb[rows], kr[rows], vb[rows], vr[rows]
        kcat = _head_variants(pk, pkr, ck, ckr)
        vcats.append(_head_variants(pv, pvr, cv, cvr))
        bias = bias_ref[jnp.where(j == 0, 1, 0)] if i == 0 else bias_ref[0]
        scores.append(_scores(_lane_slabs(qn[rows]), kcat, bias))
        pk, pkr, pv, pvr = ck, ckr, cv, cvr
    kprev[...] = pk
    kprev_rot[...] = pkr
    vprev[...] = pv
    vprev_rot[...] = pvr

    gu, vsn = _sgu_inputs(h, win_ref, lng_ref[...], lnb_ref[...])
    vsb = vsn.astype(BF16)
    ob = jnp.concatenate([
        gu[rows] * (jnp.concatenate(
            [_dot(wsp_ref[g], vsb[rows, g * SGU_GROUP_DIM:(g + 1) * SGU_GROUP_DIM])
             for g in range(SGU_GROUPS)], axis=1) + bsp_ref[...])
        for rows in blocks], axis=0)

    probs = [_sink_softmax(s, sinks_ref) for s in scores]
    oa = jnp.concatenate(
        [jnp.concatenate(_weighted_values(pr, vc), axis=1) for pr, vc in zip(probs, vcats)], axis=0)
    y_ref[0] = _merge(x, h, oa, ob, win_ref, wa_ref, wb_ref, wout_ref)

    @pl.when(j == pl.num_programs(1) - 1)
    def _():
        kwin_ref[0] = kn[(n_blocks - 1) * WINDOW:]
        vwin_ref[0] = v[(n_blocks - 1) * WINDOW:]


def _const_spec(shape, single_buffer=True):
    nd = len(shape)
    mode = pl.Buffered(1) if single_buffer else None
    return pl.BlockSpec(shape, lambda *_: (0,) * nd, pipeline_mode=mode)


def _smem_spec():
    return pl.BlockSpec(memory_space=pltpu.SMEM)


def _prompt_mixer(x, p):
    b, s, d = x.shape
    ts = PROMPT_TILE
    assert s % ts == 0 and ts % WINDOW == 0
    in_specs = [
        pl.BlockSpec((1, ts, d), lambda bi, j: (bi, j, 0)),
        _const_spec((1, d)), _const_spec((d, D_IN)),
        _const_spec((1, ATTN_WIDTH)), _const_spec((1, KV_WIDTH)),
        _const_spec((ATTN_WIDTH, ATTN_WIDTH)), _const_spec((KV_WIDTH, KV_WIDTH)),
        _smem_spec(),
        _const_spec((1, SGU_WIDTH)), _const_spec((1, SGU_WIDTH)),
        _const_spec((SGU_GROUPS, CHUNK, CHUNK)), _const_spec((CHUNK, SGU_WIDTH)),
        _const_spec((ATTN_WIDTH, d)), _const_spec((SGU_WIDTH, d)), _const_spec((d, d)),
        _const_spec((2, WINDOW, 4 * WINDOW)),
    ]
    out_specs = [
        pl.BlockSpec((1, ts, d), lambda bi, j: (bi, j, 0)),
        pl.BlockSpec((1, WINDOW, KV_WIDTH), lambda bi, j: (bi, 0, 0)),
        pl.BlockSpec((1, WINDOW, KV_WIDTH), lambda bi, j: (bi, 0, 0)),
    ]
    out_shape = [
        jax.ShapeDtypeStruct((b, s, d), F32),
        jax.ShapeDtypeStruct((b, WINDOW, KV_WIDTH), F32),
        jax.ShapeDtypeStruct((b, WINDOW, KV_WIDTH), F32),
    ]
    return pl.pallas_call(
        _prompt_kernel,
        grid=(b, s // ts),
        in_specs=in_specs, out_specs=out_specs, out_shape=out_shape,
        scratch_shapes=[pltpu.VMEM((WINDOW, KV_WIDTH), BF16)] * 4,
        compiler_params=pltpu.CompilerParams(
            dimension_semantics=("arbitrary", "arbitrary"), vmem_limit_bytes=VMEM_LIMIT),
        name="prompt_mixer",
    )(x, p["g1"], p["w_in"], p["gq"], p["gk"], p["bq"], p["bk"], p["sinks"], p["lng"], p["lnb"],
      p["wsp"], p["bsp"], p["wa"], p["wb"], p["wout"], p["bias_prompt"])


def _sample_in_kernel(x_ref, g1_ref, win_ref, gq_ref, gk_ref, bq_ref, bk_ref, lng_ref, lnb_ref,
                      coef_ref, sbias_ref,
                      q_ref, k_ref, v_ref, ob_ref, vs_ref, sga_ref, sgb_ref):
    x = x_ref[...]
    h, qn, kn, v = _qkv(x, g1_ref[...], win_ref, gq_ref[...], gk_ref[...], bq_ref[...], bk_ref[...])
    gu, vsn = _sgu_inputs(h, win_ref, lng_ref[...], lnb_ref[...])
    for p, slab in enumerate(_lane_slabs(qn)):
        q_ref[p] = slab
    k_ref[...] = kn
    v_ref[...] = v
    vs_ref[...] = vsn
    sga_ref[...] = jax.nn.sigmoid(_dot(h, win_ref[:, C_GA:C_GB]))
    sgb_ref[...] = jax.nn.sigmoid(_dot(h, win_ref[:, C_GB:D_IN]))
    nb = x.shape[0] // 4
    for t in range(4):
        mixed = sbias_ref[t:t + 1, :]
        for jj in range(t + 1):
            mixed = mixed + coef_ref[4 * t + jj:4 * t + jj + 1, :] * vsn[jj * nb:(jj + 1) * nb]
        ob_ref[t * nb:(t + 1) * nb, :] = gu[t * nb:(t + 1) * nb] * mixed


def _sample_in(x, p):
    n, d = x.shape
    shapes = [(ATTN_WIDTH // LANES, n, LANES), (n, KV_WIDTH), (n, KV_WIDTH), (n, SGU_WIDTH),
              (n, SGU_WIDTH), (n, d), (n, d)]
    return pl.pallas_call(
        _sample_in_kernel,
        grid=(1,),
        in_specs=[_const_spec((n, d)), _const_spec((1, d)), _const_spec((d, D_IN)),
                  _const_spec((1, ATTN_WIDTH)), _const_spec((1, KV_WIDTH)),
                  _const_spec((ATTN_WIDTH, ATTN_WIDTH)), _const_spec((KV_WIDTH, KV_WIDTH)),
                  _const_spec((1, SGU_WIDTH)), _const_spec((1, SGU_WIDTH)),
                  _const_spec((16, SGU_WIDTH)), _const_spec((4, SGU_WIDTH))],
        out_specs=[_const_spec(sh, False) for sh in shapes],
        out_shape=[jax.ShapeDtypeStruct(sh, F32) for sh in shapes],
        compiler_params=pltpu.CompilerParams(
            dimension_semantics=("arbitrary",), vmem_limit_bytes=VMEM_LIMIT),
        name="sample_in",
    )(x, p["g1"], p["w_in"], p["gq"], p["gk"], p["bq"], p["bk"], p["lng"], p["lnb"],
      p["coef_s"], p["bias_sgu_s"])


def _sample_attn_kernel(q_ref, k_ref, v_ref, ck_ref, cv_ref, sinks_ref, bias_ref,
                        oa_ref, nk_ref, nv_ref):
    i = pl.program_id(0)
    n_slabs = q_ref.shape[0]
    half_rows = q_ref.shape[1] // 8
    pad = jnp.zeros((WINDOW - 8, KV_WIDTH), BF16)
    row_half = lax.broadcasted_iota(jnp.int32, (8, LANES), 0) % 2

    work = []
    for bb in range(SAMPLE_BATCH_TILE):
        b_lo = i * SAMPLE_BATCH_TILE + bb
        rows = pl.ds(b_lo, 8, stride=half_rows)
        q8 = [q_ref[p, rows, :] for p in range(n_slabs)]
        k8 = k_ref[rows, :]
        v8 = v_ref[rows, :]
        ck_new = jnp.concatenate([k8.astype(BF16), pad], axis=0)
        ckr_new = jnp.concatenate([pltpu.roll(k8, HEAD_DIM, 1).astype(BF16), pad], axis=0)
        cv_new = jnp.concatenate([v8.astype(BF16), pad], axis=0)
        cvr_new = jnp.concatenate([pltpu.roll(v8, HEAD_DIM, 1).astype(BF16), pad], axis=0)
        for hf in range(2):
            kc = ck_ref[hf, bb]
            vc = cv_ref[hf, bb]
            kcat = _head_variants(kc.astype(BF16), pltpu.roll(kc, HEAD_DIM, 1).astype(BF16),
                                  ck_new, ckr_new)
            vcat = _head_variants(vc.astype(BF16), pltpu.roll(vc, HEAD_DIM, 1).astype(BF16),
                                  cv_new, cvr_new)
            work.append((rows, hf, _scores(q8, kcat, bias_ref[hf]), vcat))
            nk_ref[hf, bb] = pltpu.roll(kc, WINDOW - 4, 0)
            nv_ref[hf, bb] = pltpu.roll(vc, WINDOW - 4, 0)
            for t in range(4):
                nk_ref[hf, bb, WINDOW - 4 + t:WINDOW - 3 + t, :] = k8[2 * t + hf:2 * t + hf + 1, :]
                nv_ref[hf, bb, WINDOW - 4 + t:WINDOW - 3 + t, :] = v8[2 * t + hf:2 * t + hf + 1, :]
    probs = [_sink_softmax(s, sinks_ref) for _, _, s, _ in work]
    outs = [_weighted_values(pr, w[3]) for pr, w in zip(probs, work)]
    for (rows, _, _, _), o_even, o_odd in zip(work[0::2], outs[0::2], outs[1::2]):
        for p in range(n_slabs):
            oa_ref[p, rows, :] = jnp.where(row_half == 0, o_even[p], o_odd[p])


def _sample_attn(q, k, v, ck, cv, p):
    n_slabs, n, _ = q.shape
    nb = n // 4
    half = nb // 2
    bt = SAMPLE_BATCH_TILE
    assert half % bt == 0
    ck4 = ck.reshape(2, half, WINDOW, KV_WIDTH)
    cv4 = cv.reshape(2, half, WINDOW, KV_WIDTH)
    cache_spec = pl.BlockSpec((2, bt, WINDOW, KV_WIDTH), lambda i: (0, i, 0, 0))
    return pl.pallas_call(
        _sample_attn_kernel,
        grid=(half // bt,),
        in_specs=[_const_spec((n_slabs, n, LANES)), _const_spec((n, KV_WIDTH)),
                  _const_spec((n, KV_WIDTH)), cache_spec, cache_spec, _smem_spec(),
                  _const_spec((2, 8, 4 * WINDOW))],
        out_specs=[_const_spec((n_slabs, n, LANES), False), cache_spec, cache_spec],
        out_shape=[jax.ShapeDtypeStruct((n_slabs, n, LANES), F32),
                   jax.ShapeDtypeStruct(ck4.shape, F32), jax.ShapeDtypeStruct(cv4.shape, F32)],
        compiler_params=pltpu.CompilerParams(
            dimension_semantics=("arbitrary",), vmem_limit_bytes=VMEM_LIMIT),
        name="sample_attn",
    )(q, k, v, ck4, cv4, p["sinks"], p["bias_sample"])


def _sample_merge_kernel(x_ref, oa_ref, ob_ref, sga_ref, sgb_ref, wa_ref, wb_ref, wout_ref, y_ref):
    oa = jnp.concatenate([oa_ref[p] for p in range(oa_ref.shape[0])], axis=1)
    ya = _dot(oa.astype(BF16), wa_ref[...])
    yb = _dot(ob_ref[...].astype(BF16), wb_ref[...])
    hm = sga_ref[...] * ya + sgb_ref[...] * yb
    y_ref[...] = x_ref[...] + _dot(hm.astype(BF16), wout_ref[...])


def _sample_merge(x, oa, ob, sga, sgb, p):
    n, d = x.shape
    return pl.pallas_call(
        _sample_merge_kernel,
        grid=(1,),
        in_specs=[_const_spec((n, d)), _const_spec(oa.shape), _const_spec((n, SGU_WIDTH)),
                  _const_spec((n, d)), _const_spec((n, d)),
                  _const_spec((ATTN_WIDTH, d)), _const_spec((SGU_WIDTH, d)), _const_spec((d, d))],
        out_specs=_const_spec((n, d), False),
        out_shape=jax.ShapeDtypeStruct((n, d), F32),
        compiler_params=pltpu.CompilerParams(
            dimension_semantics=("arbitrary",), vmem_limit_bytes=VMEM_LIMIT),
        name="sample_merge",
    )(x, oa, ob, sga, sgb, p["wa"], p["wb"], p["wout"])


def _route(logits):
    lane = lax.broadcasted_iota(jnp.int32, logits.shape, 1).astype(F32)
    far = float(ROUTER_LANES)
    glm = jnp.where(lane < N_GROUPS, logits, NEG_INF)
    gmax = jnp.max(glm, axis=-1, keepdims=True)
    gidx = jnp.min(jnp.where(glm == gmax, lane, far), axis=-1, keepdims=True)
    gw = 1.0 / jnp.sum(jnp.exp(glm - gmax), axis=-1, keepdims=True)
    first = EXPERT_LANE0 + EXPERTS_PER_GROUP * gidx
    sel = (lane >= first) & (lane < first + EXPERTS_PER_GROUP)
    el = jnp.where(sel, logits, NEG_INF)
    t1 = jnp.max(el, axis=-1, keepdims=True)
    i1 = jnp.min(jnp.where(el == t1, lane, far), axis=-1, keepdims=True)
    el2 = jnp.where(lane == i1, NEG_INF, el)
    t2 = jnp.max(el2, axis=-1, keepdims=True)
    i2 = jnp.min(jnp.where(el2 == t2, lane, far), axis=-1, keepdims=True)
    e2 = jnp.exp(t2 - t1)
    den = 1.0 + e2
    w1 = (1.0 / den) * gw
    w2 = (e2 / den) * gw
    return gidx, jnp.where(lane == i1, w1, 0.0) + jnp.where(lane == i2, w2, 0.0)


def _split_bf16(x):
    hi = x.astype(BF16)
    lo = (x - hi.astype(F32)).astype(BF16)
    return hi, lo


def _moe_kernel(y_ref, g2_ref, wrh_ref, wrl_ref, br_ref, wg_ref, wu_ref, wd_ref, ex_ref,
                ltri_ref, eye_ref, o_ref, hs_ref, cws_ref, os_ref):
    tm = y_ref.shape[0]

    @pl.when(pl.program_id(0) == 0)
    def _():
        os_ref[...] = jnp.zeros_like(os_ref)

    y = y_ref[...]
    hf = _rms(y, g2_ref[...])
    hb, hl = _split_bf16(hf)
    logits = _dot(hb, wrh_ref[...]) + (_dot(hb, wrl_ref[...]) + _dot(hl, wrh_ref[...])) + br_ref[...]
    gidx, cw = _route(logits)

    lane = lax.broadcasted_iota(jnp.int32, (tm, ROUTER_LANES), 1).astype(F32)
    onehot = jnp.where(lane == gidx, 1.0, 0.0)
    before = _dot(ltri_ref[...], onehot.astype(BF16))
    count = jnp.sum(onehot, axis=0, keepdims=True)
    n_chunks = jnp.floor((count + (MOE_CHUNK - 1)) * (1.0 / MOE_CHUNK))
    lane1 = lane[0:1]
    ends = []
    start_vec = jnp.zeros_like(n_chunks)
    end = jnp.zeros((1, 1), F32)
    for g in range(N_GROUPS):
        start_vec = start_vec + jnp.where(lane1 == g, end * MOE_CHUNK, 0.0)
        end = end + jnp.sum(jnp.where(lane1 == g, n_chunks, 0.0), axis=-1, keepdims=True)
        ends.append(end[0, 0].astype(jnp.int32))
    pos = jnp.sum(onehot * (before + start_vec), axis=-1, keepdims=True)

    slot_l = lax.broadcasted_iota(jnp.int32, (tm, MOE_SLOTS), 1).astype(F32)
    unsort = jnp.where(slot_l == pos, 1.0, 0.0).astype(BF16)
    pos_hi = jnp.floor(pos * (1.0 / MOE_CHUNK))
    pos_lo = pos - pos_hi * MOE_CHUNK
    pos_cols = jnp.where(lane == 0.0, pos_hi, jnp.where(lane == 1.0, pos_lo, 0.0)).astype(BF16)
    pos_t = _dot_nt(eye_ref[...], pos_cols)
    pos_row = pos_t[0:1] * MOE_CHUNK + pos_t[1:2]
    slot_s = lax.broadcasted_iota(jnp.int32, (MOE_SLOTS, tm), 0).astype(F32)
    sort = jnp.where(slot_s == pos_row, 1.0, 0.0).astype(BF16)

    ch, cl = _split_bf16(cw)
    sorted_rows = _dot(sort, jnp.concatenate([hb, ch, cl], axis=1))
    hs_ref[...] = sorted_rows[:, :D_MODEL].astype(BF16)
    cws_ref[...] = (sorted_rows[:, D_MODEL:D_MODEL + ROUTER_LANES]
                    + sorted_rows[:, D_MODEL + ROUTER_LANES:])

    def chunk(c, carry):
        g = ((c >= ends[0]).astype(jnp.int32) + (c >= ends[1]).astype(jnp.int32)
             + (c >= ends[2]).astype(jnp.int32))
        rows = pl.ds(pl.multiple_of(c * MOE_CHUNK, MOE_CHUNK), MOE_CHUNK)
        hsc = hs_ref[rows, :]
        wh, wl = _split_bf16(cws_ref[rows, :])
        cexp = _dot(wh, ex_ref[...]) + _dot(wl, ex_ref[...])
        experts = [EXPERTS_PER_GROUP * g + e for e in range(EXPERTS_PER_GROUP)]
        a = jnp.concatenate([_dot(hsc, wg_ref[e]) for e in experts], axis=1)
        u = jnp.concatenate([_dot(hsc, wu_ref[e]) for e in experts], axis=1)
        hid = (a * jax.nn.sigmoid(a)) * u * cexp
        os_ref[rows, :] = _dot(hid.astype(BF16), wd_ref[g]).astype(BF16)
        return carry

    lax.fori_loop(0, ends[N_GROUPS - 1], chunk, 0)
    o_ref[...] = y + _dot(unsort, os_ref[...])


def _moe(y, p):
    n, d = y.shape
    tm = MOE_TILE
    assert n % tm == 0
    row_spec = pl.BlockSpec((tm, d), lambda i: (i, 0))
    return pl.pallas_call(
        _moe_kernel,
        grid=(n // tm,),
        in_specs=[row_spec, _const_spec((1, d)), _const_spec((d, ROUTER_LANES)),
                  _const_spec((d, ROUTER_LANES)), _const_spec((1, ROUTER_LANES)),
                  _const_spec((N_EXPERTS, d, D_EXPERT)), _const_spec((N_EXPERTS, d, D_EXPERT)),
                  _const_spec((N_GROUPS, GROUP_HIDDEN, d)),
                  _const_spec((ROUTER_LANES, GROUP_HIDDEN)),
                  _const_spec((tm, tm)), _const_spec((LANES, LANES))],
        out_specs=row_spec,
        out_shape=jax.ShapeDtypeStruct((n, d), F32),
        scratch_shapes=[pltpu.VMEM((MOE_SLOTS, d), BF16), pltpu.VMEM((MOE_SLOTS, ROUTER_LANES), F32),
                        pltpu.VMEM((MOE_SLOTS, d), BF16)],
        compiler_params=pltpu.CompilerParams(
            dimension_semantics=("arbitrary",), vmem_limit_bytes=VMEM_LIMIT),
        name="expert_mixer",
    )(y, p["g2"], p["wr_hi"], p["wr_lo"], p["br"], p["wg"], p["wu"], p["wd"], p["expand"],
      p["ltri"], p["eye"])


def _window_bias(n_q, q_tok, key_prev_ok, key_cur_ok):
    half = np.concatenate([key_prev_ok, key_cur_ok], axis=1)
    ok = np.concatenate([half, half], axis=1)
    return np.where(ok, 0.0, NEG_INF).astype(np.float32)


def _prompt_bias():
    t = np.arange(WINDOW)[:, None]
    s = np.arange(WINDOW)[None, :]
    prev_ok = s > t
    cur_ok = s <= t
    normal = _window_bias(WINDOW, t, prev_ok, cur_ok)
    first = _window_bias(WINDOW, t, np.zeros_like(prev_ok), cur_ok)
    return np.stack([normal, first])


def _sample_bias():
    out = []
    r = np.arange(8)[:, None]
    t = r // 2
    s = np.arange(WINDOW)[None, :]
    prev_ok = s > t
    for hf in range(2):
        c = s
        cur_ok = (c < 8) & (c % 2 == hf) & (c // 2 <= t)
        out.append(_window_bias(8, t, np.broadcast_to(prev_ok, (8, WINDOW)), cur_ok))
    return np.stack(out)


def _block_diag_mean(width):
    idx = np.arange(width) // HEAD_DIM
    return (idx[:, None] == idx[None, :]).astype(np.float32) / HEAD_DIM


def _expand_matrix():
    ex = np.zeros((ROUTER_LANES, GROUP_HIDDEN), np.float32)
    for g in range(N_GROUPS):
        for e in range(EXPERTS_PER_GROUP):
            ex[EXPERT_LANE0 + EXPERTS_PER_GROUP * g + e, e * D_EXPERT:(e + 1) * D_EXPERT] = 1.0
    return ex


def _prepare(norm1_g, w_in, q_norm_g, k_norm_g, attn_sinks, ln_v_g, ln_v_b, w_spatial, b_spatial,
             w_branch_a, w_branch_b, w_out, norm2_g, w_router_group, b_router_group,
             w_router_expert, b_router_expert, w_exp_gate, w_exp_up, w_exp_down):
    p = {}
    p["g1"] = norm1_g.reshape(1, D_MODEL)
    p["w_in"] = w_in.astype(BF16)
    p["gq"] = (jnp.tile(q_norm_g, N_HEADS) * (HEAD_DIM ** -0.5)).reshape(1, ATTN_WIDTH)
    p["gk"] = jnp.tile(k_norm_g, N_KV_HEADS).reshape(1, KV_WIDTH)
    p["bq"] = jnp.asarray(_block_diag_mean(ATTN_WIDTH), BF16)
    p["bk"] = jnp.asarray(_block_diag_mean(KV_WIDTH), BF16)
    p["sinks"] = attn_sinks.astype(F32)
    p["lng"] = ln_v_g.reshape(1, SGU_WIDTH)
    p["lnb"] = ln_v_b.reshape(1, SGU_WIDTH)
    tril = jnp.tril(jnp.ones((CHUNK, CHUNK), F32))
    wsp = w_spatial * tril[None]
    p["wsp"] = wsp.astype(BF16)
    p["bsp"] = jnp.repeat(b_spatial.T, SGU_GROUP_DIM, axis=1)
    w4 = wsp[:, :4, :4]
    p["coef_s"] = jnp.repeat(jnp.transpose(w4, (1, 2, 0)).reshape(16, SGU_GROUPS),
                             SGU_GROUP_DIM, axis=1)
    p["bias_sgu_s"] = jnp.repeat(b_spatial[:, :4].T, SGU_GROUP_DIM, axis=1)
    p["wa"] = w_branch_a.astype(BF16)
    p["wb"] = w_branch_b.astype(BF16)
    p["wout"] = w_out.astype(BF16)
    p["g2"] = norm2_g.reshape(1, D_MODEL)
    wr = jnp.concatenate(
        [w_router_group, w_router_expert,
         jnp.zeros((D_MODEL, ROUTER_LANES - N_GROUPS - N_EXPERTS), F32)], axis=1)
    p["wr_hi"], p["wr_lo"] = _split_bf16(wr)
    p["br"] = jnp.concatenate(
        [b_router_group, b_router_expert,
         jnp.zeros((ROUTER_LANES - N_GROUPS - N_EXPERTS,), F32)]).reshape(1, ROUTER_LANES)
    p["wg"] = w_exp_gate.astype(BF16)
    p["wu"] = w_exp_up.astype(BF16)
    p["wd"] = w_exp_down.reshape(N_GROUPS, GROUP_HIDDEN, D_MODEL).astype(BF16)
    p["expand"] = jnp.asarray(_expand_matrix(), BF16)
    p["ltri"] = jnp.asarray(np.tril(np.ones((MOE_TILE, MOE_TILE), np.float32), -1), BF16)
    p["eye"] = jnp.asarray(np.eye(LANES, dtype=np.float32), BF16)
    p["bias_prompt"] = jnp.asarray(_prompt_bias())
    p["bias_sample"] = jnp.asarray(_sample_bias())
    return p


def kernel(x_prompt, x_sample, cache_k_win, cache_v_win, norm1_g, w_in, q_norm_g, k_norm_g, attn_sinks, ln_v_g, ln_v_b, w_spatial, b_spatial, w_branch_a, w_branch_b, w_out, norm2_g, w_router_group, b_router_group, w_router_expert, b_router_expert, w_exp_gate, w_exp_up, w_exp_down):
    depth = norm1_g.shape[0]
    assert depth == 1
    batch, seq, d = x_prompt.shape
    dec_batch, dec_seq, _ = x_sample.shape
    assert dec_seq == 4 and d == D_MODEL
    p = _prepare(*(a[0] for a in (
        norm1_g, w_in, q_norm_g, k_norm_g, attn_sinks, ln_v_g, ln_v_b, w_spatial, b_spatial,
        w_branch_a, w_branch_b, w_out, norm2_g, w_router_group, b_router_group,
        w_router_expert, b_router_expert, w_exp_gate, w_exp_up, w_exp_down)))

    y1p, kwin, vwin = _prompt_mixer(x_prompt, p)
    yp = _moe(y1p.reshape(batch * seq, d), p).reshape(batch, seq, d)

    xs = jnp.transpose(x_sample, (1, 0, 2)).reshape(dec_seq * dec_batch, d)
    q, k, v, ob, vs, sga, sgb = _sample_in(xs, p)
    ck = cache_k_win[0].reshape(dec_batch, WINDOW, KV_WIDTH)
    cv = cache_v_win[0].reshape(dec_batch, WINDOW, KV_WIDTH)
    oa, nk, nv = _sample_attn(q, k, v, ck, cv, p)
    y1s = _sample_merge(xs, oa, ob, sga, sgb, p)
    ys = _moe(y1s, p)
    ys = jnp.transpose(ys.reshape(dec_seq, dec_batch, d), (1, 0, 2))
    vs_out = jnp.transpose(vs.reshape(dec_seq, dec_batch, SGU_GROUPS, SGU_GROUP_DIM), (1, 0, 2, 3))

    kv_shape = (WINDOW, N_KV_HEADS, HEAD_DIM)
    return (yp, ys,
            kwin.reshape(1, batch, *kv_shape), vwin.reshape(1, batch, *kv_shape),
            nk.reshape(1, dec_batch, *kv_shape), nv.reshape(1, dec_batch, *kv_shape),
            vs_out[None])
```

```python
import functools

import numpy as np
import jax
import jax.numpy as jnp
from jax import lax
from jax.experimental import pallas as pl
from jax.experimental.pallas import tpu as pltpu

F32 = jnp.float32
BF16 = jnp.bfloat16

D_MODEL = 1024
N_HEADS = 8
N_KV_HEADS = 2
HEAD_DIM = 64
WINDOW = 128
ATTN_WIDTH = N_HEADS * HEAD_DIM
KV_WIDTH = N_KV_HEADS * HEAD_DIM
SGU_GROUPS = 4
SGU_WIDTH = D_MODEL // 2
SGU_GROUP_DIM = SGU_WIDTH // SGU_GROUPS
CHUNK = 128
N_GROUPS = 4
EXPERTS_PER_GROUP = 4
N_EXPERTS = N_GROUPS * EXPERTS_PER_GROUP
D_EXPERT = 256
GROUP_HIDDEN = EXPERTS_PER_GROUP * D_EXPERT
D_IN = ATTN_WIDTH + 2 * KV_WIDTH + 2 * SGU_WIDTH + 2 * D_MODEL
EPS = 1e-6
NEG_INF = -1e30

C_Q = 0
C_K = C_Q + ATTN_WIDTH
C_V = C_K + KV_WIDTH
C_U = C_V + KV_WIDTH
C_VS = C_U + SGU_WIDTH
C_GA = C_VS + SGU_WIDTH
C_GB = C_GA + D_MODEL

LANES = 128
ROUTER_LANES = LANES
EXPERT_LANE0 = N_GROUPS
PROMPT_TILE = 512
MOE_TILE = 512
MOE_CHUNK = 144
BF16_ROWS = 16
MOE_MAX_CHUNKS = (MOE_TILE + N_GROUPS * (MOE_CHUNK - 1)) // MOE_CHUNK
MOE_SLOTS = -(-MOE_MAX_CHUNKS * MOE_CHUNK // LANES) * LANES
SLOT_SPLIT = 128
SAMPLE_BATCH_TILE = 8
VMEM_LIMIT = 56 * 1024 * 1024

_SQRT_2_OVER_PI = np.sqrt(2.0 / np.pi).astype(np.float32)


def _dot(a, b):
    return jnp.dot(a, b, preferred_element_type=F32)


def _dot_nt(a, b):
    return lax.dot_general(a, b, (((1,), (1,)), ((), ())), preferred_element_type=F32)


def _gelu(x):
    cdf = 0.5 * (1.0 + jnp.tanh(_SQRT_2_OVER_PI * (x + 0.044715 * (x * x * x))))
    return x * cdf


def _rms(x, g):
    return x * lax.rsqrt(jnp.mean(x * x, axis=-1, keepdims=True) + EPS) * g


def _head_rms(x, blockdiag, g):
    ms = _dot((x * x).astype(BF16), blockdiag)
    return x * lax.rsqrt(ms + EPS) * g


def _qkv(x, g1, win_ref, gq, gk, bq, bk):
    h = _rms(x, g1).astype(BF16)
    qn = _head_rms(_dot(h, win_ref[:, C_Q:C_K]), bq, gq)
    kv = _dot(h, win_ref[:, C_K:C_U])
    kn = _head_rms(kv[:, :KV_WIDTH], bk, gk)
    v = kv[:, KV_WIDTH:]
    return h, qn, kn, v


def _sgu_inputs(h, win_ref, lng, lnb):
    gu = _gelu(_dot(h, win_ref[:, C_U:C_VS]))
    gv = _gelu(_dot(h, win_ref[:, C_VS:C_GA]))
    mu = jnp.mean(gv, axis=-1, keepdims=True)
    xc = gv - mu
    vsn = xc * lax.rsqrt(jnp.mean(xc * xc, axis=-1, keepdims=True) + EPS) * lng + lnb
    return gu, vsn


def _head_variants(prev, prev_rot, cur, cur_rot):
    a = jnp.concatenate([prev, cur], axis=0)
    r = jnp.concatenate([prev_rot, cur_rot], axis=0)
    lo = lax.broadcasted_iota(jnp.int32, a.shape, 1) < HEAD_DIM
    zero = jnp.zeros_like(a)
    kv0 = jnp.concatenate([jnp.where(lo, a, zero), jnp.where(lo, zero, r)], axis=0)
    kv1 = jnp.concatenate([jnp.where(lo, r, zero), jnp.where(lo, zero, a)], axis=0)
    return kv0, kv1


def _lane_slabs(x):
    return [x[:, p * LANES:(p + 1) * LANES] for p in range(x.shape[1] // LANES)]


def _scores(q_slabs, kcat, bias):
    slabs_per_kv = len(q_slabs) // N_KV_HEADS
    return [_dot_nt(q.astype(BF16), kcat[p // slabs_per_kv]) + bias for p, q in enumerate(q_slabs)]


def _sink_softmax(scores, sinks_ref):
    out = []
    for p, s in enumerate(scores):
        n_keys = s.shape[1] // 2
        probs = []
        for par in range(2):
            sh = s[:, par * n_keys:(par + 1) * n_keys]
            sink = sinks_ref[2 * p + par]
            m = jnp.maximum(jnp.max(sh, axis=-1, keepdims=True), sink)
            e = jnp.exp(sh - m)
            den = jnp.sum(e, axis=-1, keepdims=True) + jnp.exp(sink - m)
            probs.append((e / den).astype(BF16))
        out.append(jnp.concatenate(probs, axis=1))
    return out


def _weighted_values(probs, vcat):
    slabs_per_kv = len(probs) // N_KV_HEADS
    return [_dot(pr, vcat[p // slabs_per_kv]) for p, pr in enumerate(probs)]


def _merge(x, h, oa, ob, win_ref, wa_ref, wb_ref, wout_ref):
    ya = _dot(oa.astype(BF16), wa_ref[...])
    yb = _dot(ob.astype(BF16), wb_ref[...])
    ga = _dot(h, win_ref[:, C_GA:C_GB])
    gb = _dot(h, win_ref[:, C_GB:D_IN])
    hm = jax.nn.sigmoid(ga) * ya + jax.nn.sigmoid(gb) * yb
    return x + _dot(hm.astype(BF16), wout_ref[...])


def _prompt_kernel(x_ref, g1_ref, win_ref, gq_ref, gk_ref, bq_ref, bk_ref, sinks_ref, lng_ref,
                   lnb_ref, wsp_ref, bsp_ref, wa_ref, wb_ref, wout_ref, bias_ref,
                   y_ref, kwin_ref, vwin_ref,
                   kprev, kprev_rot, vprev, vprev_rot):
    j = pl.program_id(1)

    @pl.when(j == 0)
    def _():
        for r in (kprev, kprev_rot, vprev, vprev_rot):
            r[...] = jnp.zeros_like(r)

    x = x_ref[0]
    h, qn, kn, v = _qkv(x, g1_ref[...], win_ref, gq_ref[...], gk_ref[...], bq_ref[...], bk_ref[...])

    kb = kn.astype(BF16)
    kr = pltpu.roll(kn, HEAD_DIM, 1).astype(BF16)
    vb = v.astype(BF16)
    vr = pltpu.roll(v, HEAD_DIM, 1).astype(BF16)

    n_blocks = x.shape[0] // WINDOW
    blocks = [slice(i * WINDOW, (i + 1) * WINDOW) for i in range(n_blocks)]
    scores, vcats = [], []
    pk, pkr, pv, pvr = kprev[...], kprev_rot[...], vprev[...], vprev_rot[...]
    for i, rows in enumerate(blocks):
        ck, ckr, cv, cvr = kb[rows], kr[rows], vb[rows], vr[rows]
        kcat = _head_variants(pk, pkr, ck, ckr)
        vcats.append(_head_variants(pv, pvr, cv, cvr))
        bias = bias_ref[jnp.where(j == 0, 1, 0)] if i == 0 else bias_ref[0]
        scores.append(_scores(_lane_slabs(qn[rows]), kcat, bias))
        pk, pkr, pv, pvr = ck, ckr, cv, cvr
    kprev[...] = pk
    kprev_rot[...] = pkr
    vprev[...] = pv
    vprev_rot[...] = pvr

    gu, vsn = _sgu_inputs(h, win_ref, lng_ref[...], lnb_ref[...])
    vsb = vsn.astype(BF16)
    ob = jnp.concatenate([
        gu[rows] * (jnp.concatenate(
            [_dot(wsp_ref[g], vsb[rows, g * SGU_GROUP_DIM:(g + 1) * SGU_GROUP_DIM])
             for g in range(SGU_GROUPS)], axis=1) + bsp_ref[...])
        for rows in blocks], axis=0)

    probs = [_sink_softmax(s, sinks_ref) for s in scores]
    oa = jnp.concatenate(
        [jnp.concatenate(_weighted_values(pr, vc), axis=1) for pr, vc in zip(probs, vcats)], axis=0)
    y_ref[0] = _merge(x, h, oa, ob, win_ref, wa_ref, wb_ref, wout_ref)

    @pl.when(j == pl.num_programs(1) - 1)
    def _():
        kwin_ref[0] = kn[(n_blocks - 1) * WINDOW:]
        vwin_ref[0] = v[(n_blocks - 1) * WINDOW:]


def _const_spec(shape, single_buffer=True):
    nd = len(shape)
    mode = pl.Buffered(1) if single_buffer else None
    return pl.BlockSpec(shape, lambda *_: (0,) * nd, pipeline_mode=mode)


def _smem_spec():
    return pl.BlockSpec(memory_space=pltpu.SMEM)


def _prompt_mixer(x, p):
    b, s, d = x.shape
    ts = PROMPT_TILE
    assert s % ts == 0 and ts % WINDOW == 0
    in_specs = [
        pl.BlockSpec((1, ts, d), lambda bi, j: (bi, j, 0)),
        _const_spec((1, d)), _const_spec((d, D_IN)),
        _const_spec((1, ATTN_WIDTH)), _const_spec((1, KV_WIDTH)),
        _const_spec((ATTN_WIDTH, ATTN_WIDTH)), _const_spec((KV_WIDTH, KV_WIDTH)),
        _smem_spec(),
        _const_spec((1, SGU_WIDTH)), _const_spec((1, SGU_WIDTH)),
        _const_spec((SGU_GROUPS, CHUNK, CHUNK)), _const_spec((CHUNK, SGU_WIDTH)),
        _const_spec((ATTN_WIDTH, d)), _const_spec((SGU_WIDTH, d)), _const_spec((d, d)),
        _const_spec((2, WINDOW, 4 * WINDOW)),
    ]
    out_specs = [
        pl.BlockSpec((1, ts, d), lambda bi, j: (bi, j, 0)),
        pl.BlockSpec((1, WINDOW, KV_WIDTH), lambda bi, j: (bi, 0, 0)),
        pl.BlockSpec((1, WINDOW, KV_WIDTH), lambda bi, j: (bi, 0, 0)),
    ]
    out_shape = [
        jax.ShapeDtypeStruct((b, s, d), F32),
        jax.ShapeDtypeStruct((b, WINDOW, KV_WIDTH), F32),
        jax.ShapeDtypeStruct((b, WINDOW, KV_WIDTH), F32),
    ]
    return pl.pallas_call(
        _prompt_kernel,
        grid=(b, s // ts),
        in_specs=in_specs, out_specs=out_specs, out_shape=out_shape,
        scratch_shapes=[pltpu.VMEM((WINDOW, KV_WIDTH), BF16)] * 4,
        compiler_params=pltpu.CompilerParams(
            dimension_semantics=("arbitrary", "arbitrary"), vmem_limit_bytes=VMEM_LIMIT),
        name="prompt_mixer",
    )(x, p["g1"], p["w_in"], p["gq"], p["gk"], p["bq"], p["bk"], p["sinks"], p["lng"], p["lnb"],
      p["wsp"], p["bsp"], p["wa"], p["wb"], p["wout"], p["bias_prompt"])


def _sample_in_kernel(x_ref, g1_ref, win_ref, gq_ref, gk_ref, bq_ref, bk_ref, lng_ref, lnb_ref,
                      coef_ref, sbias_ref,
                      q_ref, k_ref, v_ref, ob_ref, vs_ref, sga_ref, sgb_ref):
    x = x_ref[...]
    h, qn, kn, v = _qkv(x, g1_ref[...], win_ref, gq_ref[...], gk_ref[...], bq_ref[...], bk_ref[...])
    gu, vsn = _sgu_inputs(h, win_ref, lng_ref[...], lnb_ref[...])
    for p, slab in enumerate(_lane_slabs(qn)):
        q_ref[p] = slab
    k_ref[...] = kn
    v_ref[...] = v
    vs_ref[...] = vsn
    sga_ref[...] = jax.nn.sigmoid(_dot(h, win_ref[:, C_GA:C_GB]))
    sgb_ref[...] = jax.nn.sigmoid(_dot(h, win_ref[:, C_GB:D_IN]))
    nb = x.shape[0] // 4
    for t in range(4):
        mixed = sbias_ref[t:t + 1, :]
        for jj in range(t + 1):
            mixed = mixed + coef_ref[4 * t + jj:4 * t + jj + 1, :] * vsn[jj * nb:(jj + 1) * nb]
        ob_ref[t * nb:(t + 1) * nb, :] = gu[t * nb:(t + 1) * nb] * mixed


def _sample_in(x, p):
    n, d = x.shape
    shapes = [(ATTN_WIDTH // LANES, n, LANES), (n, KV_WIDTH), (n, KV_WIDTH), (n, SGU_WIDTH),
              (n, SGU_WIDTH), (n, d), (n, d)]
    return pl.pallas_call(
        _sample_in_kernel,
        grid=(1,),
        in_specs=[_const_spec((n, d)), _const_spec((1, d)), _const_spec((d, D_IN)),
                  _const_spec((1, ATTN_WIDTH)), _const_spec((1, KV_WIDTH)),
                  _const_spec((ATTN_WIDTH, ATTN_WIDTH)), _const_spec((KV_WIDTH, KV_WIDTH)),
                  _const_spec((1, SGU_WIDTH)), _const_spec((1, SGU_WIDTH)),
                  _const_spec((16, SGU_WIDTH)), _const_spec((4, SGU_WIDTH))],
        out_specs=[_const_spec(sh, False) for sh in shapes],
        out_shape=[jax.ShapeDtypeStruct(sh, F32) for sh in shapes],
        compiler_params=pltpu.CompilerParams(
            dimension_semantics=("arbitrary",), vmem_limit_bytes=VMEM_LIMIT),
        name="sample_in",
    )(x, p["g1"], p["w_in"], p["gq"], p["gk"], p["bq"], p["bk"], p["lng"], p["lnb"],
      p["coef_s"], p["bias_sgu_s"])


def _sample_attn_kernel(q_ref, k_ref, v_ref, ck_ref, cv_ref, sinks_ref, bias_ref,
                        oa_ref, nk_ref, nv_ref):
    i = pl.program_id(0)
    n_slabs = q_ref.shape[0]
    half_rows = q_ref.shape[1] // 8
    pad = jnp.zeros((WINDOW - 8, KV_WIDTH), BF16)
    row_half = lax.broadcasted_iota(jnp.int32, (8, LANES), 0) % 2

    work = []
    for bb in range(SAMPLE_BATCH_TILE):
        b_lo = i * SAMPLE_BATCH_TILE + bb
        rows = pl.ds(b_lo, 8, stride=half_rows)
        q8 = [q_ref[p, rows, :] for p in range(n_slabs)]
        k8 = k_ref[rows, :]
        v8 = v_ref[rows, :]
        ck_new = jnp.concatenate([k8.astype(BF16), pad], axis=0)
        ckr_new = jnp.concatenate([pltpu.roll(k8, HEAD_DIM, 1).astype(BF16), pad], axis=0)
        cv_new = jnp.concatenate([v8.astype(BF16), pad], axis=0)
        cvr_new = jnp.concatenate([pltpu.roll(v8, HEAD_DIM, 1).astype(BF16), pad], axis=0)
        for hf in range(2):
            kc = ck_ref[hf, bb]
            vc = cv_ref[hf, bb]
            kcat = _head_variants(kc.astype(BF16), pltpu.roll(kc, HEAD_DIM, 1).astype(BF16),
                                  ck_new, ckr_new)
            vcat = _head_variants(vc.astype(BF16), pltpu.roll(vc, HEAD_DIM, 1).astype(BF16),
                                  cv_new, cvr_new)
            work.append((rows, hf, _scores(q8, kcat, bias_ref[hf]), vcat))
            nk_ref[hf, bb] = pltpu.roll(kc, WINDOW - 4, 0)
            nv_ref[hf, bb] = pltpu.roll(vc, WINDOW - 4, 0)
            for t in range(4):
                nk_ref[hf, bb, WINDOW - 4 + t:WINDOW - 3 + t, :] = k8[2 * t + hf:2 * t + hf + 1, :]
                nv_ref[hf, bb, WINDOW - 4 + t:WINDOW - 3 + t, :] = v8[2 * t + hf:2 * t + hf + 1, :]
    probs = [_sink_softmax(s, sinks_ref) for _, _, s, _ in work]
    outs = [_weighted_values(pr, w[3]) for pr, w in zip(probs, work)]
    for (rows, _, _, _), o_even, o_odd in zip(work[0::2], outs[0::2], outs[1::2]):
        for p in range(n_slabs):
            oa_ref[p, rows, :] = jnp.where(row_half == 0, o_even[p], o_odd[p])


def _sample_attn(q, k, v, ck, cv, p):
    n_slabs, n, _ = q.shape
    nb = n // 4
    half = nb // 2
    bt = SAMPLE_BATCH_TILE
    assert half % bt == 0
    ck4 = ck.reshape(2, half, WINDOW, KV_WIDTH)
    cv4 = cv.reshape(2, half, WINDOW, KV_WIDTH)
    cache_spec = pl.BlockSpec((2, bt, WINDOW, KV_WIDTH), lambda i: (0, i, 0, 0))
    return pl.pallas_call(
        _sample_attn_kernel,
        grid=(half // bt,),
        in_specs=[_const_spec((n_slabs, n, LANES)), _const_spec((n, KV_WIDTH)),
                  _const_spec((n, KV_WIDTH)), cache_spec, cache_spec, _smem_spec(),
                  _const_spec((2, 8, 4 * WINDOW))],
        out_specs=[_const_spec((n_slabs, n, LANES), False), cache_spec, cache_spec],
        out_shape=[jax.ShapeDtypeStruct((n_slabs, n, LANES), F32),
                   jax.ShapeDtypeStruct(ck4.shape, F32), jax.ShapeDtypeStruct(cv4.shape, F32)],
        compiler_params=pltpu.CompilerParams(
            dimension_semantics=("arbitrary",), vmem_limit_bytes=VMEM_LIMIT),
        name="sample_attn",
    )(q, k, v, ck4, cv4, p["sinks"], p["bias_sample"])


def _sample_merge_kernel(x_ref, oa_ref, ob_ref, sga_ref, sgb_ref, wa_ref, wb_ref, wout_ref, y_ref):
    oa = jnp.concatenate([oa_ref[p] for p in range(oa_ref.shape[0])], axis=1)
    ya = _dot(oa.astype(BF16), wa_ref[...])
    yb = _dot(ob_ref[...].astype(BF16), wb_ref[...])
    hm = sga_ref[...] * ya + sgb_ref[...] * yb
    y_ref[...] = x_ref[...] + _dot(hm.astype(BF16), wout_ref[...])


def _sample_merge(x, oa, ob, sga, sgb, p):
    n, d = x.shape
    return pl.pallas_call(
        _sample_merge_kernel,
        grid=(1,),
        in_specs=[_const_spec((n, d)), _const_spec(oa.shape), _const_spec((n, SGU_WIDTH)),
                  _const_spec((n, d)), _const_spec((n, d)),
                  _const_spec((ATTN_WIDTH, d)), _const_spec((SGU_WIDTH, d)), _const_spec((d, d))],
        out_specs=_const_spec((n, d), False),
        out_shape=jax.ShapeDtypeStruct((n, d), F32),
        compiler_params=pltpu.CompilerParams(
            dimension_semantics=("arbitrary",), vmem_limit_bytes=VMEM_LIMIT),
        name="sample_merge",
    )(x, oa, ob, sga, sgb, p["wa"], p["wb"], p["wout"])


def _route(logits):
    lane = lax.broadcasted_iota(jnp.int32, logits.shape, 1).astype(F32)
    far = float(ROUTER_LANES)
    glm = jnp.where(lane < N_GROUPS, logits, NEG_INF)
    gmax = jnp.max(glm, axis=-1, keepdims=True)
    gidx = jnp.min(jnp.where(glm == gmax, lane, far), axis=-1, keepdims=True)
    gw = 1.0 / jnp.sum(jnp.exp(glm - gmax), axis=-1, keepdims=True)
    first = EXPERT_LANE0 + EXPERTS_PER_GROUP * gidx
    sel = (lane >= first) & (lane < first + EXPERTS_PER_GROUP)
    el = jnp.where(sel, logits, NEG_INF)
    t1 = jnp.max(el, axis=-1, keepdims=True)
    i1 = jnp.min(jnp.where(el == t1, lane, far), axis=-1, keepdims=True)
    el2 = jnp.where(lane == i1, NEG_INF, el)
    t2 = jnp.max(el2, axis=-1, keepdims=True)
    i2 = jnp.min(jnp.where(el2 == t2, lane, far), axis=-1, keepdims=True)
    e2 = jnp.exp(t2 - t1)
    den = 1.0 + e2
    w1 = (1.0 / den) * gw
    w2 = (e2 / den) * gw
    return gidx, jnp.where(lane == i1, w1, 0.0) + jnp.where(lane == i2, w2, 0.0)


def _split_bf16(x):
    hi = x.astype(BF16)
    lo = (x - hi.astype(F32)).astype(BF16)
    return hi, lo


def _moe_kernel(y_ref, g2_ref, wrc_ref, wrh_ref, br_ref, wg_ref, wu_ref, wd_ref, ex_ref,
                ltri_ref, eye_ref, o_ref, hs_ref, cws_ref, os_ref, hid_ref):
    tm = y_ref.shape[0]

    @pl.when(pl.program_id(0) == 0)
    def _():
        os_ref[...] = jnp.zeros_like(os_ref)

    y = y_ref[...]
    hf = _rms(y, g2_ref[...])
    hb, hl = _split_bf16(hf)
    part = _dot(hb, wrc_ref[...])
    logits = (part[:, :ROUTER_LANES] + (part[:, ROUTER_LANES:] + _dot(hl, wrh_ref[...]))
              + br_ref[...])
    gidx, cw = _route(logits)

    lane = lax.broadcasted_iota(jnp.int32, (tm, ROUTER_LANES), 1).astype(F32)
    onehot = jnp.where(lane == gidx, 1.0, 0.0)
    before = _dot(ltri_ref[...], onehot.astype(BF16))
    count = jnp.sum(onehot, axis=0, keepdims=True)
    n_chunks = jnp.floor((count + (MOE_CHUNK - 0.5)) * (1.0 / MOE_CHUNK))
    lane1 = lane[0:1]
    ends = []
    start_vec = jnp.zeros_like(n_chunks)
    end = jnp.zeros((1, 1), F32)
    for g in range(N_GROUPS):
        start_vec = start_vec + jnp.where(lane1 == g, end * MOE_CHUNK, 0.0)
        end = end + jnp.sum(jnp.where(lane1 == g, n_chunks, 0.0), axis=-1, keepdims=True)
        ends.append(end[0, 0].astype(jnp.int32))
    pos = jnp.sum(onehot * (before + start_vec), axis=-1, keepdims=True)

    slot_l = lax.broadcasted_iota(jnp.int32, (tm, MOE_SLOTS), 1).astype(F32)
    unsort = jnp.where(slot_l == pos, 1.0, 0.0).astype(BF16)
    pos_hi = jnp.floor(pos * (1.0 / SLOT_SPLIT))
    pos_lo = pos - pos_hi * SLOT_SPLIT
    pos_cols = jnp.where(lane == 0.0, pos_hi, jnp.where(lane == 1.0, pos_lo, 0.0)).astype(BF16)
    pos_t = _dot_nt(eye_ref[...], pos_cols)
    pos_row = pos_t[0:1] * SLOT_SPLIT + pos_t[1:2]
    slot_s = lax.broadcasted_iota(jnp.int32, (MOE_SLOTS, tm), 0).astype(F32)
    sort = jnp.where(slot_s == pos_row, 1.0, 0.0).astype(BF16)

    ch, cl = _split_bf16(cw)
    sorted_rows = _dot(sort, jnp.concatenate([hb, ch, cl], axis=1))
    hs_ref[...] = sorted_rows[:, :D_MODEL].astype(BF16)
    cws_ref[...] = (sorted_rows[:, D_MODEL:D_MODEL + ROUTER_LANES]
                    + sorted_rows[:, D_MODEL + ROUTER_LANES:])

    def group_of(c):
        return ((c >= ends[0]).astype(jnp.int32) + (c >= ends[1]).astype(jnp.int32)
                + (c >= ends[2]).astype(jnp.int32))

    def chunk_rows(c):
        return pl.ds(pl.multiple_of(c * MOE_CHUNK, BF16_ROWS), MOE_CHUNK)

    def hidden(c):
        g = group_of(c)
        hsc = hs_ref[chunk_rows(c), :]
        wh, wl = _split_bf16(cws_ref[chunk_rows(c), :])
        cexp = _dot(wh, ex_ref[...]) + _dot(wl, ex_ref[...])
        experts = [EXPERTS_PER_GROUP * g + e for e in range(EXPERTS_PER_GROUP)]
        a = jnp.concatenate([_dot(hsc, wg_ref[e]) for e in experts], axis=1)
        u = jnp.concatenate([_dot(hsc, wu_ref[e]) for e in experts], axis=1)
        return ((a * jax.nn.sigmoid(a)) * u * cexp).astype(BF16)

    def project_down(c, hid):
        os_ref[chunk_rows(c), :] = _dot(hid, wd_ref[group_of(c)]).astype(BF16)

    n_used = ends[N_GROUPS - 1]
    hid_ref[0] = hidden(0)

    def step(c, carry):
        slot = c & 1
        hid_ref[slot] = hidden(c)
        project_down(c - 1, hid_ref[1 - slot])
        return carry

    lax.fori_loop(1, n_used, step, 0)
    project_down(n_used - 1, hid_ref[(n_used - 1) & 1])
    o_ref[...] = y + _dot(unsort, os_ref[...])


def _moe(y, p):
    n, d = y.shape
    tm = MOE_TILE
    assert n % tm == 0
    row_spec = pl.BlockSpec((tm, d), lambda i: (i, 0))
    return pl.pallas_call(
        _moe_kernel,
        grid=(n // tm,),
        in_specs=[row_spec, _const_spec((1, d)), _const_spec((d, 2 * ROUTER_LANES)),
                  _const_spec((d, ROUTER_LANES)), _const_spec((1, ROUTER_LANES)),
                  _const_spec((N_EXPERTS, d, D_EXPERT)), _const_spec((N_EXPERTS, d, D_EXPERT)),
                  _const_spec((N_GROUPS, GROUP_HIDDEN, d)),
                  _const_spec((ROUTER_LANES, GROUP_HIDDEN)),
                  _const_spec((tm, tm)), _const_spec((LANES, LANES))],
        out_specs=row_spec,
        out_shape=jax.ShapeDtypeStruct((n, d), F32),
        scratch_shapes=[pltpu.VMEM((MOE_SLOTS, d), BF16), pltpu.VMEM((MOE_SLOTS, ROUTER_LANES), F32),
                        pltpu.VMEM((MOE_SLOTS, d), BF16), pltpu.VMEM((2, MOE_CHUNK, GROUP_HIDDEN), BF16)],
        compiler_params=pltpu.CompilerParams(
            dimension_semantics=("arbitrary",), vmem_limit_bytes=VMEM_LIMIT),
        name="expert_mixer",
    )(y, p["g2"], p["wr_cat"], p["wr_hi"], p["br"], p["wg"], p["wu"], p["wd"], p["expand"],
      p["ltri"], p["eye"])


def _window_bias(n_q, q_tok, key_prev_ok, key_cur_ok):
    half = np.concatenate([key_prev_ok, key_cur_ok], axis=1)
    ok = np.concatenate([half, half], axis=1)
    return np.where(ok, 0.0, NEG_INF).astype(np.float32)


def _prompt_bias():
    t = np.arange(WINDOW)[:, None]
    s = np.arange(WINDOW)[None, :]
    prev_ok = s > t
    cur_ok = s <= t
    normal = _window_bias(WINDOW, t, prev_ok, cur_ok)
    first = _window_bias(WINDOW, t, np.zeros_like(prev_ok), cur_ok)
    return np.stack([normal, first])


def _sample_bias():
    out = []
    r = np.arange(8)[:, None]
    t = r // 2
    s = np.arange(WINDOW)[None, :]
    prev_ok = s > t
    for hf in range(2):
        c = s
        cur_ok = (c < 8) & (c % 2 == hf) & (c // 2 <= t)
        out.append(_window_bias(8, t, np.broadcast_to(prev_ok, (8, WINDOW)), cur_ok))
    return np.stack(out)


def _block_diag_mean(width):
    idx = np.arange(width) // HEAD_DIM
    return (idx[:, None] == idx[None, :]).astype(np.float32) / HEAD_DIM


def _expand_matrix():
    ex = np.zeros((ROUTER_LANES, GROUP_HIDDEN), np.float32)
    for g in range(N_GROUPS):
        for e in range(EXPERTS_PER_GROUP):
            ex[EXPERT_LANE0 + EXPERTS_PER_GROUP * g + e, e * D_EXPERT:(e + 1) * D_EXPERT] = 1.0
    return ex


def _prepare(norm1_g, w_in, q_norm_g, k_norm_g, attn_sinks, ln_v_g, ln_v_b, w_spatial, b_spatial,
             w_branch_a, w_branch_b, w_out, norm2_g, w_router_group, b_router_group,
             w_router_expert, b_router_expert, w_exp_gate, w_exp_up, w_exp_down):
    p = {}
    p["g1"] = norm1_g.reshape(1, D_MODEL)
    p["w_in"] = w_in.astype(BF16)
    p["gq"] = (jnp.tile(q_norm_g, N_HEADS) * (HEAD_DIM ** -0.5)).reshape(1, ATTN_WIDTH)
    p["gk"] = jnp.tile(k_norm_g, N_KV_HEADS).reshape(1, KV_WIDTH)
    p["bq"] = jnp.asarray(_block_diag_mean(ATTN_WIDTH), BF16)
    p["bk"] = jnp.asarray(_block_diag_mean(KV_WIDTH), BF16)
    p["sinks"] = attn_sinks.astype(F32)
    p["lng"] = ln_v_g.reshape(1, SGU_WIDTH)
    p["lnb"] = ln_v_b.reshape(1, SGU_WIDTH)
    tril = jnp.tril(jnp.ones((CHUNK, CHUNK), F32))
    wsp = w_spatial * tril[None]
    p["wsp"] = wsp.astype(BF16)
    p["bsp"] = jnp.repeat(b_spatial.T, SGU_GROUP_DIM, axis=1)
    w4 = wsp[:, :4, :4]
    p["coef_s"] = jnp.repeat(jnp.transpose(w4, (1, 2, 0)).reshape(16, SGU_GROUPS),
                             SGU_GROUP_DIM, axis=1)
    p["bias_sgu_s"] = jnp.repeat(b_spatial[:, :4].T, SGU_GROUP_DIM, axis=1)
    p["wa"] = w_branch_a.astype(BF16)
    p["wb"] = w_branch_b.astype(BF16)
    p["wout"] = w_out.astype(BF16)
    p["g2"] = norm2_g.reshape(1, D_MODEL)
    wr = jnp.concatenate(
        [w_router_group, w_router_expert,
         jnp.zeros((D_MODEL, ROUTER_LANES - N_GROUPS - N_EXPERTS), F32)], axis=1)
    p["wr_hi"], wr_lo = _split_bf16(wr)
    p["wr_cat"] = jnp.concatenate([p["wr_hi"], wr_lo], axis=1)
    p["br"] = jnp.concatenate(
        [b_router_group, b_router_expert,
         jnp.zeros((ROUTER_LANES - N_GROUPS - N_EXPERTS,), F32)]).reshape(1, ROUTER_LANES)
    p["wg"] = w_exp_gate.astype(BF16)
    p["wu"] = w_exp_up.astype(BF16)
    p["wd"] = w_exp_down.reshape(N_GROUPS, GROUP_HIDDEN, D_MODEL).astype(BF16)
    p["expand"] = jnp.asarray(_expand_matrix(), BF16)
    p["ltri"] = jnp.asarray(np.tril(np.ones((MOE_TILE, MOE_TILE), np.float32), -1), BF16)
    p["eye"] = jnp.asarray(np.eye(LANES, dtype=np.float32), BF16)
    p["bias_prompt"] = jnp.asarray(_prompt_bias())
    p["bias_sample"] = jnp.asarray(_sample_bias())
    return p


def kernel(x_prompt, x_sample, cache_k_win, cache_v_win, norm1_g, w_in, q_norm_g, k_norm_g, attn_sinks, ln_v_g, ln_v_b, w_spatial, b_spatial, w_branch_a, w_branch_b, w_out, norm2_g, w_router_group, b_router_group, w_router_expert, b_router_expert, w_exp_gate, w_exp_up, w_exp_down):
    depth = norm1_g.shape[0]
    assert depth == 1
    batch, seq, d = x_prompt.shape
    dec_batch, dec_seq, _ = x_sample.shape
    assert dec_seq == 4 and d == D_MODEL
    p = _prepare(*(a[0] for a in (
        norm1_g, w_in, q_norm_g, k_norm_g, attn_sinks, ln_v_g, ln_v_b, w_spatial, b_spatial,
        w_branch_a, w_branch_b, w_out, norm2_g, w_router_group, b_router_group,
        w_router_expert, b_router_expert, w_exp_gate, w_exp_up, w_exp_down)))

    y1p, kwin, vwin = _prompt_mixer(x_prompt, p)
    yp = _moe(y1p.reshape(batch * seq, d), p).reshape(batch, seq, d)

    xs = jnp.transpose(x_sample, (1, 0, 2)).reshape(dec_seq * dec_batch, d)
    q, k, v, ob, vs, sga, sgb = _sample_in(xs, p)
    ck = cache_k_win[0].reshape(dec_batch, WINDOW, KV_WIDTH)
    cv = cache_v_win[0].reshape(dec_batch, WINDOW, KV_WIDTH)
    oa, nk, nv = _sample_attn(q, k, v, ck, cv, p)
    y1s = _sample_merge(xs, oa, ob, sga, sgb, p)
    ys = _moe(y1s, p)
    ys = jnp.transpose(ys.reshape(dec_seq, dec_batch, d), (1, 0, 2))
    vs_out = jnp.transpose(vs.reshape(dec_seq, dec_batch, SGU_GROUPS, SGU_GROUP_DIM), (1, 0, 2, 3))

    kv_shape = (WINDOW, N_KV_HEADS, HEAD_DIM)
    return (yp, ys,
            kwin.reshape(1, batch, *kv_shape), vwin.reshape(1, batch, *kv_shape),
            nk.reshape(1, dec_batch, *kv_shape), nv.reshape(1, dec_batch, *kv_shape),
            vs_out[None])
```

```python
import functools

import numpy as np
import jax
import jax.numpy as jnp
from jax import lax
from jax.experimental import pallas as pl
from jax.experimental.pallas import tpu as pltpu

F32 = jnp.float32
BF16 = jnp.bfloat16

D_MODEL = 1024
N_HEADS = 8
N_KV_HEADS = 2
HEAD_DIM = 64
WINDOW = 128
ATTN_WIDTH = N_HEADS * HEAD_DIM
KV_WIDTH = N_KV_HEADS * HEAD_DIM
SGU_GROUPS = 4
SGU_WIDTH = D_MODEL // 2
SGU_GROUP_DIM = SGU_WIDTH // SGU_GROUPS
CHUNK = 128
N_GROUPS = 4
EXPERTS_PER_GROUP = 4
N_EXPERTS = N_GROUPS * EXPERTS_PER_GROUP
D_EXPERT = 256
GROUP_HIDDEN = EXPERTS_PER_GROUP * D_EXPERT
D_IN = ATTN_WIDTH + 2 * KV_WIDTH + 2 * SGU_WIDTH + 2 * D_MODEL
EPS = 1e-6
NEG_INF = -1e30

C_Q = 0
C_K = C_Q + ATTN_WIDTH
C_V = C_K + KV_WIDTH
C_U = C_V + KV_WIDTH
C_VS = C_U + SGU_WIDTH
C_GA = C_VS + SGU_WIDTH
C_GB = C_GA + D_MODEL

LANES = 128
ROUTER_LANES = LANES
EXPERT_LANE0 = N_GROUPS
PROMPT_TILE = 512
MOE_TILE = 512
MOE_CHUNK = 64
MOE_MAX_CHUNKS = (MOE_TILE + N_GROUPS * (MOE_CHUNK - 1)) // MOE_CHUNK
MOE_SLOTS = -(-MOE_MAX_CHUNKS * MOE_CHUNK // LANES) * LANES
SLOT_SPLIT = 128
SAMPLE_BATCH_TILE = 8
VMEM_LIMIT = 56 * 1024 * 1024

_SQRT_2_OVER_PI = np.sqrt(2.0 / np.pi).astype(np.float32)


def _dot(a, b):
    return jnp.dot(a, b, preferred_element_type=F32)


def _dot_nt(a, b):
    return lax.dot_general(a, b, (((1,), (1,)), ((), ())), preferred_element_type=F32)


def _gelu(x):
    cdf = 0.5 * (1.0 + jnp.tanh(_SQRT_2_OVER_PI * (x + 0.044715 * (x * x * x))))
    return x * cdf


def _rms(x, g):
    return x * lax.rsqrt(jnp.mean(x * x, axis=-1, keepdims=True) + EPS) * g


def _head_rms(x, blockdiag, g):
    ms = _dot((x * x).astype(BF16), blockdiag)
    return x * lax.rsqrt(ms + EPS) * g


def _qkv(x, g1, win_ref, gq, gk, bq, bk):
    h = _rms(x, g1).astype(BF16)
    qn = _head_rms(_dot(h, win_ref[:, C_Q:C_K]), bq, gq)
    kv = _dot(h, win_ref[:, C_K:C_U])
    kn = _head_rms(kv[:, :KV_WIDTH], bk, gk)
    v = kv[:, KV_WIDTH:]
    return h, qn, kn, v


def _sgu_inputs(h, win_ref, lng, lnb):
    gu = _gelu(_dot(h, win_ref[:, C_U:C_VS]))
    gv = _gelu(_dot(h, win_ref[:, C_VS:C_GA]))
    mu = jnp.mean(gv, axis=-1, keepdims=True)
    xc = gv - mu
    vsn = xc * lax.rsqrt(jnp.mean(xc * xc, axis=-1, keepdims=True) + EPS) * lng + lnb
    return gu, vsn


def _head_variants(prev, prev_rot, cur, cur_rot):
    a = jnp.concatenate([prev, cur], axis=0)
    r = jnp.concatenate([prev_rot, cur_rot], axis=0)
    lo = lax.broadcasted_iota(jnp.int32, a.shape, 1) < HEAD_DIM
    zero = jnp.zeros_like(a)
    kv0 = jnp.concatenate([jnp.where(lo, a, zero), jnp.where(lo, zero, r)], axis=0)
    kv1 = jnp.concatenate([jnp.where(lo, r, zero), jnp.where(lo, zero, a)], axis=0)
    return kv0, kv1


def _lane_slabs(x):
    return [x[:, p * LANES:(p + 1) * LANES] for p in range(x.shape[1] // LANES)]


def _scores(q_slabs, kcat, bias):
    slabs_per_kv = len(q_slabs) // N_KV_HEADS
    return [_dot_nt(q.astype(BF16), kcat[p // slabs_per_kv]) + bias for p, q in enumerate(q_slabs)]


def _sink_softmax(scores, sinks_ref):
    out = []
    for p, s in enumerate(scores):
        n_keys = s.shape[1] // 2
        probs = []
        for par in range(2):
            sh = s[:, par * n_keys:(par + 1) * n_keys]
            sink = sinks_ref[2 * p + par]
            m = jnp.maximum(jnp.max(sh, axis=-1, keepdims=True), sink)
            e = jnp.exp(sh - m)
            den = jnp.sum(e, axis=-1, keepdims=True) + jnp.exp(sink - m)
            probs.append((e / den).astype(BF16))
        out.append(jnp.concatenate(probs, axis=1))
    return out


def _weighted_values(probs, vcat):
    slabs_per_kv = len(probs) // N_KV_HEADS
    return [_dot(pr, vcat[p // slabs_per_kv]) for p, pr in enumerate(probs)]


def _merge(x, h, oa, ob, win_ref, wa_ref, wb_ref, wout_ref):
    ya = _dot(oa.astype(BF16), wa_ref[...])
    yb = _dot(ob.astype(BF16), wb_ref[...])
    ga = _dot(h, win_ref[:, C_GA:C_GB])
    gb = _dot(h, win_ref[:, C_GB:D_IN])
    hm = jax.nn.sigmoid(ga) * ya + jax.nn.sigmoid(gb) * yb
    return x + _dot(hm.astype(BF16), wout_ref[...])


def _prompt_kernel(x_ref, g1_ref, win_ref, gq_ref, gk_ref, bq_ref, bk_ref, sinks_ref, lng_ref,
                   lnb_ref, wsp_ref, bsp_ref, wa_ref, wb_ref, wout_ref, bias_ref,
                   y_ref, kwin_ref, vwin_ref,
                   kprev, kprev_rot, vprev, vprev_rot):
    j = pl.program_id(1)

    @pl.when(j == 0)
    def _():
        for r in (kprev, kprev_rot, vprev, vprev_rot):
            r[...] = jnp.zeros_like(r)

    x = x_ref[0]
    h, qn, kn, v = _qkv(x, g1_ref[...], win_ref, gq_ref[...], gk_ref[...], bq_ref[...], bk_ref[...])

    kb = kn.astype(BF16)
    kr = pltpu.roll(kn, HEAD_DIM, 1).astype(BF16)
    vb = v.astype(BF16)
    vr = pltpu.roll(v, HEAD_DIM, 1).astype(BF16)

    n_blocks = x.shape[0] // WINDOW
    blocks = [slice(i * WINDOW, (i + 1) * WINDOW) for i in range(n_blocks)]
    scores, vcats = [], []
    pk, pkr, pv, pvr = kprev[...], kprev_rot[...], vprev[...], vprev_rot[...]
    for i, rows in enumerate(blocks):
        ck, ckr, cv, cvr = kb[rows], kr[rows], vb[rows], vr[rows]
        kcat = _head_variants(pk, pkr, ck, ckr)
        vcats.append(_head_variants(pv, pvr, cv, cvr))
        bias = bias_ref[jnp.where(j == 0, 1, 0)] if i == 0 else bias_ref[0]
        scores.append(_scores(_lane_slabs(qn[rows]), kcat, bias))
        pk, pkr, pv, pvr = ck, ckr, cv, cvr
    kprev[...] = pk
    kprev_rot[...] = pkr
    vprev[...] = pv
    vprev_rot[...] = pvr

    gu, vsn = _sgu_inputs(h, win_ref, lng_ref[...], lnb_ref[...])
    vsb = vsn.astype(BF16)
    ob = jnp.concatenate([
        gu[rows] * (jnp.concatenate(
            [_dot(wsp_ref[g], vsb[rows, g * SGU_GROUP_DIM:(g + 1) * SGU_GROUP_DIM])
             for g in range(SGU_GROUPS)], axis=1) + bsp_ref[...])
        for rows in blocks], axis=0)

    probs = [_sink_softmax(s, sinks_ref) for s in scores]
    oa = jnp.concatenate(
        [jnp.concatenate(_weighted_values(pr, vc), axis=1) for pr, vc in zip(probs, vcats)], axis=0)
    y_ref[0] = _merge(x, h, oa, ob, win_ref, wa_ref, wb_ref, wout_ref)

    @pl.when(j == pl.num_programs(1) - 1)
    def _():
        kwin_ref[0] = kn[(n_blocks - 1) * WINDOW:]
        vwin_ref[0] = v[(n_blocks - 1) * WINDOW:]


def _const_spec(shape, single_buffer=True):
    nd = len(shape)
    mode = pl.Buffered(1) if single_buffer else None
    return pl.BlockSpec(shape, lambda *_: (0,) * nd, pipeline_mode=mode)


def _smem_spec():
    return pl.BlockSpec(memory_space=pltpu.SMEM)


def _prompt_mixer(x, p):
    b, s, d = x.shape
    ts = PROMPT_TILE
    assert s % ts == 0 and ts % WINDOW == 0
    in_specs = [
        pl.BlockSpec((1, ts, d), lambda bi, j: (bi, j, 0)),
        _const_spec((1, d)), _const_spec((d, D_IN)),
        _const_spec((1, ATTN_WIDTH)), _const_spec((1, KV_WIDTH)),
        _const_spec((ATTN_WIDTH, ATTN_WIDTH)), _const_spec((KV_WIDTH, KV_WIDTH)),
        _smem_spec(),
        _const_spec((1, SGU_WIDTH)), _const_spec((1, SGU_WIDTH)),
        _const_spec((SGU_GROUPS, CHUNK, CHUNK)), _const_spec((CHUNK, SGU_WIDTH)),
        _const_spec((ATTN_WIDTH, d)), _const_spec((SGU_WIDTH, d)), _const_spec((d, d)),
        _const_spec((2, WINDOW, 4 * WINDOW)),
    ]
    out_specs = [
        pl.BlockSpec((1, ts, d), lambda bi, j: (bi, j, 0)),
        pl.BlockSpec((1, WINDOW, KV_WIDTH), lambda bi, j: (bi, 0, 0)),
        pl.BlockSpec((1, WINDOW, KV_WIDTH), lambda bi, j: (bi, 0, 0)),
    ]
    out_shape = [
        jax.ShapeDtypeStruct((b, s, d), F32),
        jax.ShapeDtypeStruct((b, WINDOW, KV_WIDTH), F32),
        jax.ShapeDtypeStruct((b, WINDOW, KV_WIDTH), F32),
    ]
    return pl.pallas_call(
        _prompt_kernel,
        grid=(b, s // ts),
        in_specs=in_specs, out_specs=out_specs, out_shape=out_shape,
        scratch_shapes=[pltpu.VMEM((WINDOW, KV_WIDTH), BF16)] * 4,
        compiler_params=pltpu.CompilerParams(
            dimension_semantics=("arbitrary", "arbitrary"), vmem_limit_bytes=VMEM_LIMIT),
        name="prompt_mixer",
    )(x, p["g1"], p["w_in"], p["gq"], p["gk"], p["bq"], p["bk"], p["sinks"], p["lng"], p["lnb"],
      p["wsp"], p["bsp"], p["wa"], p["wb"], p["wout"], p["bias_prompt"])


def _sample_in_kernel(x_ref, g1_ref, win_ref, gq_ref, gk_ref, bq_ref, bk_ref, lng_ref, lnb_ref,
                      coef_ref, sbias_ref,
                      q_ref, k_ref, v_ref, ob_ref, vs_ref, sga_ref, sgb_ref):
    x = x_ref[...]
    h, qn, kn, v = _qkv(x, g1_ref[...], win_ref, gq_ref[...], gk_ref[...], bq_ref[...], bk_ref[...])
    gu, vsn = _sgu_inputs(h, win_ref, lng_ref[...], lnb_ref[...])
    for p, slab in enumerate(_lane_slabs(qn)):
        q_ref[p] = slab
    k_ref[...] = kn
    v_ref[...] = v
    vs_ref[...] = vsn
    sga_ref[...] = jax.nn.sigmoid(_dot(h, win_ref[:, C_GA:C_GB]))
    sgb_ref[...] = jax.nn.sigmoid(_dot(h, win_ref[:, C_GB:D_IN]))
    nb = x.shape[0] // 4
    for t in range(4):
        mixed = sbias_ref[t:t + 1, :]
        for jj in range(t + 1):
            mixed = mixed + coef_ref[4 * t + jj:4 * t + jj + 1, :] * vsn[jj * nb:(jj + 1) * nb]
        ob_ref[t * nb:(t + 1) * nb, :] = gu[t * nb:(t + 1) * nb] * mixed


def _sample_in(x, p):
    n, d = x.shape
    shapes = [(ATTN_WIDTH // LANES, n, LANES), (n, KV_WIDTH), (n, KV_WIDTH), (n, SGU_WIDTH),
              (n, SGU_WIDTH), (n, d), (n, d)]
    return pl.pallas_call(
        _sample_in_kernel,
        grid=(1,),
        in_specs=[_const_spec((n, d)), _const_spec((1, d)), _const_spec((d, D_IN)),
                  _const_spec((1, ATTN_WIDTH)), _const_spec((1, KV_WIDTH)),
                  _const_spec((ATTN_WIDTH, ATTN_WIDTH)), _const_spec((KV_WIDTH, KV_WIDTH)),
                  _const_spec((1, SGU_WIDTH)), _const_spec((1, SGU_WIDTH)),
                  _const_spec((16, SGU_WIDTH)), _const_spec((4, SGU_WIDTH))],
        out_specs=[_const_spec(sh, False) for sh in shapes],
        out_shape=[jax.ShapeDtypeStruct(sh, F32) for sh in shapes],
        compiler_params=pltpu.CompilerParams(
            dimension_semantics=("arbitrary",), vmem_limit_bytes=VMEM_LIMIT),
        name="sample_in",
    )(x, p["g1"], p["w_in"], p["gq"], p["gk"], p["bq"], p["bk"], p["lng"], p["lnb"],
      p["coef_s"], p["bias_sgu_s"])


def _sample_attn_kernel(q_ref, k_ref, v_ref, ck_ref, cv_ref, sinks_ref, bias_ref,
                        oa_ref, nk_ref, nv_ref):
    i = pl.program_id(0)
    n_slabs = q_ref.shape[0]
    half_rows = q_ref.shape[1] // 8
    pad = jnp.zeros((WINDOW - 8, KV_WIDTH), BF16)
    row_half = lax.broadcasted_iota(jnp.int32, (8, LANES), 0) % 2

    work = []
    for bb in range(SAMPLE_BATCH_TILE):
        b_lo = i * SAMPLE_BATCH_TILE + bb
        rows = pl.ds(b_lo, 8, stride=half_rows)
        q8 = [q_ref[p, rows, :] for p in range(n_slabs)]
        k8 = k_ref[rows, :]
        v8 = v_ref[rows, :]
        ck_new = jnp.concatenate([k8.astype(BF16), pad], axis=0)
        ckr_new = jnp.concatenate([pltpu.roll(k8, HEAD_DIM, 1).astype(BF16), pad], axis=0)
        cv_new = jnp.concatenate([v8.astype(BF16), pad], axis=0)
        cvr_new = jnp.concatenate([pltpu.roll(v8, HEAD_DIM, 1).astype(BF16), pad], axis=0)
        for hf in range(2):
            kc = ck_ref[hf, bb]
            vc = cv_ref[hf, bb]
            kcat = _head_variants(kc.astype(BF16), pltpu.roll(kc, HEAD_DIM, 1).astype(BF16),
                                  ck_new, ckr_new)
            vcat = _head_variants(vc.astype(BF16), pltpu.roll(vc, HEAD_DIM, 1).astype(BF16),
                                  cv_new, cvr_new)
            work.append((rows, hf, _scores(q8, kcat, bias_ref[hf]), vcat))
            nk_ref[hf, bb] = pltpu.roll(kc, WINDOW - 4, 0)
            nv_ref[hf, bb] = pltpu.roll(vc, WINDOW - 4, 0)
            for t in range(4):
                nk_ref[hf, bb, WINDOW - 4 + t:WINDOW - 3 + t, :] = k8[2 * t + hf:2 * t + hf + 1, :]
                nv_ref[hf, bb, WINDOW - 4 + t:WINDOW - 3 + t, :] = v8[2 * t + hf:2 * t + hf + 1, :]
    probs = [_sink_softmax(s, sinks_ref) for _, _, s, _ in work]
    outs = [_weighted_values(pr, w[3]) for pr, w in zip(probs, work)]
    for (rows, _, _, _), o_even, o_odd in zip(work[0::2], outs[0::2], outs[1::2]):
        for p in range(n_slabs):
            oa_ref[p, rows, :] = jnp.where(row_half == 0, o_even[p], o_odd[p])


def _sample_attn(q, k, v, ck, cv, p):
    n_slabs, n, _ = q.shape
    nb = n // 4
    half = nb // 2
    bt = SAMPLE_BATCH_TILE
    assert half % bt == 0
    ck4 = ck.reshape(2, half, WINDOW, KV_WIDTH)
    cv4 = cv.reshape(2, half, WINDOW, KV_WIDTH)
    cache_spec = pl.BlockSpec((2, bt, WINDOW, KV_WIDTH), lambda i: (0, i, 0, 0))
    return pl.pallas_call(
        _sample_attn_kernel,
        grid=(half // bt,),
        in_specs=[_const_spec((n_slabs, n, LANES)), _const_spec((n, KV_WIDTH)),
                  _const_spec((n, KV_WIDTH)), cache_spec, cache_spec, _smem_spec(),
                  _const_spec((2, 8, 4 * WINDOW))],
        out_specs=[_const_spec((n_slabs, n, LANES), False), cache_spec, cache_spec],
        out_shape=[jax.ShapeDtypeStruct((n_slabs, n, LANES), F32),
                   jax.ShapeDtypeStruct(ck4.shape, F32), jax.ShapeDtypeStruct(cv4.shape, F32)],
        compiler_params=pltpu.CompilerParams(
            dimension_semantics=("arbitrary",), vmem_limit_bytes=VMEM_LIMIT),
        name="sample_attn",
    )(q, k, v, ck4, cv4, p["sinks"], p["bias_sample"])


def _sample_merge_kernel(x_ref, oa_ref, ob_ref, sga_ref, sgb_ref, wa_ref, wb_ref, wout_ref, y_ref):
    oa = jnp.concatenate([oa_ref[p] for p in range(oa_ref.shape[0])], axis=1)
    ya = _dot(oa.astype(BF16), wa_ref[...])
    yb = _dot(ob_ref[...].astype(BF16), wb_ref[...])
    hm = sga_ref[...] * ya + sgb_ref[...] * yb
    y_ref[...] = x_ref[...] + _dot(hm.astype(BF16), wout_ref[...])


def _sample_merge(x, oa, ob, sga, sgb, p):
    n, d = x.shape
    return pl.pallas_call(
        _sample_merge_kernel,
        grid=(1,),
        in_specs=[_const_spec((n, d)), _const_spec(oa.shape), _const_spec((n, SGU_WIDTH)),
                  _const_spec((n, d)), _const_spec((n, d)),
                  _const_spec((ATTN_WIDTH, d)), _const_spec((SGU_WIDTH, d)), _const_spec((d, d))],
        out_specs=_const_spec((n, d), False),
        out_shape=jax.ShapeDtypeStruct((n, d), F32),
        compiler_params=pltpu.CompilerParams(
            dimension_semantics=("arbitrary",), vmem_limit_bytes=VMEM_LIMIT),
        name="sample_merge",
    )(x, oa, ob, sga, sgb, p["wa"], p["wb"], p["wout"])


def _route(logits):
    lane = lax.broadcasted_iota(jnp.int32, logits.shape, 1).astype(F32)
    far = float(ROUTER_LANES)
    glm = jnp.where(lane < N_GROUPS, logits, NEG_INF)
    gmax = jnp.max(glm, axis=-1, keepdims=True)
    gidx = jnp.min(jnp.where(glm == gmax, lane, far), axis=-1, keepdims=True)
    gw = 1.0 / jnp.sum(jnp.exp(glm - gmax), axis=-1, keepdims=True)
    first = EXPERT_LANE0 + EXPERTS_PER_GROUP * gidx
    sel = (lane >= first) & (lane < first + EXPERTS_PER_GROUP)
    el = jnp.where(sel, logits, NEG_INF)
    t1 = jnp.max(el, axis=-1, keepdims=True)
    i1 = jnp.min(jnp.where(el == t1, lane, far), axis=-1, keepdims=True)
    el2 = jnp.where(lane == i1, NEG_INF, el)
    t2 = jnp.max(el2, axis=-1, keepdims=True)
    i2 = jnp.min(jnp.where(el2 == t2, lane, far), axis=-1, keepdims=True)
    e2 = jnp.exp(t2 - t1)
    den = 1.0 + e2
    w1 = (1.0 / den) * gw
    w2 = (e2 / den) * gw
    return gidx, jnp.where(lane == i1, w1, 0.0) + jnp.where(lane == i2, w2, 0.0)


def _split_bf16(x):
    hi = x.astype(BF16)
    lo = (x - hi.astype(F32)).astype(BF16)
    return hi, lo


def _moe_kernel(y_ref, g2_ref, wrc_ref, wrh_ref, br_ref, wg_ref, wu_ref, wd_ref, ex_ref,
                ltri_ref, eye_ref, o_ref, hs_ref, cws_ref, os_ref):
    tm = y_ref.shape[0]

    @pl.when(pl.program_id(0) == 0)
    def _():
        os_ref[...] = jnp.zeros_like(os_ref)

    y = y_ref[...]
    hf = _rms(y, g2_ref[...])
    hb, hl = _split_bf16(hf)
    part = _dot(hb, wrc_ref[...])
    logits = (part[:, :ROUTER_LANES] + (part[:, ROUTER_LANES:] + _dot(hl, wrh_ref[...]))
              + br_ref[...])
    gidx, cw = _route(logits)

    lane = lax.broadcasted_iota(jnp.int32, (tm, ROUTER_LANES), 1).astype(F32)
    onehot = jnp.where(lane == gidx, 1.0, 0.0)
    before = _dot(ltri_ref[...], onehot.astype(BF16))
    count = jnp.sum(onehot, axis=0, keepdims=True)
    n_chunks = jnp.floor((count + (MOE_CHUNK - 1)) * (1.0 / MOE_CHUNK))
    lane1 = lane[0:1]
    ends = []
    start_vec = jnp.zeros_like(n_chunks)
    end = jnp.zeros((1, 1), F32)
    for g in range(N_GROUPS):
        start_vec = start_vec + jnp.where(lane1 == g, end * MOE_CHUNK, 0.0)
        end = end + jnp.sum(jnp.where(lane1 == g, n_chunks, 0.0), axis=-1, keepdims=True)
        ends.append(end[0, 0].astype(jnp.int32))
    pos = jnp.sum(onehot * (before + start_vec), axis=-1, keepdims=True)

    slot_l = lax.broadcasted_iota(jnp.int32, (tm, MOE_SLOTS), 1).astype(F32)
    unsort = jnp.where(slot_l == pos, 1.0, 0.0).astype(BF16)
    pos_hi = jnp.floor(pos * (1.0 / SLOT_SPLIT))
    pos_lo = pos - pos_hi * SLOT_SPLIT
    pos_cols = jnp.where(lane == 0.0, pos_hi, jnp.where(lane == 1.0, pos_lo, 0.0)).astype(BF16)
    pos_t = _dot_nt(eye_ref[...], pos_cols)
    pos_row = pos_t[0:1] * SLOT_SPLIT + pos_t[1:2]
    slot_s = lax.broadcasted_iota(jnp.int32, (MOE_SLOTS, tm), 0).astype(F32)
    sort = jnp.where(slot_s == pos_row, 1.0, 0.0).astype(BF16)

    ch, cl = _split_bf16(cw)
    sorted_rows = _dot(sort, jnp.concatenate([hb, ch, cl], axis=1))
    hs_ref[...] = sorted_rows[:, :D_MODEL].astype(BF16)
    cws_ref[...] = (sorted_rows[:, D_MODEL:D_MODEL + ROUTER_LANES]
                    + sorted_rows[:, D_MODEL + ROUTER_LANES:])

    def run_group(g, first_chunk, n_rows):
        rows = pl.ds(pl.multiple_of(first_chunk * MOE_CHUNK, MOE_CHUNK), n_rows)
        hsc = hs_ref[rows, :]
        wh, wl = _split_bf16(cws_ref[rows, :])
        cexp = _dot(wh, ex_ref[...]) + _dot(wl, ex_ref[...])
        experts = [EXPERTS_PER_GROUP * g + e for e in range(EXPERTS_PER_GROUP)]
        a = jnp.concatenate([_dot(hsc, wg_ref[e]) for e in experts], axis=1)
        u = jnp.concatenate([_dot(hsc, wu_ref[e]) for e in experts], axis=1)
        hid = (a * jax.nn.sigmoid(a)) * u * cexp
        os_ref[rows, :] = _dot(hid.astype(BF16), wd_ref[g]).astype(BF16)

    def group(g, carry):
        first = jnp.where(g == 0, 0, jnp.where(g == 1, ends[0], jnp.where(g == 2, ends[1], ends[2])))
        last = jnp.where(g == 0, ends[0], jnp.where(g == 1, ends[1],
                                                    jnp.where(g == 2, ends[2], ends[3])))
        n = last - first
        for k in range(1, MOE_TILE // MOE_CHUNK + 1):
            @pl.when(n == k)
            def _(k=k):
                run_group(g, first, k * MOE_CHUNK)
        return carry

    lax.fori_loop(0, N_GROUPS, group, 0)
    o_ref[...] = y + _dot(unsort, os_ref[...])


def _moe(y, p):
    n, d = y.shape
    tm = MOE_TILE
    assert n % tm == 0
    row_spec = pl.BlockSpec((tm, d), lambda i: (i, 0))
    return pl.pallas_call(
        _moe_kernel,
        grid=(n // tm,),
        in_specs=[row_spec, _const_spec((1, d)), _const_spec((d, 2 * ROUTER_LANES)),
                  _const_spec((d, ROUTER_LANES)), _const_spec((1, ROUTER_LANES)),
                  _const_spec((N_EXPERTS, d, D_EXPERT)), _const_spec((N_EXPERTS, d, D_EXPERT)),
                  _const_spec((N_GROUPS, GROUP_HIDDEN, d)),
                  _const_spec((ROUTER_LANES, GROUP_HIDDEN)),
                  _const_spec((tm, tm)), _const_spec((LANES, LANES))],
        out_specs=row_spec,
        out_shape=jax.ShapeDtypeStruct((n, d), F32),
        scratch_shapes=[pltpu.VMEM((MOE_SLOTS, d), BF16), pltpu.VMEM((MOE_SLOTS, ROUTER_LANES), F32),
                        pltpu.VMEM((MOE_SLOTS, d), BF16)],
        compiler_params=pltpu.CompilerParams(
            dimension_semantics=("arbitrary",), vmem_limit_bytes=VMEM_LIMIT),
        name="expert_mixer",
    )(y, p["g2"], p["wr_cat"], p["wr_hi"], p["br"], p["wg"], p["wu"], p["wd"], p["expand"],
      p["ltri"], p["eye"])


def _window_bias(n_q, q_tok, key_prev_ok, key_cur_ok):
    half = np.concatenate([key_prev_ok, key_cur_ok], axis=1)
    ok = np.concatenate([half, half], axis=1)
    return np.where(ok, 0.0, NEG_INF).astype(np.float32)


def _prompt_bias():
    t = np.arange(WINDOW)[:, None]
    s = np.arange(WINDOW)[None, :]
    prev_ok = s > t
    cur_ok = s <= t
    normal = _window_bias(WINDOW, t, prev_ok, cur_ok)
    first = _window_bias(WINDOW, t, np.zeros_like(prev_ok), cur_ok)
    return np.stack([normal, first])


def _sample_bias():
    out = []
    r = np.arange(8)[:, None]
    t = r // 2
    s = np.arange(WINDOW)[None, :]
    prev_ok = s > t
    for hf in range(2):
        c = s
        cur_ok = (c < 8) & (c % 2 == hf) & (c // 2 <= t)
        out.append(_window_bias(8, t, np.broadcast_to(prev_ok, (8, WINDOW)), cur_ok))
    return np.stack(out)


def _block_diag_mean(width):
    idx = np.arange(width) // HEAD_DIM
    return (idx[:, None] == idx[None, :]).astype(np.float32) / HEAD_DIM


def _expand_matrix():
    ex = np.zeros((ROUTER_LANES, GROUP_HIDDEN), np.float32)
    for g in range(N_GROUPS):
        for e in range(EXPERTS_PER_GROUP):
            ex[EXPERT_LANE0 + EXPERTS_PER_GROUP * g + e, e * D_EXPERT:(e + 1) * D_EXPERT] = 1.0
    return ex


def _prepare(norm1_g, w_in, q_norm_g, k_norm_g, attn_sinks, ln_v_g, ln_v_b, w_spatial, b_spatial,
             w_branch_a, w_branch_b, w_out, norm2_g, w_router_group, b_router_group,
             w_router_expert, b_router_expert, w_exp_gate, w_exp_up, w_exp_down):
    p = {}
    p["g1"] = norm1_g.reshape(1, D_MODEL)
    p["w_in"] = w_in.astype(BF16)
    p["gq"] = (jnp.tile(q_norm_g, N_HEADS) * (HEAD_DIM ** -0.5)).reshape(1, ATTN_WIDTH)
    p["gk"] = jnp.tile(k_norm_g, N_KV_HEADS).reshape(1, KV_WIDTH)
    p["bq"] = jnp.asarray(_block_diag_mean(ATTN_WIDTH), BF16)
    p["bk"] = jnp.asarray(_block_diag_mean(KV_WIDTH), BF16)
    p["sinks"] = attn_sinks.astype(F32)
    p["lng"] = ln_v_g.reshape(1, SGU_WIDTH)
    p["lnb"] = ln_v_b.reshape(1, SGU_WIDTH)
    tril = jnp.tril(jnp.ones((CHUNK, CHUNK), F32))
    wsp = w_spatial * tril[None]
    p["wsp"] = wsp.astype(BF16)
    p["bsp"] = jnp.repeat(b_spatial.T, SGU_GROUP_DIM, axis=1)
    w4 = wsp[:, :4, :4]
    p["coef_s"] = jnp.repeat(jnp.transpose(w4, (1, 2, 0)).reshape(16, SGU_GROUPS),
                             SGU_GROUP_DIM, axis=1)
    p["bias_sgu_s"] = jnp.repeat(b_spatial[:, :4].T, SGU_GROUP_DIM, axis=1)
    p["wa"] = w_branch_a.astype(BF16)
    p["wb"] = w_branch_b.astype(BF16)
    p["wout"] = w_out.astype(BF16)
    p["g2"] = norm2_g.reshape(1, D_MODEL)
    wr = jnp.concatenate(
        [w_router_group, w_router_expert,
         jnp.zeros((D_MODEL, ROUTER_LANES - N_GROUPS - N_EXPERTS), F32)], axis=1)
    p["wr_hi"], wr_lo = _split_bf16(wr)
    p["wr_cat"] = jnp.concatenate([p["wr_hi"], wr_lo], axis=1)
    p["br"] = jnp.concatenate(
        [b_router_group, b_router_expert,
         jnp.zeros((ROUTER_LANES - N_GROUPS - N_EXPERTS,), F32)]).reshape(1, ROUTER_LANES)
    p["wg"] = w_exp_gate.astype(BF16)
    p["wu"] = w_exp_up.astype(BF16)
    p["wd"] = w_exp_down.reshape(N_GROUPS, GROUP_HIDDEN, D_MODEL).astype(BF16)
    p["expand"] = jnp.asarray(_expand_matrix(), BF16)
    p["ltri"] = jnp.asarray(np.tril(np.ones((MOE_TILE, MOE_TILE), np.float32), -1), BF16)
    p["eye"] = jnp.asarray(np.eye(LANES, dtype=np.float32), BF16)
    p["bias_prompt"] = jnp.asarray(_prompt_bias())
    p["bias_sample"] = jnp.asarray(_sample_bias())
    return p


def kernel(x_prompt, x_sample, cache_k_win, cache_v_win, norm1_g, w_in, q_norm_g, k_norm_g, attn_sinks, ln_v_g, ln_v_b, w_spatial, b_spatial, w_branch_a, w_branch_b, w_out, norm2_g, w_router_group, b_router_group, w_router_expert, b_router_expert, w_exp_gate, w_exp_up, w_exp_down):
    depth = norm1_g.shape[0]
    assert depth == 1
    batch, seq, d = x_prompt.shape
    dec_batch, dec_seq, _ = x_sample.shape
    assert dec_seq == 4 and d == D_MODEL
    p = _prepare(*(a[0] for a in (
        norm1_g, w_in, q_norm_g, k_norm_g, attn_sinks, ln_v_g, ln_v_b, w_spatial, b_spatial,
        w_branch_a, w_branch_b, w_out, norm2_g, w_router_group, b_router_group,
        w_router_expert, b_router_expert, w_exp_gate, w_exp_up, w_exp_down)))

    y1p, kwin, vwin = _prompt_mixer(x_prompt, p)
    yp = _moe(y1p.reshape(batch * seq, d), p).reshape(batch, seq, d)

    xs = jnp.transpose(x_sample, (1, 0, 2)).reshape(dec_seq * dec_batch, d)
    q, k, v, ob, vs, sga, sgb = _sample_in(xs, p)
    ck = cache_k_win[0].reshape(dec_batch, WINDOW, KV_WIDTH)
    cv = cache_v_win[0].reshape(dec_batch, WINDOW, KV_WIDTH)
    oa, nk, nv = _sample_attn(q, k, v, ck, cv, p)
    y1s = _sample_merge(xs, oa, ob, sga, sgb, p)
    ys = _moe(y1s, p)
    ys = jnp.transpose(ys.reshape(dec_seq, dec_batch, d), (1, 0, 2))
    vs_out = jnp.transpose(vs.reshape(dec_seq, dec_batch, SGU_GROUPS, SGU_GROUP_DIM), (1, 0, 2, 3))

    kv_shape = (WINDOW, N_KV_HEADS, HEAD_DIM)
    return (yp, ys,
            kwin.reshape(1, batch, *kv_shape), vwin.reshape(1, batch, *kv_shape),
            nk.reshape(1, dec_batch, *kv_shape), nv.reshape(1, dec_batch, *kv_shape),
            vs_out[None])
```

```python
import functools

import numpy as np
import jax
import jax.numpy as jnp
from jax import lax
from jax.experimental import pallas as pl
from jax.experimental.pallas import tpu as pltpu

F32 = jnp.float32
BF16 = jnp.bfloat16

D_MODEL = 1024
N_HEADS = 8
N_KV_HEADS = 2
HEAD_DIM = 64
WINDOW = 128
ATTN_WIDTH = N_HEADS * HEAD_DIM
KV_WIDTH = N_KV_HEADS * HEAD_DIM
SGU_GROUPS = 4
SGU_WIDTH = D_MODEL // 2
SGU_GROUP_DIM = SGU_WIDTH // SGU_GROUPS
CHUNK = 128
N_GROUPS = 4
GROUP_BITS = 2
EXPERTS_PER_GROUP = 4
N_EXPERTS = N_GROUPS * EXPERTS_PER_GROUP
D_EXPERT = 256
GROUP_HIDDEN = EXPERTS_PER_GROUP * D_EXPERT
D_IN = ATTN_WIDTH + 2 * KV_WIDTH + 2 * SGU_WIDTH + 2 * D_MODEL
EPS = 1e-6
NEG_INF = -1e30

C_Q = 0
C_K = C_Q + ATTN_WIDTH
C_V = C_K + KV_WIDTH
C_U = C_V + KV_WIDTH
C_VS = C_U + SGU_WIDTH
C_GA = C_VS + SGU_WIDTH
C_GB = C_GA + D_MODEL

LANES = 128
ROUTER_LANES = LANES
EXPERT_LANE0 = N_GROUPS
PROMPT_TILE = 512
MOE_TILE = 512
MOE_STEP_ROWS = 2 * MOE_TILE
MOE_CHUNK = 64
MOE_MAX_CHUNKS = (MOE_TILE + N_GROUPS * (MOE_CHUNK - 1)) // MOE_CHUNK
MOE_SLOTS = -(-MOE_MAX_CHUNKS * MOE_CHUNK // LANES) * LANES
SLOT_SPLIT = 128
SAMPLE_BATCH_TILE = 8
VMEM_LIMIT = 56 * 1024 * 1024
MOE_VMEM_LIMIT = 62 * 1024 * 1024

_SQRT_2_OVER_PI = np.sqrt(2.0 / np.pi).astype(np.float32)


def _dot(a, b):
    return jnp.dot(a, b, preferred_element_type=F32)


def _dot_nt(a, b):
    return lax.dot_general(a, b, (((1,), (1,)), ((), ())), preferred_element_type=F32)


def _gelu(x):
    cdf = 0.5 * (1.0 + jnp.tanh(_SQRT_2_OVER_PI * (x + 0.044715 * (x * x * x))))
    return x * cdf


def _rms(x, g):
    return x * lax.rsqrt(jnp.mean(x * x, axis=-1, keepdims=True) + EPS) * g


def _head_rms(x, blockdiag, g):
    ms = _dot((x * x).astype(BF16), blockdiag)
    return x * lax.rsqrt(ms + EPS) * g


def _qkv(x, g1, win_ref, gq, gk, bq, bk):
    h = _rms(x, g1).astype(BF16)
    qn = _head_rms(_dot(h, win_ref[:, C_Q:C_K]), bq, gq)
    kv = _dot(h, win_ref[:, C_K:C_U])
    kn = _head_rms(kv[:, :KV_WIDTH], bk, gk)
    v = kv[:, KV_WIDTH:]
    return h, qn, kn, v


def _sgu_inputs(h, win_ref, lng, lnb):
    gu = _gelu(_dot(h, win_ref[:, C_U:C_VS]))
    gv = _gelu(_dot(h, win_ref[:, C_VS:C_GA]))
    mu = jnp.mean(gv, axis=-1, keepdims=True)
    xc = gv - mu
    vsn = xc * lax.rsqrt(jnp.mean(xc * xc, axis=-1, keepdims=True) + EPS) * lng + lnb
    return gu, vsn


def _head_variants(prev, prev_rot, cur, cur_rot):
    a = jnp.concatenate([prev, cur], axis=0)
    r = jnp.concatenate([prev_rot, cur_rot], axis=0)
    lo = lax.broadcasted_iota(jnp.int32, a.shape, 1) < HEAD_DIM
    zero = jnp.zeros_like(a)
    kv0 = jnp.concatenate([jnp.where(lo, a, zero), jnp.where(lo, zero, r)], axis=0)
    kv1 = jnp.concatenate([jnp.where(lo, r, zero), jnp.where(lo, zero, a)], axis=0)
    return kv0, kv1


def _lane_slabs(x):
    return [x[:, p * LANES:(p + 1) * LANES] for p in range(x.shape[1] // LANES)]


def _scores(q_slabs, kcat, bias):
    slabs_per_kv = len(q_slabs) // N_KV_HEADS
    m = q_slabs[0].shape[0]
    out = []
    for kv in range(N_KV_HEADS):
        q = jnp.concatenate(q_slabs[kv * slabs_per_kv:(kv + 1) * slabs_per_kv], axis=0)
        s = _dot_nt(q.astype(BF16), kcat[kv])
        out += [s[i * m:(i + 1) * m] + bias for i in range(slabs_per_kv)]
    return out


def _sink_softmax(scores, sinks_ref):
    out = []
    for p, s in enumerate(scores):
        n_keys = s.shape[1] // 2
        probs = []
        for par in range(2):
            sh = s[:, par * n_keys:(par + 1) * n_keys]
            sink = sinks_ref[2 * p + par]
            m = jnp.maximum(jnp.max(sh, axis=-1, keepdims=True), sink)
            e = jnp.exp(sh - m)
            den = jnp.sum(e, axis=-1, keepdims=True) + jnp.exp(sink - m)
            probs.append((e / den).astype(BF16))
        out.append(jnp.concatenate(probs, axis=1))
    return out


def _weighted_values(probs, vcat):
    slabs_per_kv = len(probs) // N_KV_HEADS
    m = probs[0].shape[0]
    out = []
    for kv in range(N_KV_HEADS):
        pr = jnp.concatenate(probs[kv * slabs_per_kv:(kv + 1) * slabs_per_kv], axis=0)
        o = _dot(pr, vcat[kv])
        out += [o[i * m:(i + 1) * m] for i in range(slabs_per_kv)]
    return out


def _gates(h, win_ref):
    return (jax.nn.sigmoid(_dot(h, win_ref[:, C_GA:C_GB])),
            jax.nn.sigmoid(_dot(h, win_ref[:, C_GB:D_IN])))


def _merge(x, sga, sgb, oa, ob, wa_ref, wb_ref, wout_ref):
    ya = _dot(oa.astype(BF16), wa_ref[...])
    yb = _dot(ob.astype(BF16), wb_ref[...])
    hm = sga * ya + sgb * yb
    return x + _dot(hm.astype(BF16), wout_ref[...])


def _prompt_kernel(x_ref, g1_ref, win_ref, gq_ref, gk_ref, bq_ref, bk_ref, sinks_ref, lng_ref,
                   lnb_ref, wsp_ref, bsp_ref, wa_ref, wb_ref, wout_ref, bias_ref,
                   y_ref, kwin_ref, vwin_ref,
                   kprev, kprev_rot, vprev, vprev_rot):
    j = pl.program_id(1)

    @pl.when(j == 0)
    def _():
        for r in (kprev, kprev_rot, vprev, vprev_rot):
            r[...] = jnp.zeros_like(r)

    x = x_ref[0]
    h, qn, kn, v = _qkv(x, g1_ref[...], win_ref, gq_ref[...], gk_ref[...], bq_ref[...], bk_ref[...])

    kb = kn.astype(BF16)
    kr = pltpu.roll(kn, HEAD_DIM, 1).astype(BF16)
    vb = v.astype(BF16)
    vr = pltpu.roll(v, HEAD_DIM, 1).astype(BF16)

    n_blocks = x.shape[0] // WINDOW
    blocks = [slice(i * WINDOW, (i + 1) * WINDOW) for i in range(n_blocks)]
    scores, vcats = [], []
    pk, pkr, pv, pvr = kprev[...], kprev_rot[...], vprev[...], vprev_rot[...]
    for i, rows in enumerate(blocks):
        ck, ckr, cv, cvr = kb[rows], kr[rows], vb[rows], vr[rows]
        kcat = _head_variants(pk, pkr, ck, ckr)
        vcats.append(_head_variants(pv, pvr, cv, cvr))
        bias = bias_ref[jnp.where(j == 0, 1, 0)] if i == 0 else bias_ref[0]
        scores.append(_scores(_lane_slabs(qn[rows]), kcat, bias))
        pk, pkr, pv, pvr = ck, ckr, cv, cvr
    kprev[...] = pk
    kprev_rot[...] = pkr
    vprev[...] = pv
    vprev_rot[...] = pvr

    gu, vsn = _sgu_inputs(h, win_ref, lng_ref[...], lnb_ref[...])
    vsb = vsn.astype(BF16)
    ob = jnp.concatenate([
        gu[rows] * (jnp.concatenate(
            [_dot(wsp_ref[g], vsb[rows, g * SGU_GROUP_DIM:(g + 1) * SGU_GROUP_DIM])
             for g in range(SGU_GROUPS)], axis=1) + bsp_ref[...])
        for rows in blocks], axis=0)

    sga, sgb = _gates(h, win_ref)

    probs = [_sink_softmax(s, sinks_ref) for s in scores]
    oa = jnp.concatenate(
        [jnp.concatenate(_weighted_values(pr, vc), axis=1) for pr, vc in zip(probs, vcats)], axis=0)
    y_ref[0] = _merge(x, sga, sgb, oa, ob, wa_ref, wb_ref, wout_ref)

    @pl.when(j == pl.num_programs(1) - 1)
    def _():
        kwin_ref[0] = kn[(n_blocks - 1) * WINDOW:]
        vwin_ref[0] = v[(n_blocks - 1) * WINDOW:]


def _const_spec(shape, single_buffer=True):
    nd = len(shape)
    mode = pl.Buffered(1) if single_buffer else None
    return pl.BlockSpec(shape, lambda *_: (0,) * nd, pipeline_mode=mode)


def _smem_spec():
    return pl.BlockSpec(memory_space=pltpu.SMEM)


def _prompt_mixer(x, p):
    b, s, d = x.shape
    ts = PROMPT_TILE
    assert s % ts == 0 and ts % WINDOW == 0
    in_specs = [
        pl.BlockSpec((1, ts, d), lambda bi, j: (bi, j, 0)),
        _const_spec((1, d)), _const_spec((d, D_IN)),
        _const_spec((1, ATTN_WIDTH)), _const_spec((1, KV_WIDTH)),
        _const_spec((ATTN_WIDTH, ATTN_WIDTH)), _const_spec((KV_WIDTH, KV_WIDTH)),
        _smem_spec(),
        _const_spec((1, SGU_WIDTH)), _const_spec((1, SGU_WIDTH)),
        _const_spec((SGU_GROUPS, CHUNK, CHUNK)), _const_spec((CHUNK, SGU_WIDTH)),
        _const_spec((ATTN_WIDTH, d)), _const_spec((SGU_WIDTH, d)), _const_spec((d, d)),
        _const_spec((2, WINDOW, 4 * WINDOW)),
    ]
    out_specs = [
        pl.BlockSpec((1, ts, d), lambda bi, j: (bi, j, 0)),
        pl.BlockSpec((1, WINDOW, KV_WIDTH), lambda bi, j: (bi, 0, 0)),
        pl.BlockSpec((1, WINDOW, KV_WIDTH), lambda bi, j: (bi, 0, 0)),
    ]
    out_shape = [
        jax.ShapeDtypeStruct((b, s, d), F32),
        jax.ShapeDtypeStruct((b, WINDOW, KV_WIDTH), F32),
        jax.ShapeDtypeStruct((b, WINDOW, KV_WIDTH), F32),
    ]
    return pl.pallas_call(
        _prompt_kernel,
        grid=(b, s // ts),
        in_specs=in_specs, out_specs=out_specs, out_shape=out_shape,
        scratch_shapes=[pltpu.VMEM((WINDOW, KV_WIDTH), BF16)] * 4,
        compiler_params=pltpu.CompilerParams(
            dimension_semantics=("arbitrary", "arbitrary"), vmem_limit_bytes=VMEM_LIMIT),
        name="prompt_mixer",
    )(x, p["g1"], p["w_in"], p["gq"], p["gk"], p["bq"], p["bk"], p["sinks"], p["lng"], p["lnb"],
      p["wsp"], p["bsp"], p["wa"], p["wb"], p["wout"], p["bias_prompt"])


def _sample_in_kernel(x_ref, g1_ref, win_ref, gq_ref, gk_ref, bq_ref, bk_ref, lng_ref, lnb_ref,
                      coef_ref, sbias_ref,
                      q_ref, k_ref, v_ref, ob_ref, vs_ref, sga_ref, sgb_ref):
    x = x_ref[...]
    h, qn, kn, v = _qkv(x, g1_ref[...], win_ref, gq_ref[...], gk_ref[...], bq_ref[...], bk_ref[...])
    gu, vsn = _sgu_inputs(h, win_ref, lng_ref[...], lnb_ref[...])
    for p, slab in enumerate(_lane_slabs(qn)):
        q_ref[p] = slab
    k_ref[...] = kn
    v_ref[...] = v
    vs_ref[...] = vsn
    sga_ref[...], sgb_ref[...] = _gates(h, win_ref)
    nb = x.shape[0] // 4
    for t in range(4):
        mixed = sbias_ref[t:t + 1, :]
        for jj in range(t + 1):
            mixed = mixed + coef_ref[4 * t + jj:4 * t + jj + 1, :] * vsn[jj * nb:(jj + 1) * nb]
        ob_ref[t * nb:(t + 1) * nb, :] = gu[t * nb:(t + 1) * nb] * mixed


def _sample_in(x, p):
    n, d = x.shape
    shapes = [(ATTN_WIDTH // LANES, n, LANES), (n, KV_WIDTH), (n, KV_WIDTH), (n, SGU_WIDTH),
              (n, SGU_WIDTH), (n, d), (n, d)]
    return pl.pallas_call(
        _sample_in_kernel,
        grid=(1,),
        in_specs=[_const_spec((n, d)), _const_spec((1, d)), _const_spec((d, D_IN)),
                  _const_spec((1, ATTN_WIDTH)), _const_spec((1, KV_WIDTH)),
                  _const_spec((ATTN_WIDTH, ATTN_WIDTH)), _const_spec((KV_WIDTH, KV_WIDTH)),
                  _const_spec((1, SGU_WIDTH)), _const_spec((1, SGU_WIDTH)),
                  _const_spec((16, SGU_WIDTH)), _const_spec((4, SGU_WIDTH))],
        out_specs=[_const_spec(sh, False) for sh in shapes],
        out_shape=[jax.ShapeDtypeStruct(sh, F32) for sh in shapes],
        compiler_params=pltpu.CompilerParams(
            dimension_semantics=("arbitrary",), vmem_limit_bytes=VMEM_LIMIT),
        name="sample_in",
    )(x, p["g1"], p["w_in"], p["gq"], p["gk"], p["bq"], p["bk"], p["lng"], p["lnb"],
      p["coef_s"], p["bias_sgu_s"])


def _sample_attn_kernel(q_ref, k_ref, v_ref, ck_ref, cv_ref, sinks_ref, bias_ref,
                        oa_ref, nk_ref, nv_ref):
    i = pl.program_id(0)
    n_slabs = q_ref.shape[0]
    half_rows = q_ref.shape[1] // 8
    pad = jnp.zeros((WINDOW - 8, KV_WIDTH), BF16)
    row_half = lax.broadcasted_iota(jnp.int32, (8, LANES), 0) % 2

    work = []
    for bb in range(SAMPLE_BATCH_TILE):
        b_lo = i * SAMPLE_BATCH_TILE + bb
        rows = pl.ds(b_lo, 8, stride=half_rows)
        q8 = [q_ref[p, rows, :] for p in range(n_slabs)]
        k8 = k_ref[rows, :]
        v8 = v_ref[rows, :]
        ck_new = jnp.concatenate([k8.astype(BF16), pad], axis=0)
        ckr_new = jnp.concatenate([pltpu.roll(k8, HEAD_DIM, 1).astype(BF16), pad], axis=0)
        cv_new = jnp.concatenate([v8.astype(BF16), pad], axis=0)
        cvr_new = jnp.concatenate([pltpu.roll(v8, HEAD_DIM, 1).astype(BF16), pad], axis=0)
        for hf in range(2):
            kc = ck_ref[hf, bb]
            vc = cv_ref[hf, bb]
            kcat = _head_variants(kc.astype(BF16), pltpu.roll(kc, HEAD_DIM, 1).astype(BF16),
                                  ck_new, ckr_new)
            vcat = _head_variants(vc.astype(BF16), pltpu.roll(vc, HEAD_DIM, 1).astype(BF16),
                                  cv_new, cvr_new)
            work.append((rows, hf, _scores(q8, kcat, bias_ref[hf]), vcat))
            nk_ref[hf, bb] = pltpu.roll(kc, WINDOW - 4, 0)
            nv_ref[hf, bb] = pltpu.roll(vc, WINDOW - 4, 0)
            for t in range(4):
                nk_ref[hf, bb, WINDOW - 4 + t:WINDOW - 3 + t, :] = k8[2 * t + hf:2 * t + hf + 1, :]
                nv_ref[hf, bb, WINDOW - 4 + t:WINDOW - 3 + t, :] = v8[2 * t + hf:2 * t + hf + 1, :]
    probs = [_sink_softmax(s, sinks_ref) for _, _, s, _ in work]
    outs = [_weighted_values(pr, w[3]) for pr, w in zip(probs, work)]
    for (rows, _, _, _), o_even, o_odd in zip(work[0::2], outs[0::2], outs[1::2]):
        for p in range(n_slabs):
            oa_ref[p, rows, :] = jnp.where(row_half == 0, o_even[p], o_odd[p])


def _sample_attn(q, k, v, ck, cv, p):
    n_slabs, n, _ = q.shape
    nb = n // 4
    half = nb // 2
    bt = SAMPLE_BATCH_TILE
    assert half % bt == 0
    ck4 = ck.reshape(2, half, WINDOW, KV_WIDTH)
    cv4 = cv.reshape(2, half, WINDOW, KV_WIDTH)
    cache_spec = pl.BlockSpec((2, bt, WINDOW, KV_WIDTH), lambda i: (0, i, 0, 0))
    return pl.pallas_call(
        _sample_attn_kernel,
        grid=(half // bt,),
        in_specs=[_const_spec((n_slabs, n, LANES)), _const_spec((n, KV_WIDTH)),
                  _const_spec((n, KV_WIDTH)), cache_spec, cache_spec, _smem_spec(),
                  _const_spec((2, 8, 4 * WINDOW))],
        out_specs=[_const_spec((n_slabs, n, LANES), False), cache_spec, cache_spec],
        out_shape=[jax.ShapeDtypeStruct((n_slabs, n, LANES), F32),
                   jax.ShapeDtypeStruct(ck4.shape, F32), jax.ShapeDtypeStruct(cv4.shape, F32)],
        compiler_params=pltpu.CompilerParams(
            dimension_semantics=("arbitrary",), vmem_limit_bytes=VMEM_LIMIT),
        name="sample_attn",
    )(q, k, v, ck4, cv4, p["sinks"], p["bias_sample"])


def _sample_merge_kernel(x_ref, oa_ref, ob_ref, sga_ref, sgb_ref, wa_ref, wb_ref, wout_ref, y_ref):
    oa = jnp.concatenate([oa_ref[p] for p in range(oa_ref.shape[0])], axis=1)
    y_ref[...] = _merge(x_ref[...], sga_ref[...], sgb_ref[...], oa, ob_ref[...],
                        wa_ref, wb_ref, wout_ref)


def _sample_merge(x, oa, ob, sga, sgb, p):
    n, d = x.shape
    return pl.pallas_call(
        _sample_merge_kernel,
        grid=(1,),
        in_specs=[_const_spec((n, d)), _const_spec(oa.shape), _const_spec((n, SGU_WIDTH)),
                  _const_spec((n, d)), _const_spec((n, d)),
                  _const_spec((ATTN_WIDTH, d)), _const_spec((SGU_WIDTH, d)), _const_spec((d, d))],
        out_specs=_const_spec((n, d), False),
        out_shape=jax.ShapeDtypeStruct((n, d), F32),
        compiler_params=pltpu.CompilerParams(
            dimension_semantics=("arbitrary",), vmem_limit_bytes=VMEM_LIMIT),
        name="sample_merge",
    )(x, oa, ob, sga, sgb, p["wa"], p["wb"], p["wout"])


def _route(logits):
    lane = lax.broadcasted_iota(jnp.int32, logits.shape, 1).astype(F32)
    far = float(ROUTER_LANES)
    glm = jnp.where(lane < N_GROUPS, logits, NEG_INF)
    gmax = jnp.max(glm, axis=-1, keepdims=True)
    gidx = jnp.min(jnp.where(glm == gmax, lane, far), axis=-1, keepdims=True)
    gw = 1.0 / jnp.sum(jnp.exp(glm - gmax), axis=-1, keepdims=True)
    first = EXPERT_LANE0 + EXPERTS_PER_GROUP * gidx
    sel = (lane >= first) & (lane < first + EXPERTS_PER_GROUP)
    el = jnp.where(sel, logits, NEG_INF)
    t1 = jnp.max(el, axis=-1, keepdims=True)
    i1 = jnp.min(jnp.where(el == t1, lane, far), axis=-1, keepdims=True)
    el2 = jnp.where(lane == i1, NEG_INF, el)
    t2 = jnp.max(el2, axis=-1, keepdims=True)
    i2 = jnp.min(jnp.where(el2 == t2, lane, far), axis=-1, keepdims=True)
    e2 = jnp.exp(t2 - t1)
    den = 1.0 + e2
    w1 = (1.0 / den) * gw
    w2 = (e2 / den) * gw
    return gidx, jnp.where(lane == i1, w1, 0.0) + jnp.where(lane == i2, w2, 0.0)


def _split_bf16(x):
    hi = x.astype(BF16)
    lo = (x - hi.astype(F32)).astype(BF16)
    return hi, lo


def _moe_kernel(y_ref, g2_ref, wrc_ref, wrh_ref, br_ref, wg_ref, wu_ref, wd_ref, ex_ref,
                ltri_ref, eye_ref, o_ref, hs_ref, cws_ref, os_ref):
    n_tiles = y_ref.shape[0] // MOE_TILE
    tiles = [slice(t * MOE_TILE, (t + 1) * MOE_TILE) for t in range(n_tiles)]

    @pl.when(pl.program_id(0) == 0)
    def _():
        os_ref[...] = jnp.zeros_like(os_ref)

    h_split = [_split_bf16(_rms(y_ref[r, :], g2_ref[...])) for r in tiles]
    part = [_dot(hb, wrc_ref[...]) for hb, _ in h_split]
    low = [_dot(hl, wrh_ref[...]) for _, hl in h_split]
    routed = [_route(pt[:, :ROUTER_LANES] + (pt[:, ROUTER_LANES:] + lw) + br_ref[...])
              for pt, lw in zip(part, low)]

    lane = lax.broadcasted_iota(jnp.int32, (MOE_TILE, ROUTER_LANES), 1).astype(F32)
    lane1 = lane[0:1]
    onehot = [jnp.where(lane == gidx, 1.0, 0.0) for gidx, _ in routed]
    before = [_dot(ltri_ref[...], oh.astype(BF16)) for oh in onehot]
    ends, pos = [], []
    for oh, bf in zip(onehot, before):
        count = jnp.sum(oh, axis=0, keepdims=True)
        n_chunks = jnp.floor((count + (MOE_CHUNK - 1)) * (1.0 / MOE_CHUNK))
        tile_ends = []
        start_vec = jnp.zeros_like(n_chunks)
        end = jnp.zeros((1, 1), F32)
        for g in range(N_GROUPS):
            start_vec = start_vec + jnp.where(lane1 == g, end * MOE_CHUNK, 0.0)
            end = end + jnp.sum(jnp.where(lane1 == g, n_chunks, 0.0), axis=-1, keepdims=True)
            tile_ends.append(end[0, 0].astype(jnp.int32))
        ends.append(tile_ends)
        pos.append(jnp.sum(oh * (bf + start_vec), axis=-1, keepdims=True))

    pos_t = []
    for ps in pos:
        pos_hi = jnp.floor(ps * (1.0 / SLOT_SPLIT))
        pos_lo = ps - pos_hi * SLOT_SPLIT
        cols = jnp.where(lane == 0.0, pos_hi, jnp.where(lane == 1.0, pos_lo, 0.0)).astype(BF16)
        pos_t.append(_dot_nt(eye_ref[...], cols))
    slot_s = lax.broadcasted_iota(jnp.int32, (MOE_SLOTS, MOE_TILE), 0).astype(F32)
    for t, (pt, (hb, _), (_, cw)) in enumerate(zip(pos_t, h_split, routed)):
        pos_row = pt[0:1] * SLOT_SPLIT + pt[1:2]
        sort = jnp.where(slot_s == pos_row, 1.0, 0.0).astype(BF16)
        ch, cl = _split_bf16(cw)
        sorted_rows = _dot(sort, jnp.concatenate([hb, ch, cl], axis=1))
        hs_ref[t] = sorted_rows[:, :D_MODEL].astype(BF16)
        cws_ref[t] = (sorted_rows[:, D_MODEL:D_MODEL + ROUTER_LANES]
                      + sorted_rows[:, D_MODEL + ROUTER_LANES:])

    def run_group(t, g, first_chunk, n_rows):
        rows = pl.ds(pl.multiple_of(first_chunk * MOE_CHUNK, MOE_CHUNK), n_rows)
        hsc = hs_ref[t, rows, :]
        wh, wl = _split_bf16(cws_ref[t, rows, :])
        cexp = _dot(wh, ex_ref[...]) + _dot(wl, ex_ref[...])
        experts = [EXPERTS_PER_GROUP * g + e for e in range(EXPERTS_PER_GROUP)]
        a = jnp.concatenate([_dot(hsc, wg_ref[e]) for e in experts], axis=1)
        u = jnp.concatenate([_dot(hsc, wu_ref[e]) for e in experts], axis=1)
        hid = (a * jax.nn.sigmoid(a)) * u * cexp
        os_ref[t, rows, :] = _dot(hid.astype(BF16), wd_ref[g]).astype(BF16)

    def group(i, carry):
        t = lax.shift_right_logical(i, GROUP_BITS)
        g = i & (N_GROUPS - 1)
        first, last = jnp.int32(0), jnp.int32(0)
        for tt in range(n_tiles):
            for gg in range(N_GROUPS):
                here = (t == tt) & (g == gg)
                first = jnp.where(here, ends[tt][gg - 1] if gg else 0, first)
                last = jnp.where(here, ends[tt][gg], last)
        n = last - first
        for k in range(1, MOE_TILE // MOE_CHUNK + 1):
            @pl.when(n == k)
            def _(k=k):
                run_group(t, g, first, k * MOE_CHUNK)
        return carry

    lax.fori_loop(0, n_tiles * N_GROUPS, group, 0)

    slot_l = lax.broadcasted_iota(jnp.int32, (MOE_TILE, MOE_SLOTS), 1).astype(F32)
    for t, (r, ps) in enumerate(zip(tiles, pos)):
        unsort = jnp.where(slot_l == ps, 1.0, 0.0).astype(BF16)
        o_ref[r, :] = y_ref[r, :] + _dot(unsort, os_ref[t])


def _moe(y, p):
    n, d = y.shape
    tm = min(MOE_STEP_ROWS, n)
    assert n % tm == 0 and tm % MOE_TILE == 0
    n_tiles = tm // MOE_TILE
    row_spec = pl.BlockSpec((tm, d), lambda i: (i, 0))
    return pl.pallas_call(
        _moe_kernel,
        grid=(n // tm,),
        in_specs=[row_spec, _const_spec((1, d)), _const_spec((d, 2 * ROUTER_LANES)),
                  _const_spec((d, ROUTER_LANES)), _const_spec((1, ROUTER_LANES)),
                  _const_spec((N_EXPERTS, d, D_EXPERT)), _const_spec((N_EXPERTS, d, D_EXPERT)),
                  _const_spec((N_GROUPS, GROUP_HIDDEN, d)),
                  _const_spec((ROUTER_LANES, GROUP_HIDDEN)),
                  _const_spec((MOE_TILE, MOE_TILE)), _const_spec((LANES, LANES))],
        out_specs=row_spec,
        out_shape=jax.ShapeDtypeStruct((n, d), F32),
        scratch_shapes=[pltpu.VMEM((n_tiles, MOE_SLOTS, d), BF16),
                        pltpu.VMEM((n_tiles, MOE_SLOTS, ROUTER_LANES), F32),
                        pltpu.VMEM((n_tiles, MOE_SLOTS, d), BF16)],
        compiler_params=pltpu.CompilerParams(
            dimension_semantics=("arbitrary",), vmem_limit_bytes=MOE_VMEM_LIMIT),
        name="expert_mixer",
    )(y, p["g2"], p["wr_cat"], p["wr_hi"], p["br"], p["wg"], p["wu"], p["wd"], p["expand"],
      p["ltri"], p["eye"])


def _window_bias(n_q, q_tok, key_prev_ok, key_cur_ok):
    half = np.concatenate([key_prev_ok, key_cur_ok], axis=1)
    ok = np.concatenate([half, half], axis=1)
    return np.where(ok, 0.0, NEG_INF).astype(np.float32)


def _prompt_bias():
    t = np.arange(WINDOW)[:, None]
    s = np.arange(WINDOW)[None, :]
    prev_ok = s > t
    cur_ok = s <= t
    normal = _window_bias(WINDOW, t, prev_ok, cur_ok)
    first = _window_bias(WINDOW, t, np.zeros_like(prev_ok), cur_ok)
    return np.stack([normal, first])


def _sample_bias():
    out = []
    r = np.arange(8)[:, None]
    t = r // 2
    s = np.arange(WINDOW)[None, :]
    prev_ok = s > t
    for hf in range(2):
        c = s
        cur_ok = (c < 8) & (c % 2 == hf) & (c // 2 <= t)
        out.append(_window_bias(8, t, np.broadcast_to(prev_ok, (8, WINDOW)), cur_ok))
    return np.stack(out)


def _block_diag_mean(width):
    idx = np.arange(width) // HEAD_DIM
    return (idx[:, None] == idx[None, :]).astype(np.float32) / HEAD_DIM


def _expand_matrix():
    ex = np.zeros((ROUTER_LANES, GROUP_HIDDEN), np.float32)
    for g in range(N_GROUPS):
        for e in range(EXPERTS_PER_GROUP):
            ex[EXPERT_LANE0 + EXPERTS_PER_GROUP * g + e, e * D_EXPERT:(e + 1) * D_EXPERT] = 1.0
    return ex


def _prepare(norm1_g, w_in, q_norm_g, k_norm_g, attn_sinks, ln_v_g, ln_v_b, w_spatial, b_spatial,
             w_branch_a, w_branch_b, w_out, norm2_g, w_router_group, b_router_group,
             w_router_expert, b_router_expert, w_exp_gate, w_exp_up, w_exp_down):
    p = {}
    p["g1"] = norm1_g.reshape(1, D_MODEL)
    p["w_in"] = w_in.astype(BF16)
    p["gq"] = (jnp.tile(q_norm_g, N_HEADS) * (HEAD_DIM ** -0.5)).reshape(1, ATTN_WIDTH)
    p["gk"] = jnp.tile(k_norm_g, N_KV_HEADS).reshape(1, KV_WIDTH)
    p["bq"] = jnp.asarray(_block_diag_mean(ATTN_WIDTH), BF16)
    p["bk"] = jnp.asarray(_block_diag_mean(KV_WIDTH), BF16)
    p["sinks"] = attn_sinks.astype(F32)
    p["lng"] = ln_v_g.reshape(1, SGU_WIDTH)
    p["lnb"] = ln_v_b.reshape(1, SGU_WIDTH)
    tril = jnp.tril(jnp.ones((CHUNK, CHUNK), F32))
    wsp = w_spatial * tril[None]
    p["wsp"] = wsp.astype(BF16)
    p["bsp"] = jnp.repeat(b_spatial.T, SGU_GROUP_DIM, axis=1)
    w4 = wsp[:, :4, :4]
    p["coef_s"] = jnp.repeat(jnp.transpose(w4, (1, 2, 0)).reshape(16, SGU_GROUPS),
                             SGU_GROUP_DIM, axis=1)
    p["bias_sgu_s"] = jnp.repeat(b_spatial[:, :4].T, SGU_GROUP_DIM, axis=1)
    p["wa"] = w_branch_a.astype(BF16)
    p["wb"] = w_branch_b.astype(BF16)
    p["wout"] = w_out.astype(BF16)
    p["g2"] = norm2_g.reshape(1, D_MODEL)
    wr = jnp.concatenate(
        [w_router_group, w_router_expert,
         jnp.zeros((D_MODEL, ROUTER_LANES - N_GROUPS - N_EXPERTS), F32)], axis=1)
    p["wr_hi"], wr_lo = _split_bf16(wr)
    p["wr_cat"] = jnp.concatenate([p["wr_hi"], wr_lo], axis=1)
    p["br"] = jnp.concatenate(
        [b_router_group, b_router_expert,
         jnp.zeros((ROUTER_LANES - N_GROUPS - N_EXPERTS,), F32)]).reshape(1, ROUTER_LANES)
    p["wg"] = w_exp_gate.astype(BF16)
    p["wu"] = w_exp_up.astype(BF16)
    p["wd"] = w_exp_down.reshape(N_GROUPS, GROUP_HIDDEN, D_MODEL).astype(BF16)
    p["expand"] = jnp.asarray(_expand_matrix(), BF16)
    p["ltri"] = jnp.asarray(np.tril(np.ones((MOE_TILE, MOE_TILE), np.float32), -1), BF16)
    p["eye"] = jnp.asarray(np.eye(LANES, dtype=np.float32), BF16)
    p["bias_prompt"] = jnp.asarray(_prompt_bias())
    p["bias_sample"] = jnp.asarray(_sample_bias())
    return p


def kernel(x_prompt, x_sample, cache_k_win, cache_v_win, norm1_g, w_in, q_norm_g, k_norm_g, attn_sinks, ln_v_g, ln_v_b, w_spatial, b_spatial, w_branch_a, w_branch_b, w_out, norm2_g, w_router_group, b_router_group, w_router_expert, b_router_expert, w_exp_gate, w_exp_up, w_exp_down):
    depth = norm1_g.shape[0]
    assert depth == 1
    batch, seq, d = x_prompt.shape
    dec_batch, dec_seq, _ = x_sample.shape
    assert dec_seq == 4 and d == D_MODEL
    p = _prepare(*(a[0] for a in (
        norm1_g, w_in, q_norm_g, k_norm_g, attn_sinks, ln_v_g, ln_v_b, w_spatial, b_spatial,
        w_branch_a, w_branch_b, w_out, norm2_g, w_router_group, b_router_group,
        w_router_expert, b_router_expert, w_exp_gate, w_exp_up, w_exp_down)))

    y1p, kwin, vwin = _prompt_mixer(x_prompt, p)
    yp = _moe(y1p.reshape(batch * seq, d), p).reshape(batch, seq, d)

    xs = jnp.transpose(x_sample, (1, 0, 2)).reshape(dec_seq * dec_batch, d)
    q, k, v, ob, vs, sga, sgb = _sample_in(xs, p)
    ck = cache_k_win[0].reshape(dec_batch, WINDOW, KV_WIDTH)
    cv = cache_v_win[0].reshape(dec_batch, WINDOW, KV_WIDTH)
    oa, nk, nv = _sample_attn(q, k, v, ck, cv, p)
    y1s = _sample_merge(xs, oa, ob, sga, sgb, p)
    ys = _moe(y1s, p)
    ys = jnp.transpose(ys.reshape(dec_seq, dec_batch, d), (1, 0, 2))
    vs_out = jnp.transpose(vs.reshape(dec_seq, dec_batch, SGU_GROUPS, SGU_GROUP_DIM), (1, 0, 2, 3))

    kv_shape = (WINDOW, N_KV_HEADS, HEAD_DIM)
    return (yp, ys,
            kwin.reshape(1, batch, *kv_shape), vwin.reshape(1, batch, *kv_shape),
            nk.reshape(1, dec_batch, *kv_shape), nv.reshape(1, dec_batch, *kv_shape),
            vs_out[None])
```

```python
import functools

import numpy as np
import jax
import jax.numpy as jnp
from jax import lax
from jax.experimental import pallas as pl
from jax.experimental.pallas import tpu as pltpu

F32 = jnp.float32
BF16 = jnp.bfloat16

D_MODEL = 1024
N_HEADS = 8
N_KV_HEADS = 2
HEAD_DIM = 64
WINDOW = 128
ATTN_WIDTH = N_HEADS * HEAD_DIM
KV_WIDTH = N_KV_HEADS * HEAD_DIM
SGU_GROUPS = 4
SGU_WIDTH = D_MODEL // 2
SGU_GROUP_DIM = SGU_WIDTH // SGU_GROUPS
CHUNK = 128
N_GROUPS = 4
GROUP_BITS = 2
EXPERTS_PER_GROUP = 4
N_EXPERTS = N_GROUPS * EXPERTS_PER_GROUP
D_EXPERT = 256
GROUP_HIDDEN = EXPERTS_PER_GROUP * D_EXPERT
D_IN = ATTN_WIDTH + 2 * KV_WIDTH + 2 * SGU_WIDTH + 2 * D_MODEL
EPS = 1e-6
NEG_INF = -1e30

C_Q = 0
C_K = C_Q + ATTN_WIDTH
C_V = C_K + KV_WIDTH
C_U = C_V + KV_WIDTH
C_VS = C_U + SGU_WIDTH
C_GA = C_VS + SGU_WIDTH
C_GB = C_GA + D_MODEL

LANES = 128
ROUTER_LANES = LANES
EXPERT_LANE0 = N_GROUPS
PROMPT_TILE = 512
MOE_TILE = 512
MOE_STEP_ROWS = 2 * MOE_TILE
MOE_CHUNK = 32
MOE_MAX_CHUNKS = (MOE_TILE + N_GROUPS * (MOE_CHUNK - 1)) // MOE_CHUNK
MOE_SLOTS = -(-MOE_MAX_CHUNKS * MOE_CHUNK // LANES) * LANES
SLOT_SPLIT = 128
SAMPLE_BATCH_TILE = 8
VMEM_LIMIT = 56 * 1024 * 1024
MOE_VMEM_LIMIT = 62 * 1024 * 1024

_SQRT_2_OVER_PI = np.sqrt(2.0 / np.pi).astype(np.float32)


def _dot(a, b):
    return jnp.dot(a, b, preferred_element_type=F32)


def _dot_nt(a, b):
    return lax.dot_general(a, b, (((1,), (1,)), ((), ())), preferred_element_type=F32)


def _gelu(x):
    cdf = 0.5 * (1.0 + jnp.tanh(_SQRT_2_OVER_PI * (x + 0.044715 * (x * x * x))))
    return x * cdf


def _rms(x, g):
    return x * lax.rsqrt(jnp.mean(x * x, axis=-1, keepdims=True) + EPS) * g


def _head_rms(x, blockdiag, g):
    ms = _dot((x * x).astype(BF16), blockdiag)
    return x * lax.rsqrt(ms + EPS) * g


def _qkv(x, g1, win_ref, gq, gk, bq, bk):
    h = _rms(x, g1).astype(BF16)
    qn = _head_rms(_dot(h, win_ref[:, C_Q:C_K]), bq, gq)
    kv = _dot(h, win_ref[:, C_K:C_U])
    kn = _head_rms(kv[:, :KV_WIDTH], bk, gk)
    v = kv[:, KV_WIDTH:]
    return h, qn, kn, v


def _sgu_inputs(h, win_ref, lng, lnb):
    gu = _gelu(_dot(h, win_ref[:, C_U:C_VS]))
    gv = _gelu(_dot(h, win_ref[:, C_VS:C_GA]))
    mu = jnp.mean(gv, axis=-1, keepdims=True)
    xc = gv - mu
    vsn = xc * lax.rsqrt(jnp.mean(xc * xc, axis=-1, keepdims=True) + EPS) * lng + lnb
    return gu, vsn


def _head_variants(prev, prev_rot, cur, cur_rot):
    a = jnp.concatenate([prev, cur], axis=0)
    r = jnp.concatenate([prev_rot, cur_rot], axis=0)
    lo = lax.broadcasted_iota(jnp.int32, a.shape, 1) < HEAD_DIM
    zero = jnp.zeros_like(a)
    kv0 = jnp.concatenate([jnp.where(lo, a, zero), jnp.where(lo, zero, r)], axis=0)
    kv1 = jnp.concatenate([jnp.where(lo, r, zero), jnp.where(lo, zero, a)], axis=0)
    return kv0, kv1


def _lane_slabs(x):
    return [x[:, p * LANES:(p + 1) * LANES] for p in range(x.shape[1] // LANES)]


def _scores(q_slabs, kcat, bias):
    slabs_per_kv = len(q_slabs) // N_KV_HEADS
    m = q_slabs[0].shape[0]
    out = []
    for kv in range(N_KV_HEADS):
        q = jnp.concatenate(q_slabs[kv * slabs_per_kv:(kv + 1) * slabs_per_kv], axis=0)
        s = _dot_nt(q.astype(BF16), kcat[kv])
        out += [s[i * m:(i + 1) * m] + bias for i in range(slabs_per_kv)]
    return out


def _sink_softmax(scores, sinks_ref):
    out = []
    for p, s in enumerate(scores):
        n_keys = s.shape[1] // 2
        probs = []
        for par in range(2):
            sh = s[:, par * n_keys:(par + 1) * n_keys]
            sink = sinks_ref[2 * p + par]
            m = jnp.maximum(jnp.max(sh, axis=-1, keepdims=True), sink)
            e = jnp.exp(sh - m)
            den = jnp.sum(e, axis=-1, keepdims=True) + jnp.exp(sink - m)
            probs.append((e / den).astype(BF16))
        out.append(jnp.concatenate(probs, axis=1))
    return out


def _weighted_values(probs, vcat):
    slabs_per_kv = len(probs) // N_KV_HEADS
    m = probs[0].shape[0]
    out = []
    for kv in range(N_KV_HEADS):
        pr = jnp.concatenate(probs[kv * slabs_per_kv:(kv + 1) * slabs_per_kv], axis=0)
        o = _dot(pr, vcat[kv])
        out += [o[i * m:(i + 1) * m] for i in range(slabs_per_kv)]
    return out


def _gates(h, win_ref):
    return (jax.nn.sigmoid(_dot(h, win_ref[:, C_GA:C_GB])),
            jax.nn.sigmoid(_dot(h, win_ref[:, C_GB:D_IN])))


def _merge(x, sga, sgb, oa, ob, wa_ref, wb_ref, wout_ref):
    ya = _dot(oa.astype(BF16), wa_ref[...])
    yb = _dot(ob.astype(BF16), wb_ref[...])
    hm = sga * ya + sgb * yb
    return x + _dot(hm.astype(BF16), wout_ref[...])


def _prompt_kernel(x_ref, g1_ref, win_ref, gq_ref, gk_ref, bq_ref, bk_ref, sinks_ref, lng_ref,
                   lnb_ref, wsp_ref, bsp_ref, wa_ref, wb_ref, wout_ref, bias_ref,
                   y_ref, kwin_ref, vwin_ref,
                   kprev, kprev_rot, vprev, vprev_rot):
    j = pl.program_id(1)

    @pl.when(j == 0)
    def _():
        for r in (kprev, kprev_rot, vprev, vprev_rot):
            r[...] = jnp.zeros_like(r)

    x = x_ref[0]
    h, qn, kn, v = _qkv(x, g1_ref[...], win_ref, gq_ref[...], gk_ref[...], bq_ref[...], bk_ref[...])

    kb = kn.astype(BF16)
    kr = pltpu.roll(kn, HEAD_DIM, 1).astype(BF16)
    vb = v.astype(BF16)
    vr = pltpu.roll(v, HEAD_DIM, 1).astype(BF16)

    n_blocks = x.shape[0] // WINDOW
    blocks = [slice(i * WINDOW, (i + 1) * WINDOW) for i in range(n_blocks)]
    scores, vcats = [], []
    pk, pkr, pv, pvr = kprev[...], kprev_rot[...], vprev[...], vprev_rot[...]
    for i, rows in enumerate(blocks):
        ck, ckr, cv, cvr = kb[rows], kr[rows], vb[rows], vr[rows]
        kcat = _head_variants(pk, pkr, ck, ckr)
        vcats.append(_head_variants(pv, pvr, cv, cvr))
        bias = bias_ref[jnp.where(j == 0, 1, 0)] if i == 0 else bias_ref[0]
        scores.append(_scores(_lane_slabs(qn[rows]), kcat, bias))
        pk, pkr, pv, pvr = ck, ckr, cv, cvr
    kprev[...] = pk
    kprev_rot[...] = pkr
    vprev[...] = pv
    vprev_rot[...] = pvr

    gu, vsn = _sgu_inputs(h, win_ref, lng_ref[...], lnb_ref[...])
    vsb = vsn.astype(BF16)
    mixed = [_dot(wsp_ref[g], jnp.concatenate(
        [vsb[rows, g * SGU_GROUP_DIM:(g + 1) * SGU_GROUP_DIM] for rows in blocks], axis=1))
        for g in range(SGU_GROUPS)]
    ob = jnp.concatenate([
        gu[rows] * (jnp.concatenate(
            [m[:, i * SGU_GROUP_DIM:(i + 1) * SGU_GROUP_DIM] for m in mixed], axis=1) + bsp_ref[...])
        for i, rows in enumerate(blocks)], axis=0)

    sga, sgb = _gates(h, win_ref)

    probs = [_sink_softmax(s, sinks_ref) for s in scores]
    oa = jnp.concatenate(
        [jnp.concatenate(_weighted_values(pr, vc), axis=1) for pr, vc in zip(probs, vcats)], axis=0)
    y_ref[0] = _merge(x, sga, sgb, oa, ob, wa_ref, wb_ref, wout_ref)

    @pl.when(j == pl.num_programs(1) - 1)
    def _():
        kwin_ref[0] = kn[(n_blocks - 1) * WINDOW:]
        vwin_ref[0] = v[(n_blocks - 1) * WINDOW:]


def _const_spec(shape, single_buffer=True):
    nd = len(shape)
    mode = pl.Buffered(1) if single_buffer else None
    return pl.BlockSpec(shape, lambda *_: (0,) * nd, pipeline_mode=mode)


def _smem_spec():
    return pl.BlockSpec(memory_space=pltpu.SMEM)


def _prompt_mixer(x, p):
    b, s, d = x.shape
    ts = PROMPT_TILE
    assert s % ts == 0 and ts % WINDOW == 0
    in_specs = [
        pl.BlockSpec((1, ts, d), lambda bi, j: (bi, j, 0)),
        _const_spec((1, d)), _const_spec((d, D_IN)),
        _const_spec((1, ATTN_WIDTH)), _const_spec((1, KV_WIDTH)),
        _const_spec((ATTN_WIDTH, ATTN_WIDTH)), _const_spec((KV_WIDTH, KV_WIDTH)),
        _smem_spec(),
        _const_spec((1, SGU_WIDTH)), _const_spec((1, SGU_WIDTH)),
        _const_spec((SGU_GROUPS, CHUNK, CHUNK)), _const_spec((CHUNK, SGU_WIDTH)),
        _const_spec((ATTN_WIDTH, d)), _const_spec((SGU_WIDTH, d)), _const_spec((d, d)),
        _const_spec((2, WINDOW, 4 * WINDOW)),
    ]
    out_specs = [
        pl.BlockSpec((1, ts, d), lambda bi, j: (bi, j, 0)),
        pl.BlockSpec((1, WINDOW, KV_WIDTH), lambda bi, j: (bi, 0, 0)),
        pl.BlockSpec((1, WINDOW, KV_WIDTH), lambda bi, j: (bi, 0, 0)),
    ]
    out_shape = [
        jax.ShapeDtypeStruct((b, s, d), F32),
        jax.ShapeDtypeStruct((b, WINDOW, KV_WIDTH), F32),
        jax.ShapeDtypeStruct((b, WINDOW, KV_WIDTH), F32),
    ]
    return pl.pallas_call(
        _prompt_kernel,
        grid=(b, s // ts),
        in_specs=in_specs, out_specs=out_specs, out_shape=out_shape,
        scratch_shapes=[pltpu.VMEM((WINDOW, KV_WIDTH), BF16)] * 4,
        compiler_params=pltpu.CompilerParams(
            dimension_semantics=("arbitrary", "arbitrary"), vmem_limit_bytes=VMEM_LIMIT),
        name="prompt_mixer",
    )(x, p["g1"], p["w_in"], p["gq"], p["gk"], p["bq"], p["bk"], p["sinks"], p["lng"], p["lnb"],
      p["wsp"], p["bsp"], p["wa"], p["wb"], p["wout"], p["bias_prompt"])


def _sample_in_kernel(x_ref, g1_ref, win_ref, gq_ref, gk_ref, bq_ref, bk_ref, lng_ref, lnb_ref,
                      coef_ref, sbias_ref,
                      q_ref, k_ref, v_ref, ob_ref, vs_ref, sga_ref, sgb_ref):
    x = x_ref[...]
    h, qn, kn, v = _qkv(x, g1_ref[...], win_ref, gq_ref[...], gk_ref[...], bq_ref[...], bk_ref[...])
    gu, vsn = _sgu_inputs(h, win_ref, lng_ref[...], lnb_ref[...])
    for p, slab in enumerate(_lane_slabs(qn)):
        q_ref[p] = slab
    k_ref[...] = kn
    v_ref[...] = v
    vs_ref[...] = vsn
    sga_ref[...], sgb_ref[...] = _gates(h, win_ref)
    nb = x.shape[0] // 4
    for t in range(4):
        mixed = sbias_ref[t:t + 1, :]
        for jj in range(t + 1):
            mixed = mixed + coef_ref[4 * t + jj:4 * t + jj + 1, :] * vsn[jj * nb:(jj + 1) * nb]
        ob_ref[t * nb:(t + 1) * nb, :] = gu[t * nb:(t + 1) * nb] * mixed


def _sample_in(x, p):
    n, d = x.shape
    shapes = [(ATTN_WIDTH // LANES, n, LANES), (n, KV_WIDTH), (n, KV_WIDTH), (n, SGU_WIDTH),
              (n, SGU_WIDTH), (n, d), (n, d)]
    return pl.pallas_call(
        _sample_in_kernel,
        grid=(1,),
        in_specs=[_const_spec((n, d)), _const_spec((1, d)), _const_spec((d, D_IN)),
                  _const_spec((1, ATTN_WIDTH)), _const_spec((1, KV_WIDTH)),
                  _const_spec((ATTN_WIDTH, ATTN_WIDTH)), _const_spec((KV_WIDTH, KV_WIDTH)),
                  _const_spec((1, SGU_WIDTH)), _const_spec((1, SGU_WIDTH)),
                  _const_spec((16, SGU_WIDTH)), _const_spec((4, SGU_WIDTH))],
        out_specs=[_const_spec(sh, False) for sh in shapes],
        out_shape=[jax.ShapeDtypeStruct(sh, F32) for sh in shapes],
        compiler_params=pltpu.CompilerParams(
            dimension_semantics=("arbitrary",), vmem_limit_bytes=VMEM_LIMIT),
        name="sample_in",
    )(x, p["g1"], p["w_in"], p["gq"], p["gk"], p["bq"], p["bk"], p["lng"], p["lnb"],
      p["coef_s"], p["bias_sgu_s"])


def _sample_attn_kernel(q_ref, k_ref, v_ref, ck_ref, cv_ref, sinks_ref, bias_ref,
                        oa_ref, nk_ref, nv_ref):
    i = pl.program_id(0)
    n_slabs = q_ref.shape[0]
    half_rows = q_ref.shape[1] // 8
    pad = jnp.zeros((WINDOW - 8, KV_WIDTH), BF16)
    row_half = lax.broadcasted_iota(jnp.int32, (8, LANES), 0) % 2

    work = []
    for bb in range(SAMPLE_BATCH_TILE):
        b_lo = i * SAMPLE_BATCH_TILE + bb
        rows = pl.ds(b_lo, 8, stride=half_rows)
        q8 = [q_ref[p, rows, :] for p in range(n_slabs)]
        k8 = k_ref[rows, :]
        v8 = v_ref[rows, :]
        ck_new = jnp.concatenate([k8.astype(BF16), pad], axis=0)
        ckr_new = jnp.concatenate([pltpu.roll(k8, HEAD_DIM, 1).astype(BF16), pad], axis=0)
        cv_new = jnp.concatenate([v8.astype(BF16), pad], axis=0)
        cvr_new = jnp.concatenate([pltpu.roll(v8, HEAD_DIM, 1).astype(BF16), pad], axis=0)
        for hf in range(2):
            kc = ck_ref[hf, bb]
            vc = cv_ref[hf, bb]
            kcat = _head_variants(kc.astype(BF16), pltpu.roll(kc, HEAD_DIM, 1).astype(BF16),
                                  ck_new, ckr_new)
            vcat = _head_variants(vc.astype(BF16), pltpu.roll(vc, HEAD_DIM, 1).astype(BF16),
                                  cv_new, cvr_new)
            work.append((rows, hf, _scores(q8, kcat, bias_ref[hf]), vcat))
            nk_ref[hf, bb] = pltpu.roll(kc, WINDOW - 4, 0)
            nv_ref[hf, bb] = pltpu.roll(vc, WINDOW - 4, 0)
            for t in range(4):
                nk_ref[hf, bb, WINDOW - 4 + t:WINDOW - 3 + t, :] = k8[2 * t + hf:2 * t + hf + 1, :]
                nv_ref[hf, bb, WINDOW - 4 + t:WINDOW - 3 + t, :] = v8[2 * t + hf:2 * t + hf + 1, :]
    probs = [_sink_softmax(s, sinks_ref) for _, _, s, _ in work]
    outs = [_weighted_values(pr, w[3]) for pr, w in zip(probs, work)]
    for (rows, _, _, _), o_even, o_odd in zip(work[0::2], outs[0::2], outs[1::2]):
        for p in range(n_slabs):
            oa_ref[p, rows, :] = jnp.where(row_half == 0, o_even[p], o_odd[p])


def _sample_attn(q, k, v, ck, cv, p):
    n_slabs, n, _ = q.shape
    nb = n // 4
    half = nb // 2
    bt = SAMPLE_BATCH_TILE
    assert half % bt == 0
    ck4 = ck.reshape(2, half, WINDOW, KV_WIDTH)
    cv4 = cv.reshape(2, half, WINDOW, KV_WIDTH)
    cache_spec = pl.BlockSpec((2, bt, WINDOW, KV_WIDTH), lambda i: (0, i, 0, 0))
    return pl.pallas_call(
        _sample_attn_kernel,
        grid=(half // bt,),
        in_specs=[_const_spec((n_slabs, n, LANES)), _const_spec((n, KV_WIDTH)),
                  _const_spec((n, KV_WIDTH)), cache_spec, cache_spec, _smem_spec(),
                  _const_spec((2, 8, 4 * WINDOW))],
        out_specs=[_const_spec((n_slabs, n, LANES), False), cache_spec, cache_spec],
        out_shape=[jax.ShapeDtypeStruct((n_slabs, n, LANES), F32),
                   jax.ShapeDtypeStruct(ck4.shape, F32), jax.ShapeDtypeStruct(cv4.shape, F32)],
        compiler_params=pltpu.CompilerParams(
            dimension_semantics=("arbitrary",), vmem_limit_bytes=VMEM_LIMIT),
        name="sample_attn",
    )(q, k, v, ck4, cv4, p["sinks"], p["bias_sample"])


def _sample_merge_kernel(x_ref, oa_ref, ob_ref, sga_ref, sgb_ref, wa_ref, wb_ref, wout_ref, y_ref):
    oa = jnp.concatenate([oa_ref[p] for p in range(oa_ref.shape[0])], axis=1)
    y_ref[...] = _merge(x_ref[...], sga_ref[...], sgb_ref[...], oa, ob_ref[...],
                        wa_ref, wb_ref, wout_ref)


def _sample_merge(x, oa, ob, sga, sgb, p):
    n, d = x.shape
    return pl.pallas_call(
        _sample_merge_kernel,
        grid=(1,),
        in_specs=[_const_spec((n, d)), _const_spec(oa.shape), _const_spec((n, SGU_WIDTH)),
                  _const_spec((n, d)), _const_spec((n, d)),
                  _const_spec((ATTN_WIDTH, d)), _const_spec((SGU_WIDTH, d)), _const_spec((d, d))],
        out_specs=_const_spec((n, d), False),
        out_shape=jax.ShapeDtypeStruct((n, d), F32),
        compiler_params=pltpu.CompilerParams(
            dimension_semantics=("arbitrary",), vmem_limit_bytes=VMEM_LIMIT),
        name="sample_merge",
    )(x, oa, ob, sga, sgb, p["wa"], p["wb"], p["wout"])


def _route(logits):
    lane = lax.broadcasted_iota(jnp.int32, logits.shape, 1).astype(F32)
    far = float(ROUTER_LANES)
    glm = jnp.where(lane < N_GROUPS, logits, NEG_INF)
    gmax = jnp.max(glm, axis=-1, keepdims=True)
    gidx = jnp.min(jnp.where(glm == gmax, lane, far), axis=-1, keepdims=True)
    gw = 1.0 / jnp.sum(jnp.exp(glm - gmax), axis=-1, keepdims=True)
    first = EXPERT_LANE0 + EXPERTS_PER_GROUP * gidx
    sel = (lane >= first) & (lane < first + EXPERTS_PER_GROUP)
    el = jnp.where(sel, logits, NEG_INF)
    t1 = jnp.max(el, axis=-1, keepdims=True)
    i1 = jnp.min(jnp.where(el == t1, lane, far), axis=-1, keepdims=True)
    el2 = jnp.where(lane == i1, NEG_INF, el)
    t2 = jnp.max(el2, axis=-1, keepdims=True)
    i2 = jnp.min(jnp.where(el2 == t2, lane, far), axis=-1, keepdims=True)
    e2 = jnp.exp(t2 - t1)
    den = 1.0 + e2
    w1 = (1.0 / den) * gw
    w2 = (e2 / den) * gw
    return gidx, jnp.where(lane == i1 - first, w1, 0.0) + jnp.where(lane == i2 - first, w2, 0.0)


def _split_bf16(x):
    hi = x.astype(BF16)
    lo = (x - hi.astype(F32)).astype(BF16)
    return hi, lo


def _moe_kernel(y_ref, g2_ref, wrc_ref, wrh_ref, br_ref, wg_ref, wu_ref, wd_ref,
                ltri_ref, eye_ref, o_ref, hs_ref, cws_ref, os_ref):
    n_tiles = y_ref.shape[0] // MOE_TILE
    tiles = [slice(t * MOE_TILE, (t + 1) * MOE_TILE) for t in range(n_tiles)]

    @pl.when(pl.program_id(0) == 0)
    def _():
        os_ref[...] = jnp.zeros_like(os_ref)

    h_split = [_split_bf16(_rms(y_ref[r, :], g2_ref[...])) for r in tiles]
    part = [_dot(hb, wrc_ref[...]) for hb, _ in h_split]
    low = [_dot(hl, wrh_ref[...]) for _, hl in h_split]
    routed = [_route(pt[:, :ROUTER_LANES] + (pt[:, ROUTER_LANES:] + lw) + br_ref[...])
              for pt, lw in zip(part, low)]

    lane = lax.broadcasted_iota(jnp.int32, (MOE_TILE, ROUTER_LANES), 1).astype(F32)
    lane1 = lane[0:1]
    onehot = [jnp.where(lane == gidx, 1.0, 0.0) for gidx, _ in routed]
    before = [_dot(ltri_ref[...], oh.astype(BF16)) for oh in onehot]
    ends, pos = [], []
    for oh, bf in zip(onehot, before):
        count = jnp.sum(oh, axis=0, keepdims=True)
        n_chunks = jnp.floor((count + (MOE_CHUNK - 1)) * (1.0 / MOE_CHUNK))
        tile_ends = []
        start_vec = jnp.zeros_like(n_chunks)
        end = jnp.zeros((1, 1), F32)
        for g in range(N_GROUPS):
            start_vec = start_vec + jnp.where(lane1 == g, end * MOE_CHUNK, 0.0)
            end = end + jnp.sum(jnp.where(lane1 == g, n_chunks, 0.0), axis=-1, keepdims=True)
            tile_ends.append(end[0, 0].astype(jnp.int32))
        ends.append(tile_ends)
        pos.append(jnp.sum(oh * (bf + start_vec), axis=-1, keepdims=True))

    pos_t = []
    for ps in pos:
        pos_hi = jnp.floor(ps * (1.0 / SLOT_SPLIT))
        pos_lo = ps - pos_hi * SLOT_SPLIT
        cols = jnp.where(lane == 0.0, pos_hi, jnp.where(lane == 1.0, pos_lo, 0.0)).astype(BF16)
        pos_t.append(_dot_nt(eye_ref[...], cols))
    slot_s = lax.broadcasted_iota(jnp.int32, (MOE_SLOTS, MOE_TILE), 0).astype(F32)
    for t, (pt, (hb, _), (_, cw)) in enumerate(zip(pos_t, h_split, routed)):
        pos_row = pt[0:1] * SLOT_SPLIT + pt[1:2]
        sort = jnp.where(slot_s == pos_row, 1.0, 0.0).astype(BF16)
        cw_hi = cw.astype(BF16).astype(F32)
        cw_parts = (cw_hi + pltpu.roll(cw - cw_hi, LANES // 2, 1)).astype(BF16)
        sorted_rows = _dot(sort, jnp.concatenate([hb, cw_parts], axis=1))
        hs_ref[t] = sorted_rows[:, :D_MODEL].astype(BF16)
        parts = sorted_rows[:, D_MODEL:]
        cws_ref[t] = parts + pltpu.roll(parts, LANES // 2, 1)

    def run_group(t, g, first_chunk, n_rows):
        rows = pl.ds(pl.multiple_of(first_chunk * MOE_CHUNK, MOE_CHUNK), n_rows)
        hsc = hs_ref[t, rows, :]
        cwc = cws_ref[t, rows, :]
        cexp = jnp.concatenate([jnp.broadcast_to(cwc[:, e:e + 1], (n_rows, D_EXPERT))
                                for e in range(EXPERTS_PER_GROUP)], axis=1)
        experts = [EXPERTS_PER_GROUP * g + e for e in range(EXPERTS_PER_GROUP)]
        a = jnp.concatenate([_dot(hsc, wg_ref[e]) for e in experts], axis=1)
        u = jnp.concatenate([_dot(hsc, wu_ref[e]) for e in experts], axis=1)
        hid = (a * jax.nn.sigmoid(a)) * u * cexp
        os_ref[t, rows, :] = _dot(hid.astype(BF16), wd_ref[g]).astype(BF16)

    def group(i, carry):
        t = lax.shift_right_logical(i, GROUP_BITS)
        g = i & (N_GROUPS - 1)
        first, last = jnp.int32(0), jnp.int32(0)
        for tt in range(n_tiles):
            for gg in range(N_GROUPS):
                here = (t == tt) & (g == gg)
                first = jnp.where(here, ends[tt][gg - 1] if gg else 0, first)
                last = jnp.where(here, ends[tt][gg], last)
        n = last - first
        for k in range(1, MOE_TILE // MOE_CHUNK + 1):
            @pl.when(n == k)
            def _(k=k):
                run_group(t, g, first, k * MOE_CHUNK)
        return carry

    lax.fori_loop(0, n_tiles * N_GROUPS, group, 0)

    slot_l = lax.broadcasted_iota(jnp.int32, (MOE_TILE, MOE_SLOTS), 1).astype(F32)
    for t, (r, ps) in enumerate(zip(tiles, pos)):
        unsort = jnp.where(slot_l == ps, 1.0, 0.0).astype(BF16)
        o_ref[r, :] = y_ref[r, :] + _dot(unsort, os_ref[t])


def _moe(y, p):
    n, d = y.shape
    tm = min(MOE_STEP_ROWS, n)
    assert n % tm == 0 and tm % MOE_TILE == 0
    n_tiles = tm // MOE_TILE
    row_spec = pl.BlockSpec((tm, d), lambda i: (i, 0))
    return pl.pallas_call(
        _moe_kernel,
        grid=(n // tm,),
        in_specs=[row_spec, _const_spec((1, d)), _const_spec((d, 2 * ROUTER_LANES)),
                  _const_spec((d, ROUTER_LANES)), _const_spec((1, ROUTER_LANES)),
                  _const_spec((N_EXPERTS, d, D_EXPERT)), _const_spec((N_EXPERTS, d, D_EXPERT)),
                  _const_spec((N_GROUPS, GROUP_HIDDEN, d)),
                  _const_spec((MOE_TILE, MOE_TILE)), _const_spec((LANES, LANES))],
        out_specs=row_spec,
        out_shape=jax.ShapeDtypeStruct((n, d), F32),
        scratch_shapes=[pltpu.VMEM((n_tiles, MOE_SLOTS, d), BF16),
                        pltpu.VMEM((n_tiles, MOE_SLOTS, ROUTER_LANES), F32),
                        pltpu.VMEM((n_tiles, MOE_SLOTS, d), BF16)],
        compiler_params=pltpu.CompilerParams(
            dimension_semantics=("arbitrary",), vmem_limit_bytes=MOE_VMEM_LIMIT),
        name="expert_mixer",
    )(y, p["g2"], p["wr_cat"], p["wr_hi"], p["br"], p["wg"], p["wu"], p["wd"],
      p["ltri"], p["eye"])


def _window_bias(n_q, q_tok, key_prev_ok, key_cur_ok):
    half = np.concatenate([key_prev_ok, key_cur_ok], axis=1)
    ok = np.concatenate([half, half], axis=1)
    return np.where(ok, 0.0, NEG_INF).astype(np.float32)


def _prompt_bias():
    t = np.arange(WINDOW)[:, None]
    s = np.arange(WINDOW)[None, :]
    prev_ok = s > t
    cur_ok = s <= t
    normal = _window_bias(WINDOW, t, prev_ok, cur_ok)
    first = _window_bias(WINDOW, t, np.zeros_like(prev_ok), cur_ok)
    return np.stack([normal, first])


def _sample_bias():
    out = []
    r = np.arange(8)[:, None]
    t = r // 2
    s = np.arange(WINDOW)[None, :]
    prev_ok = s > t
    for hf in range(2):
        c = s
        cur_ok = (c < 8) & (c % 2 == hf) & (c // 2 <= t)
        out.append(_window_bias(8, t, np.broadcast_to(prev_ok, (8, WINDOW)), cur_ok))
    return np.stack(out)


def _block_diag_mean(width):
    idx = np.arange(width) // HEAD_DIM
    return (idx[:, None] == idx[None, :]).astype(np.float32) / HEAD_DIM


def _prepare(norm1_g, w_in, q_norm_g, k_norm_g, attn_sinks, ln_v_g, ln_v_b, w_spatial, b_spatial,
             w_branch_a, w_branch_b, w_out, norm2_g, w_router_group, b_router_group,
             w_router_expert, b_router_expert, w_exp_gate, w_exp_up, w_exp_down):
    p = {}
    p["g1"] = norm1_g.reshape(1, D_MODEL)
    p["w_in"] = w_in.astype(BF16)
    p["gq"] = (jnp.tile(q_norm_g, N_HEADS) * (HEAD_DIM ** -0.5)).reshape(1, ATTN_WIDTH)
    p["gk"] = jnp.tile(k_norm_g, N_KV_HEADS).reshape(1, KV_WIDTH)
    p["bq"] = jnp.asarray(_block_diag_mean(ATTN_WIDTH), BF16)
    p["bk"] = jnp.asarray(_block_diag_mean(KV_WIDTH), BF16)
    p["sinks"] = attn_sinks.astype(F32)
    p["lng"] = ln_v_g.reshape(1, SGU_WIDTH)
    p["lnb"] = ln_v_b.reshape(1, SGU_WIDTH)
    tril = jnp.tril(jnp.ones((CHUNK, CHUNK), F32))
    wsp = w_spatial * tril[None]
    p["wsp"] = wsp.astype(BF16)
    p["bsp"] = jnp.repeat(b_spatial.T, SGU_GROUP_DIM, axis=1)
    w4 = wsp[:, :4, :4]
    p["coef_s"] = jnp.repeat(jnp.transpose(w4, (1, 2, 0)).reshape(16, SGU_GROUPS),
                             SGU_GROUP_DIM, axis=1)
    p["bias_sgu_s"] = jnp.repeat(b_spatial[:, :4].T, SGU_GROUP_DIM, axis=1)
    p["wa"] = w_branch_a.astype(BF16)
    p["wb"] = w_branch_b.astype(BF16)
    p["wout"] = w_out.astype(BF16)
    p["g2"] = norm2_g.reshape(1, D_MODEL)
    wr = jnp.concatenate(
        [w_router_group, w_router_expert,
         jnp.zeros((D_MODEL, ROUTER_LANES - N_GROUPS - N_EXPERTS), F32)], axis=1)
    p["wr_hi"], wr_lo = _split_bf16(wr)
    p["wr_cat"] = jnp.concatenate([p["wr_hi"], wr_lo], axis=1)
    p["br"] = jnp.concatenate(
        [b_router_group, b_router_expert,
         jnp.zeros((ROUTER_LANES - N_GROUPS - N_EXPERTS,), F32)]).reshape(1, ROUTER_LANES)
    p["wg"] = w_exp_gate.astype(BF16)
    p["wu"] = w_exp_up.astype(BF16)
    p["wd"] = w_exp_down.reshape(N_GROUPS, GROUP_HIDDEN, D_MODEL).astype(BF16)
    p["ltri"] = jnp.asarray(np.tril(np.ones((MOE_TILE, MOE_TILE), np.float32), -1), BF16)
    p["eye"] = jnp.asarray(np.eye(LANES, dtype=np.float32), BF16)
    p["bias_prompt"] = jnp.asarray(_prompt_bias())
    p["bias_sample"] = jnp.asarray(_sample_bias())
    return p


def kernel(x_prompt, x_sample, cache_k_win, cache_v_win, norm1_g, w_in, q_norm_g, k_norm_g, attn_sinks, ln_v_g, ln_v_b, w_spatial, b_spatial, w_branch_a, w_branch_b, w_out, norm2_g, w_router_group, b_router_group, w_router_expert, b_router_expert, w_exp_gate, w_exp_up, w_exp_down):
    depth = norm1_g.shape[0]
    assert depth == 1
    batch, seq, d = x_prompt.shape
    dec_batch, dec_seq, _ = x_sample.shape
    assert dec_seq == 4 and d == D_MODEL
    p = _prepare(*(a[0] for a in (
        norm1_g, w_in, q_norm_g, k_norm_g, attn_sinks, ln_v_g, ln_v_b, w_spatial, b_spatial,
        w_branch_a, w_branch_b, w_out, norm2_g, w_router_group, b_router_group,
        w_router_expert, b_router_expert, w_exp_gate, w_exp_up, w_exp_down)))

    y1p, kwin, vwin = _prompt_mixer(x_prompt, p)
    yp = _moe(y1p.reshape(batch * seq, d), p).reshape(batch, seq, d)

    xs = jnp.transpose(x_sample, (1, 0, 2)).reshape(dec_seq * dec_batch, d)
    q, k, v, ob, vs, sga, sgb = _sample_in(xs, p)
    ck = cache_k_win[0].reshape(dec_batch, WINDOW, KV_WIDTH)
    cv = cache_v_win[0].reshape(dec_batch, WINDOW, KV_WIDTH)
    oa, nk, nv = _sample_attn(q, k, v, ck, cv, p)
    y1s = _sample_merge(xs, oa, ob, sga, sgb, p)
    ys = _moe(y1s, p)
    ys = jnp.transpose(ys.reshape(dec_seq, dec_batch, d), (1, 0, 2))
    vs_out = jnp.transpose(vs.reshape(dec_seq, dec_batch, SGU_GROUPS, SGU_GROUP_DIM), (1, 0, 2, 3))

    kv_shape = (WINDOW, N_KV_HEADS, HEAD_DIM)
    return (yp, ys,
            kwin.reshape(1, batch, *kv_shape), vwin.reshape(1, batch, *kv_shape),
            nk.reshape(1, dec_batch, *kv_shape), nv.reshape(1, dec_batch, *kv_shape),
            vs_out[None])
```

```python
import functools

import numpy as np
import jax
import jax.numpy as jnp
from jax import lax
from jax.experimental import pallas as pl
from jax.experimental.pallas import tpu as pltpu

F32 = jnp.float32
BF16 = jnp.bfloat16

D_MODEL = 1024
N_HEADS = 8
N_KV_HEADS = 2
HEAD_DIM = 64
WINDOW = 128
ATTN_WIDTH = N_HEADS * HEAD_DIM
KV_WIDTH = N_KV_HEADS * HEAD_DIM
SGU_GROUPS = 4
SGU_WIDTH = D_MODEL // 2
SGU_GROUP_DIM = SGU_WIDTH // SGU_GROUPS
CHUNK = 128
N_GROUPS = 4
GROUP_BITS = 2
EXPERTS_PER_GROUP = 4
N_EXPERTS = N_GROUPS * EXPERTS_PER_GROUP
D_EXPERT = 256
GROUP_HIDDEN = EXPERTS_PER_GROUP * D_EXPERT
D_IN = ATTN_WIDTH + 2 * KV_WIDTH + 2 * SGU_WIDTH + 2 * D_MODEL
EPS = 1e-6
NEG_INF = -1e30

C_Q = 0
C_K = C_Q + ATTN_WIDTH
C_V = C_K + KV_WIDTH
C_U = C_V + KV_WIDTH
C_VS = C_U + SGU_WIDTH
C_GA = C_VS + SGU_WIDTH
C_GB = C_GA + D_MODEL

LANES = 128
ROUTER_LANES = LANES
EXPERT_LANE0 = N_GROUPS
PROMPT_TILE = 512
MOE_TILE = 512
MOE_STEP_ROWS = 2 * MOE_TILE
MOE_CHUNK = 32
MOE_PASS_CHUNKS = 8
MOE_MAX_CHUNKS = (MOE_TILE + N_GROUPS * (MOE_CHUNK - 1)) // MOE_CHUNK
MOE_SLOTS = -(-MOE_MAX_CHUNKS * MOE_CHUNK // LANES) * LANES
SLOT_SPLIT = 128
SAMPLE_BATCH_TILE = 8
VMEM_LIMIT = 56 * 1024 * 1024
MOE_VMEM_LIMIT = 62 * 1024 * 1024

_SQRT_2_OVER_PI = np.sqrt(2.0 / np.pi).astype(np.float32)


def _dot(a, b):
    return jnp.dot(a, b, preferred_element_type=F32)


def _dot_nt(a, b):
    return lax.dot_general(a, b, (((1,), (1,)), ((), ())), preferred_element_type=F32)


def _gelu(x):
    cdf = 0.5 * (1.0 + jnp.tanh(_SQRT_2_OVER_PI * (x + 0.044715 * (x * x * x))))
    return x * cdf


def _rms(x, g):
    return x * lax.rsqrt(jnp.mean(x * x, axis=-1, keepdims=True) + EPS) * g


def _head_rms(x, blockdiag, g):
    ms = _dot((x * x).astype(BF16), blockdiag)
    return x * lax.rsqrt(ms + EPS) * g


def _qkv(x, g1, win_ref, gq, gk, bq, bk):
    h = _rms(x, g1).astype(BF16)
    qn = _head_rms(_dot(h, win_ref[:, C_Q:C_K]), bq, gq)
    kv = _dot(h, win_ref[:, C_K:C_U])
    kn = _head_rms(kv[:, :KV_WIDTH], bk, gk)
    v = kv[:, KV_WIDTH:]
    return h, qn, kn, v


def _sgu_inputs(h, win_ref, lng, lnb):
    gu = _gelu(_dot(h, win_ref[:, C_U:C_VS]))
    gv = _gelu(_dot(h, win_ref[:, C_VS:C_GA]))
    mu = jnp.mean(gv, axis=-1, keepdims=True)
    xc = gv - mu
    vsn = xc * lax.rsqrt(jnp.mean(xc * xc, axis=-1, keepdims=True) + EPS) * lng + lnb
    return gu, vsn


def _head_variants(prev, prev_rot, cur, cur_rot):
    a = jnp.concatenate([prev, cur], axis=0)
    r = jnp.concatenate([prev_rot, cur_rot], axis=0)
    lo = lax.broadcasted_iota(jnp.int32, a.shape, 1) < HEAD_DIM
    zero = jnp.zeros_like(a)
    kv0 = jnp.concatenate([jnp.where(lo, a, zero), jnp.where(lo, zero, r)], axis=0)
    kv1 = jnp.concatenate([jnp.where(lo, r, zero), jnp.where(lo, zero, a)], axis=0)
    return kv0, kv1


def _lane_slabs(x):
    return [x[:, p * LANES:(p + 1) * LANES] for p in range(x.shape[1] // LANES)]


def _scores(q_slabs, kcat, bias):
    slabs_per_kv = len(q_slabs) // N_KV_HEADS
    m = q_slabs[0].shape[0]
    out = []
    for kv in range(N_KV_HEADS):
        q = jnp.concatenate(q_slabs[kv * slabs_per_kv:(kv + 1) * slabs_per_kv], axis=0)
        s = _dot_nt(q.astype(BF16), kcat[kv])
        out += [s[i * m:(i + 1) * m] + bias for i in range(slabs_per_kv)]
    return out


def _sink_softmax(scores, sinks_ref):
    out = []
    for p, s in enumerate(scores):
        n_keys = s.shape[1] // 2
        probs = []
        for par in range(2):
            sh = s[:, par * n_keys:(par + 1) * n_keys]
            sink = sinks_ref[2 * p + par]
            m = jnp.maximum(jnp.max(sh, axis=-1, keepdims=True), sink)
            e = jnp.exp(sh - m)
            den = jnp.sum(e, axis=-1, keepdims=True) + jnp.exp(sink - m)
            probs.append((e / den).astype(BF16))
        out.append(jnp.concatenate(probs, axis=1))
    return out


def _weighted_values(probs, vcat):
    slabs_per_kv = len(probs) // N_KV_HEADS
    m = probs[0].shape[0]
    out = []
    for kv in range(N_KV_HEADS):
        pr = jnp.concatenate(probs[kv * slabs_per_kv:(kv + 1) * slabs_per_kv], axis=0)
        o = _dot(pr, vcat[kv])
        out += [o[i * m:(i + 1) * m] for i in range(slabs_per_kv)]
    return out


def _gates(h, win_ref):
    return (jax.nn.sigmoid(_dot(h, win_ref[:, C_GA:C_GB])),
            jax.nn.sigmoid(_dot(h, win_ref[:, C_GB:D_IN])))


def _merge(x, sga, sgb, oa, ob, wa_ref, wb_ref, wout_ref):
    ya = _dot(oa.astype(BF16), wa_ref[...])
    yb = _dot(ob.astype(BF16), wb_ref[...])
    hm = sga * ya + sgb * yb
    return x + _dot(hm.astype(BF16), wout_ref[...])


def _prompt_kernel(x_ref, g1_ref, win_ref, gq_ref, gk_ref, bq_ref, bk_ref, sinks_ref, lng_ref,
                   lnb_ref, wsp_ref, bsp_ref, wa_ref, wb_ref, wout_ref, bias_ref,
                   y_ref, kwin_ref, vwin_ref,
                   kprev, kprev_rot, vprev, vprev_rot):
    j = pl.program_id(1)

    @pl.when(j == 0)
    def _():
        for r in (kprev, kprev_rot, vprev, vprev_rot):
            r[...] = jnp.zeros_like(r)

    x = x_ref[0]
    h, qn, kn, v = _qkv(x, g1_ref[...], win_ref, gq_ref[...], gk_ref[...], bq_ref[...], bk_ref[...])

    kb = kn.astype(BF16)
    kr = pltpu.roll(kn, HEAD_DIM, 1).astype(BF16)
    vb = v.astype(BF16)
    vr = pltpu.roll(v, HEAD_DIM, 1).astype(BF16)

    n_blocks = x.shape[0] // WINDOW
    blocks = [slice(i * WINDOW, (i + 1) * WINDOW) for i in range(n_blocks)]
    scores, vcats = [], []
    pk, pkr, pv, pvr = kprev[...], kprev_rot[...], vprev[...], vprev_rot[...]
    for i, rows in enumerate(blocks):
        ck, ckr, cv, cvr = kb[rows], kr[rows], vb[rows], vr[rows]
        kcat = _head_variants(pk, pkr, ck, ckr)
        vcats.append(_head_variants(pv, pvr, cv, cvr))
        bias = bias_ref[jnp.where(j == 0, 1, 0)] if i == 0 else bias_ref[0]
        scores.append(_scores(_lane_slabs(qn[rows]), kcat, bias))
        pk, pkr, pv, pvr = ck, ckr, cv, cvr
    kprev[...] = pk
    kprev_rot[...] = pkr
    vprev[...] = pv
    vprev_rot[...] = pvr

    gu, vsn = _sgu_inputs(h, win_ref, lng_ref[...], lnb_ref[...])
    vsb = vsn.astype(BF16)
    mixed = [_dot(wsp_ref[g], jnp.concatenate(
        [vsb[rows, g * SGU_GROUP_DIM:(g + 1) * SGU_GROUP_DIM] for rows in blocks], axis=1))
        for g in range(SGU_GROUPS)]
    ob = jnp.concatenate([
        gu[rows] * (jnp.concatenate(
            [m[:, i * SGU_GROUP_DIM:(i + 1) * SGU_GROUP_DIM] for m in mixed], axis=1) + bsp_ref[...])
        for i, rows in enumerate(blocks)], axis=0)

    sga, sgb = _gates(h, win_ref)

    probs = [_sink_softmax(s, sinks_ref) for s in scores]
    oa = jnp.concatenate(
        [jnp.concatenate(_weighted_values(pr, vc), axis=1) for pr, vc in zip(probs, vcats)], axis=0)
    y_ref[0] = _merge(x, sga, sgb, oa, ob, wa_ref, wb_ref, wout_ref)

    @pl.when(j == pl.num_programs(1) - 1)
    def _():
        kwin_ref[0] = kn[(n_blocks - 1) * WINDOW:]
        vwin_ref[0] = v[(n_blocks - 1) * WINDOW:]


def _const_spec(shape, single_buffer=True):
    nd = len(shape)
    mode = pl.Buffered(1) if single_buffer else None
    return pl.BlockSpec(shape, lambda *_: (0,) * nd, pipeline_mode=mode)


def _smem_spec():
    return pl.BlockSpec(memory_space=pltpu.SMEM)


def _prompt_mixer(x, p):
    b, s, d = x.shape
    ts = PROMPT_TILE
    assert s % ts == 0 and ts % WINDOW == 0
    in_specs = [
        pl.BlockSpec((1, ts, d), lambda bi, j: (bi, j, 0)),
        _const_spec((1, d)), _const_spec((d, D_IN)),
        _const_spec((1, ATTN_WIDTH)), _const_spec((1, KV_WIDTH)),
        _const_spec((ATTN_WIDTH, ATTN_WIDTH)), _const_spec((KV_WIDTH, KV_WIDTH)),
        _smem_spec(),
        _const_spec((1, SGU_WIDTH)), _const_spec((1, SGU_WIDTH)),
        _const_spec((SGU_GROUPS, CHUNK, CHUNK)), _const_spec((CHUNK, SGU_WIDTH)),
        _const_spec((ATTN_WIDTH, d)), _const_spec((SGU_WIDTH, d)), _const_spec((d, d)),
        _const_spec((2, WINDOW, 4 * WINDOW)),
    ]
    out_specs = [
        pl.BlockSpec((1, ts, d), lambda bi, j: (bi, j, 0)),
        pl.BlockSpec((1, WINDOW, KV_WIDTH), lambda bi, j: (bi, 0, 0)),
        pl.BlockSpec((1, WINDOW, KV_WIDTH), lambda bi, j: (bi, 0, 0)),
    ]
    out_shape = [
        jax.ShapeDtypeStruct((b, s, d), F32),
        jax.ShapeDtypeStruct((b, WINDOW, KV_WIDTH), F32),
        jax.ShapeDtypeStruct((b, WINDOW, KV_WIDTH), F32),
    ]
    return pl.pallas_call(
        _prompt_kernel,
        grid=(b, s // ts),
        in_specs=in_specs, out_specs=out_specs, out_shape=out_shape,
        scratch_shapes=[pltpu.VMEM((WINDOW, KV_WIDTH), BF16)] * 4,
        compiler_params=pltpu.CompilerParams(
            dimension_semantics=("arbitrary", "arbitrary"), vmem_limit_bytes=VMEM_LIMIT),
        name="prompt_mixer",
    )(x, p["g1"], p["w_in"], p["gq"], p["gk"], p["bq"], p["bk"], p["sinks"], p["lng"], p["lnb"],
      p["wsp"], p["bsp"], p["wa"], p["wb"], p["wout"], p["bias_prompt"])


def _sample_in_kernel(x_ref, g1_ref, win_ref, gq_ref, gk_ref, bq_ref, bk_ref, lng_ref, lnb_ref,
                      coef_ref, sbias_ref,
                      q_ref, k_ref, v_ref, ob_ref, vs_ref, sga_ref, sgb_ref):
    x = x_ref[...]
    h, qn, kn, v = _qkv(x, g1_ref[...], win_ref, gq_ref[...], gk_ref[...], bq_ref[...], bk_ref[...])
    gu, vsn = _sgu_inputs(h, win_ref, lng_ref[...], lnb_ref[...])
    for p, slab in enumerate(_lane_slabs(qn)):
        q_ref[p] = slab
    k_ref[...] = kn
    v_ref[...] = v
    vs_ref[...] = vsn
    sga_ref[...], sgb_ref[...] = _gates(h, win_ref)
    nb = x.shape[0] // 4
    for t in range(4):
        mixed = sbias_ref[t:t + 1, :]
        for jj in range(t + 1):
            mixed = mixed + coef_ref[4 * t + jj:4 * t + jj + 1, :] * vsn[jj * nb:(jj + 1) * nb]
        ob_ref[t * nb:(t + 1) * nb, :] = gu[t * nb:(t + 1) * nb] * mixed


def _sample_in(x, p):
    n, d = x.shape
    shapes = [(ATTN_WIDTH // LANES, n, LANES), (n, KV_WIDTH), (n, KV_WIDTH), (n, SGU_WIDTH),
              (n, SGU_WIDTH), (n, d), (n, d)]
    return pl.pallas_call(
        _sample_in_kernel,
        grid=(1,),
        in_specs=[_const_spec((n, d)), _const_spec((1, d)), _const_spec((d, D_IN)),
                  _const_spec((1, ATTN_WIDTH)), _const_spec((1, KV_WIDTH)),
                  _const_spec((ATTN_WIDTH, ATTN_WIDTH)), _const_spec((KV_WIDTH, KV_WIDTH)),
                  _const_spec((1, SGU_WIDTH)), _const_spec((1, SGU_WIDTH)),
                  _const_spec((16, SGU_WIDTH)), _const_spec((4, SGU_WIDTH))],
        out_specs=[_const_spec(sh, False) for sh in shapes],
        out_shape=[jax.ShapeDtypeStruct(sh, F32) for sh in shapes],
        compiler_params=pltpu.CompilerParams(
            dimension_semantics=("arbitrary",), vmem_limit_bytes=VMEM_LIMIT),
        name="sample_in",
    )(x, p["g1"], p["w_in"], p["gq"], p["gk"], p["bq"], p["bk"], p["lng"], p["lnb"],
      p["coef_s"], p["bias_sgu_s"])


def _sample_attn_kernel(q_ref, k_ref, v_ref, ck_ref, cv_ref, sinks_ref, bias_ref,
                        oa_ref, nk_ref, nv_ref):
    i = pl.program_id(0)
    n_slabs = q_ref.shape[0]
    half_rows = q_ref.shape[1] // 8
    pad = jnp.zeros((WINDOW - 8, KV_WIDTH), BF16)
    row_half = lax.broadcasted_iota(jnp.int32, (8, LANES), 0) % 2

    work = []
    for bb in range(SAMPLE_BATCH_TILE):
        b_lo = i * SAMPLE_BATCH_TILE + bb
        rows = pl.ds(b_lo, 8, stride=half_rows)
        q8 = [q_ref[p, rows, :] for p in range(n_slabs)]
        k8 = k_ref[rows, :]
        v8 = v_ref[rows, :]
        ck_new = jnp.concatenate([k8.astype(BF16), pad], axis=0)
        ckr_new = jnp.concatenate([pltpu.roll(k8, HEAD_DIM, 1).astype(BF16), pad], axis=0)
        cv_new = jnp.concatenate([v8.astype(BF16), pad], axis=0)
        cvr_new = jnp.concatenate([pltpu.roll(v8, HEAD_DIM, 1).astype(BF16), pad], axis=0)
        for hf in range(2):
            kc = ck_ref[hf, bb]
            vc = cv_ref[hf, bb]
            kcat = _head_variants(kc.astype(BF16), pltpu.roll(kc, HEAD_DIM, 1).astype(BF16),
                                  ck_new, ckr_new)
            vcat = _head_variants(vc.astype(BF16), pltpu.roll(vc, HEAD_DIM, 1).astype(BF16),
                                  cv_new, cvr_new)
            work.append((rows, hf, _scores(q8, kcat, bias_ref[hf]), vcat))
            nk_ref[hf, bb] = pltpu.roll(kc, WINDOW - 4, 0)
            nv_ref[hf, bb] = pltpu.roll(vc, WINDOW - 4, 0)
            for t in range(4):
                nk_ref[hf, bb, WINDOW - 4 + t:WINDOW - 3 + t, :] = k8[2 * t + hf:2 * t + hf + 1, :]
                nv_ref[hf, bb, WINDOW - 4 + t:WINDOW - 3 + t, :] = v8[2 * t + hf:2 * t + hf + 1, :]
    probs = [_sink_softmax(s, sinks_ref) for _, _, s, _ in work]
    outs = [_weighted_values(pr, w[3]) for pr, w in zip(probs, work)]
    for (rows, _, _, _), o_even, o_odd in zip(work[0::2], outs[0::2], outs[1::2]):
        for p in range(n_slabs):
            oa_ref[p, rows, :] = jnp.where(row_half == 0, o_even[p], o_odd[p])


def _sample_attn(q, k, v, ck, cv, p):
    n_slabs, n, _ = q.shape
    nb = n // 4
    half = nb // 2
    bt = SAMPLE_BATCH_TILE
    assert half % bt == 0
    ck4 = ck.reshape(2, half, WINDOW, KV_WIDTH)
    cv4 = cv.reshape(2, half, WINDOW, KV_WIDTH)
    cache_spec = pl.BlockSpec((2, bt, WINDOW, KV_WIDTH), lambda i: (0, i, 0, 0))
    return pl.pallas_call(
        _sample_attn_kernel,
        grid=(half // bt,),
        in_specs=[_const_spec((n_slabs, n, LANES)), _const_spec((n, KV_WIDTH)),
                  _const_spec((n, KV_WIDTH)), cache_spec, cache_spec, _smem_spec(),
                  _const_spec((2, 8, 4 * WINDOW))],
        out_specs=[_const_spec((n_slabs, n, LANES), False), cache_spec, cache_spec],
        out_shape=[jax.ShapeDtypeStruct((n_slabs, n, LANES), F32),
                   jax.ShapeDtypeStruct(ck4.shape, F32), jax.ShapeDtypeStruct(cv4.shape, F32)],
        compiler_params=pltpu.CompilerParams(
            dimension_semantics=("arbitrary",), vmem_limit_bytes=VMEM_LIMIT),
        name="sample_attn",
    )(q, k, v, ck4, cv4, p["sinks"], p["bias_sample"])


def _sample_merge_kernel(x_ref, oa_ref, ob_ref, sga_ref, sgb_ref, wa_ref, wb_ref, wout_ref, y_ref):
    oa = jnp.concatenate([oa_ref[p] for p in range(oa_ref.shape[0])], axis=1)
    y_ref[...] = _merge(x_ref[...], sga_ref[...], sgb_ref[...], oa, ob_ref[...],
                        wa_ref, wb_ref, wout_ref)


def _sample_merge(x, oa, ob, sga, sgb, p):
    n, d = x.shape
    return pl.pallas_call(
        _sample_merge_kernel,
        grid=(1,),
        in_specs=[_const_spec((n, d)), _const_spec(oa.shape), _const_spec((n, SGU_WIDTH)),
                  _const_spec((n, d)), _const_spec((n, d)),
                  _const_spec((ATTN_WIDTH, d)), _const_spec((SGU_WIDTH, d)), _const_spec((d, d))],
        out_specs=_const_spec((n, d), False),
        out_shape=jax.ShapeDtypeStruct((n, d), F32),
        compiler_params=pltpu.CompilerParams(
            dimension_semantics=("arbitrary",), vmem_limit_bytes=VMEM_LIMIT),
        name="sample_merge",
    )(x, oa, ob, sga, sgb, p["wa"], p["wb"], p["wout"])


def _route(logits):
    lane = lax.broadcasted_iota(jnp.int32, logits.shape, 1).astype(F32)
    far = float(ROUTER_LANES)
    glm = jnp.where(lane < N_GROUPS, logits, NEG_INF)
    gmax = jnp.max(glm, axis=-1, keepdims=True)
    gidx = jnp.min(jnp.where(glm == gmax, lane, far), axis=-1, keepdims=True)
    gw = 1.0 / jnp.sum(jnp.exp(glm - gmax), axis=-1, keepdims=True)
    first = EXPERT_LANE0 + EXPERTS_PER_GROUP * gidx
    sel = (lane >= first) & (lane < first + EXPERTS_PER_GROUP)
    el = jnp.where(sel, logits, NEG_INF)
    t1 = jnp.max(el, axis=-1, keepdims=True)
    i1 = jnp.min(jnp.where(el == t1, lane, far), axis=-1, keepdims=True)
    el2 = jnp.where(lane == i1, NEG_INF, el)
    t2 = jnp.max(el2, axis=-1, keepdims=True)
    i2 = jnp.min(jnp.where(el2 == t2, lane, far), axis=-1, keepdims=True)
    e2 = jnp.exp(t2 - t1)
    den = 1.0 + e2
    w1 = (1.0 / den) * gw
    w2 = (e2 / den) * gw
    return gidx, jnp.where(lane == i1 - first, w1, 0.0) + jnp.where(lane == i2 - first, w2, 0.0)


def _split_bf16(x):
    hi = x.astype(BF16)
    lo = (x - hi.astype(F32)).astype(BF16)
    return hi, lo


def _moe_kernel(y_ref, g2_ref, wrc_ref, wrh_ref, br_ref, wg_ref, wu_ref, wd_ref,
                ltri_ref, eye_ref, o_ref, hs_ref, cws_ref, os_ref):
    n_tiles = y_ref.shape[0] // MOE_TILE
    tiles = [slice(t * MOE_TILE, (t + 1) * MOE_TILE) for t in range(n_tiles)]

    @pl.when(pl.program_id(0) == 0)
    def _():
        os_ref[...] = jnp.zeros_like(os_ref)

    h_split = [_split_bf16(_rms(y_ref[r, :], g2_ref[...])) for r in tiles]
    part = [_dot(hb, wrc_ref[...]) for hb, _ in h_split]
    low = [_dot(hl, wrh_ref[...]) for _, hl in h_split]
    routed = [_route(pt[:, :ROUTER_LANES] + (pt[:, ROUTER_LANES:] + lw) + br_ref[...])
              for pt, lw in zip(part, low)]

    lane = lax.broadcasted_iota(jnp.int32, (MOE_TILE, ROUTER_LANES), 1).astype(F32)
    lane1 = lane[0:1]
    onehot = [jnp.where(lane == gidx, 1.0, 0.0) for gidx, _ in routed]
    before = [_dot(ltri_ref[...], oh.astype(BF16)) for oh in onehot]
    ends, pos = [], []
    for oh, bf in zip(onehot, before):
        count = jnp.sum(oh, axis=0, keepdims=True)
        n_chunks = jnp.floor((count + (MOE_CHUNK - 1)) * (1.0 / MOE_CHUNK))
        tile_ends = []
        start_vec = jnp.zeros_like(n_chunks)
        end = jnp.zeros((1, 1), F32)
        for g in range(N_GROUPS):
            start_vec = start_vec + jnp.where(lane1 == g, end * MOE_CHUNK, 0.0)
            end = end + jnp.sum(jnp.where(lane1 == g, n_chunks, 0.0), axis=-1, keepdims=True)
            tile_ends.append(end[0, 0].astype(jnp.int32))
        ends.append(tile_ends)
        pos.append(jnp.sum(oh * (bf + start_vec), axis=-1, keepdims=True))

    pos_t = []
    for ps in pos:
        pos_hi = jnp.floor(ps * (1.0 / SLOT_SPLIT))
        pos_lo = ps - pos_hi * SLOT_SPLIT
        cols = jnp.where(lane == 0.0, pos_hi, jnp.where(lane == 1.0, pos_lo, 0.0)).astype(BF16)
        pos_t.append(_dot_nt(eye_ref[...], cols))
    slot_s = lax.broadcasted_iota(jnp.int32, (MOE_SLOTS, MOE_TILE), 0).astype(F32)
    for t, (pt, (hb, _), (_, cw)) in enumerate(zip(pos_t, h_split, routed)):
        pos_row = pt[0:1] * SLOT_SPLIT + pt[1:2]
        sort = jnp.where(slot_s == pos_row, 1.0, 0.0).astype(BF16)
        cw_hi = cw.astype(BF16).astype(F32)
        cw_parts = (cw_hi + pltpu.roll(cw - cw_hi, LANES // 2, 1)).astype(BF16)
        sorted_rows = _dot(sort, jnp.concatenate([hb, cw_parts], axis=1))
        hs_ref[t] = sorted_rows[:, :D_MODEL].astype(BF16)
        parts = sorted_rows[:, D_MODEL:]
        cws_ref[t] = parts + pltpu.roll(parts, LANES // 2, 1)

    def run_group(t, g, first_chunk, n_rows):
        rows = pl.ds(pl.multiple_of(first_chunk * MOE_CHUNK, MOE_CHUNK), n_rows)
        hsc = hs_ref[t, rows, :]
        cwc = cws_ref[t, rows, :]
        cexp = jnp.concatenate([jnp.broadcast_to(cwc[:, e:e + 1], (n_rows, D_EXPERT))
                                for e in range(EXPERTS_PER_GROUP)], axis=1)
        experts = [EXPERTS_PER_GROUP * g + e for e in range(EXPERTS_PER_GROUP)]
        a = jnp.concatenate([_dot(hsc, wg_ref[e]) for e in experts], axis=1)
        u = jnp.concatenate([_dot(hsc, wu_ref[e]) for e in experts], axis=1)
        hid = (a * jax.nn.sigmoid(a)) * u * cexp
        os_ref[t, rows, :] = _dot(hid.astype(BF16), wd_ref[g]).astype(BF16)

    def group(i, carry):
        t = lax.shift_right_logical(i, GROUP_BITS)
        g = i & (N_GROUPS - 1)
        first, last = jnp.int32(0), jnp.int32(0)
        for tt in range(n_tiles):
            for gg in range(N_GROUPS):
                here = (t == tt) & (g == gg)
                first = jnp.where(here, ends[tt][gg - 1] if gg else 0, first)
                last = jnp.where(here, ends[tt][gg], last)
        def full_pass(state):
            start, left = state
            run_group(t, g, start, MOE_PASS_CHUNKS * MOE_CHUNK)
            return start + MOE_PASS_CHUNKS, left - MOE_PASS_CHUNKS

        first, n = lax.while_loop(lambda st: st[1] > MOE_PASS_CHUNKS, full_pass,
                                  (first, last - first))
        for k in range(1, MOE_PASS_CHUNKS + 1):
            @pl.when(n == k)
            def _(k=k):
                run_group(t, g, first, k * MOE_CHUNK)
        return carry

    lax.fori_loop(0, n_tiles * N_GROUPS, group, 0)

    slot_l = lax.broadcasted_iota(jnp.int32, (MOE_TILE, MOE_SLOTS), 1).astype(F32)
    for t, (r, ps) in enumerate(zip(tiles, pos)):
        unsort = jnp.where(slot_l == ps, 1.0, 0.0).astype(BF16)
        o_ref[r, :] = y_ref[r, :] + _dot(unsort, os_ref[t])


def _moe(y, p):
    n, d = y.shape
    tm = min(MOE_STEP_ROWS, n)
    assert n % tm == 0 and tm % MOE_TILE == 0
    n_tiles = tm // MOE_TILE
    row_spec = pl.BlockSpec((tm, d), lambda i: (i, 0))
    return pl.pallas_call(
        _moe_kernel,
        grid=(n // tm,),
        in_specs=[row_spec, _const_spec((1, d)), _const_spec((d, 2 * ROUTER_LANES)),
                  _const_spec((d, ROUTER_LANES)), _const_spec((1, ROUTER_LANES)),
                  _const_spec((N_EXPERTS, d, D_EXPERT)), _const_spec((N_EXPERTS, d, D_EXPERT)),
                  _const_spec((N_GROUPS, GROUP_HIDDEN, d)),
                  _const_spec((MOE_TILE, MOE_TILE)), _const_spec((LANES, LANES))],
        out_specs=row_spec,
        out_shape=jax.ShapeDtypeStruct((n, d), F32),
        scratch_shapes=[pltpu.VMEM((n_tiles, MOE_SLOTS, d), BF16),
                        pltpu.VMEM((n_tiles, MOE_SLOTS, ROUTER_LANES), F32),
                        pltpu.VMEM((n_tiles, MOE_SLOTS, d), BF16)],
        compiler_params=pltpu.CompilerParams(
            dimension_semantics=("arbitrary",), vmem_limit_bytes=MOE_VMEM_LIMIT),
        name="expert_mixer",
    )(y, p["g2"], p["wr_cat"], p["wr_hi"], p["br"], p["wg"], p["wu"], p["wd"],
      p["ltri"], p["eye"])


def _window_bias(n_q, q_tok, key_prev_ok, key_cur_ok):
    half = np.concatenate([key_prev_ok, key_cur_ok], axis=1)
    ok = np.concatenate([half, half], axis=1)
    return np.where(ok, 0.0, NEG_INF).astype(np.float32)


def _prompt_bias():
    t = np.arange(WINDOW)[:, None]
    s = np.arange(WINDOW)[None, :]
    prev_ok = s > t
    cur_ok = s <= t
    normal = _window_bias(WINDOW, t, prev_ok, cur_ok)
    first = _window_bias(WINDOW, t, np.zeros_like(prev_ok), cur_ok)
    return np.stack([normal, first])


def _sample_bias():
    out = []
    r = np.arange(8)[:, None]
    t = r // 2
    s = np.arange(WINDOW)[None, :]
    prev_ok = s > t
    for hf in range(2):
        c = s
        cur_ok = (c < 8) & (c % 2 == hf) & (c // 2 <= t)
        out.append(_window_bias(8, t, np.broadcast_to(prev_ok, (8, WINDOW)), cur_ok))
    return np.stack(out)


def _block_diag_mean(width):
    idx = np.arange(width) // HEAD_DIM
    return (idx[:, None] == idx[None, :]).astype(np.float32) / HEAD_DIM


def _prepare(norm1_g, w_in, q_norm_g, k_norm_g, attn_sinks, ln_v_g, ln_v_b, w_spatial, b_spatial,
             w_branch_a, w_branch_b, w_out, norm2_g, w_router_group, b_router_group,
             w_router_expert, b_router_expert, w_exp_gate, w_exp_up, w_exp_down):
    p = {}
    p["g1"] = norm1_g.reshape(1, D_MODEL)
    p["w_in"] = w_in.astype(BF16)
    p["gq"] = (jnp.tile(q_norm_g, N_HEADS) * (HEAD_DIM ** -0.5)).reshape(1, ATTN_WIDTH)
    p["gk"] = jnp.tile(k_norm_g, N_KV_HEADS).reshape(1, KV_WIDTH)
    p["bq"] = jnp.asarray(_block_diag_mean(ATTN_WIDTH), BF16)
    p["bk"] = jnp.asarray(_block_diag_mean(KV_WIDTH), BF16)
    p["sinks"] = attn_sinks.astype(F32)
    p["lng"] = ln_v_g.reshape(1, SGU_WIDTH)
    p["lnb"] = ln_v_b.reshape(1, SGU_WIDTH)
    tril = jnp.tril(jnp.ones((CHUNK, CHUNK), F32))
    wsp = w_spatial * tril[None]
    p["wsp"] = wsp.astype(BF16)
    p["bsp"] = jnp.repeat(b_spatial.T, SGU_GROUP_DIM, axis=1)
    w4 = wsp[:, :4, :4]
    p["coef_s"] = jnp.repeat(jnp.transpose(w4, (1, 2, 0)).reshape(16, SGU_GROUPS),
                             SGU_GROUP_DIM, axis=1)
    p["bias_sgu_s"] = jnp.repeat(b_spatial[:, :4].T, SGU_GROUP_DIM, axis=1)
    p["wa"] = w_branch_a.astype(BF16)
    p["wb"] = w_branch_b.astype(BF16)
    p["wout"] = w_out.astype(BF16)
    p["g2"] = norm2_g.reshape(1, D_MODEL)
    wr = jnp.concatenate(
        [w_router_group, w_router_expert,
         jnp.zeros((D_MODEL, ROUTER_LANES - N_GROUPS - N_EXPERTS), F32)], axis=1)
    p["wr_hi"], wr_lo = _split_bf16(wr)
    p["wr_cat"] = jnp.concatenate([p["wr_hi"], wr_lo], axis=1)
    p["br"] = jnp.concatenate(
        [b_router_group, b_router_expert,
         jnp.zeros((ROUTER_LANES - N_GROUPS - N_EXPERTS,), F32)]).reshape(1, ROUTER_LANES)
    p["wg"] = w_exp_gate.astype(BF16)
    p["wu"] = w_exp_up.astype(BF16)
    p["wd"] = w_exp_down.reshape(N_GROUPS, GROUP_HIDDEN, D_MODEL).astype(BF16)
    p["ltri"] = jnp.asarray(np.tril(np.ones((MOE_TILE, MOE_TILE), np.float32), -1), BF16)
    p["eye"] = jnp.asarray(np.eye(LANES, dtype=np.float32), BF16)
    p["bias_prompt"] = jnp.asarray(_prompt_bias())
    p["bias_sample"] = jnp.asarray(_sample_bias())
    return p


def kernel(x_prompt, x_sample, cache_k_win, cache_v_win, norm1_g, w_in, q_norm_g, k_norm_g, attn_sinks, ln_v_g, ln_v_b, w_spatial, b_spatial, w_branch_a, w_branch_b, w_out, norm2_g, w_router_group, b_router_group, w_router_expert, b_router_expert, w_exp_gate, w_exp_up, w_exp_down):
    depth = norm1_g.shape[0]
    assert depth == 1
    batch, seq, d = x_prompt.shape
    dec_batch, dec_seq, _ = x_sample.shape
    assert dec_seq == 4 and d == D_MODEL
    p = _prepare(*(a[0] for a in (
        norm1_g, w_in, q_norm_g, k_norm_g, attn_sinks, ln_v_g, ln_v_b, w_spatial, b_spatial,
        w_branch_a, w_branch_b, w_out, norm2_g, w_router_group, b_router_group,
        w_router_expert, b_router_expert, w_exp_gate, w_exp_up, w_exp_down)))

    y1p, kwin, vwin = _prompt_mixer(x_prompt, p)
    yp = _moe(y1p.reshape(batch * seq, d), p).reshape(batch, seq, d)

    xs = jnp.transpose(x_sample, (1, 0, 2)).reshape(dec_seq * dec_batch, d)
    q, k, v, ob, vs, sga, sgb = _sample_in(xs, p)
    ck = cache_k_win[0].reshape(dec_batch, WINDOW, KV_WIDTH)
    cv = cache_v_win[0].reshape(dec_batch, WINDOW, KV_WIDTH)
    oa, nk, nv = _sample_attn(q, k, v, ck, cv, p)
    y1s = _sample_merge(xs, oa, ob, sga, sgb, p)
    ys = _moe(y1s, p)
    ys = jnp.transpose(ys.reshape(dec_seq, dec_batch, d), (1, 0, 2))
    vs_out = jnp.transpose(vs.reshape(dec_seq, dec_batch, SGU_GROUPS, SGU_GROUP_DIM), (1, 0, 2, 3))

    kv_shape = (WINDOW, N_KV_HEADS, HEAD_DIM)
    return (yp, ys,
            kwin.reshape(1, batch, *kv_shape), vwin.reshape(1, batch, *kv_shape),
            nk.reshape(1, dec_batch, *kv_shape), nv.reshape(1, dec_batch, *kv_shape),
            vs_out[None])
```

```python
import functools

import numpy as np
import jax
import jax.numpy as jnp
from jax import lax
from jax.experimental import pallas as pl
from jax.experimental.pallas import tpu as pltpu

F32 = jnp.float32
BF16 = jnp.bfloat16

D_MODEL = 1024
N_HEADS = 8
N_KV_HEADS = 2
HEAD_DIM = 64
WINDOW = 128
ATTN_WIDTH = N_HEADS * HEAD_DIM
KV_WIDTH = N_KV_HEADS * HEAD_DIM
SGU_GROUPS = 4
SGU_WIDTH = D_MODEL // 2
SGU_GROUP_DIM = SGU_WIDTH // SGU_GROUPS
CHUNK = 128
N_GROUPS = 4
GROUP_BITS = 2
EXPERTS_PER_GROUP = 4
N_EXPERTS = N_GROUPS * EXPERTS_PER_GROUP
D_EXPERT = 256
GROUP_HIDDEN = EXPERTS_PER_GROUP * D_EXPERT
D_IN = ATTN_WIDTH + 2 * KV_WIDTH + 2 * SGU_WIDTH + 2 * D_MODEL
EPS = 1e-6
NEG_INF = -1e30

C_Q = 0
C_K = C_Q + ATTN_WIDTH
C_V = C_K + KV_WIDTH
C_U = C_V + KV_WIDTH
C_VS = C_U + SGU_WIDTH
C_GA = C_VS + SGU_WIDTH
C_GB = C_GA + D_MODEL

LANES = 128
ROUTER_LANES = LANES
EXPERT_LANE0 = N_GROUPS
PROMPT_TILE = 1024
MOE_TILE = 512
MOE_STEP_ROWS = 2 * MOE_TILE
MOE_CHUNK = 32
MOE_PASS_CHUNKS = 8
MOE_MAX_CHUNKS = (MOE_TILE + N_GROUPS * (MOE_CHUNK - 1)) // MOE_CHUNK
MOE_SLOTS = -(-MOE_MAX_CHUNKS * MOE_CHUNK // LANES) * LANES
SLOT_SPLIT = 128
SAMPLE_BATCH_TILE = 8
VMEM_LIMIT = 56 * 1024 * 1024
MOE_VMEM_LIMIT = 62 * 1024 * 1024

_SQRT_2_OVER_PI = np.sqrt(2.0 / np.pi).astype(np.float32)


def _dot(a, b):
    return jnp.dot(a, b, preferred_element_type=F32)


def _dot_nt(a, b):
    return lax.dot_general(a, b, (((1,), (1,)), ((), ())), preferred_element_type=F32)


def _gelu(x):
    cdf = 0.5 * (1.0 + jnp.tanh(_SQRT_2_OVER_PI * (x + 0.044715 * (x * x * x))))
    return x * cdf


def _rms(x, g):
    return x * lax.rsqrt(jnp.mean(x * x, axis=-1, keepdims=True) + EPS) * g


def _head_rms(x, g):
    first = lax.broadcasted_iota(jnp.int32, (x.shape[0], LANES), 1) < HEAD_DIM
    ms = []
    for sq in _lane_slabs(x * x):
        total = jnp.sum(sq, axis=-1, keepdims=True)
        diff = jnp.sum(jnp.where(first, sq, -sq), axis=-1, keepdims=True)
        ms.append(jnp.where(first, total + diff, total - diff) * (0.5 / HEAD_DIM))
    return x * lax.rsqrt(jnp.concatenate(ms, axis=1) + EPS) * g


def _qkv(x, g1, win_ref, gq, gk):
    h = _rms(x, g1).astype(BF16)
    qn = _head_rms(_dot(h, win_ref[:, C_Q:C_K]), gq)
    kv = _dot(h, win_ref[:, C_K:C_U])
    kn = _head_rms(kv[:, :KV_WIDTH], gk)
    v = kv[:, KV_WIDTH:]
    return h, qn, kn, v


def _sgu_inputs(h, win_ref, lng, lnb):
    gu = _gelu(_dot(h, win_ref[:, C_U:C_VS]))
    gv = _gelu(_dot(h, win_ref[:, C_VS:C_GA]))
    mu = jnp.mean(gv, axis=-1, keepdims=True)
    xc = gv - mu
    vsn = xc * lax.rsqrt(jnp.mean(xc * xc, axis=-1, keepdims=True) + EPS) * lng + lnb
    return gu, vsn


def _head_variants(prev, prev_rot, cur, cur_rot):
    a = jnp.concatenate([prev, cur], axis=0)
    r = jnp.concatenate([prev_rot, cur_rot], axis=0)
    lo = lax.broadcasted_iota(jnp.int32, a.shape, 1) < HEAD_DIM
    zero = jnp.zeros_like(a)
    kv0 = jnp.concatenate([jnp.where(lo, a, zero), jnp.where(lo, zero, r)], axis=0)
    kv1 = jnp.concatenate([jnp.where(lo, r, zero), jnp.where(lo, zero, a)], axis=0)
    return kv0, kv1


def _lane_slabs(x):
    return [x[:, p * LANES:(p + 1) * LANES] for p in range(x.shape[1] // LANES)]


def _scores(q_slabs, kcat, bias):
    slabs_per_kv = len(q_slabs) // N_KV_HEADS
    m = q_slabs[0].shape[0]
    out = []
    for kv in range(N_KV_HEADS):
        q = jnp.concatenate(q_slabs[kv * slabs_per_kv:(kv + 1) * slabs_per_kv], axis=0)
        s = _dot_nt(q.astype(BF16), kcat[kv])
        out += [s[i * m:(i + 1) * m] + bias for i in range(slabs_per_kv)]
    return out


def _sink_softmax(scores, sinks_ref):
    out = []
    for p, s in enumerate(scores):
        n_keys = s.shape[1] // 2
        probs = []
        for par in range(2):
            sh = s[:, par * n_keys:(par + 1) * n_keys]
            sink = sinks_ref[2 * p + par]
            m = jnp.maximum(jnp.max(sh, axis=-1, keepdims=True), sink)
            e = jnp.exp(sh - m)
            den = jnp.sum(e, axis=-1, keepdims=True) + jnp.exp(sink - m)
            probs.append((e / den).astype(BF16))
        out.append(jnp.concatenate(probs, axis=1))
    return out


def _weighted_values(probs, vcat):
    slabs_per_kv = len(probs) // N_KV_HEADS
    m = probs[0].shape[0]
    out = []
    for kv in range(N_KV_HEADS):
        pr = jnp.concatenate(probs[kv * slabs_per_kv:(kv + 1) * slabs_per_kv], axis=0)
        o = _dot(pr, vcat[kv])
        out += [o[i * m:(i + 1) * m] for i in range(slabs_per_kv)]
    return out


def _gates(h, win_ref):
    return (jax.nn.sigmoid(_dot(h, win_ref[:, C_GA:C_GB])),
            jax.nn.sigmoid(_dot(h, win_ref[:, C_GB:D_IN])))


def _merge(x, sga, sgb, oa, ob, wa_ref, wb_ref, wout_ref):
    ya = _dot(oa.astype(BF16), wa_ref[...])
    yb = _dot(ob.astype(BF16), wb_ref[...])
    hm = sga * ya + sgb * yb
    return x + _dot(hm.astype(BF16), wout_ref[...])


def _prompt_kernel(x_ref, g1_ref, win_ref, gq_ref, gk_ref, sinks_ref, lng_ref,
                   lnb_ref, wsp_ref, bsp_ref, wa_ref, wb_ref, wout_ref, bias_ref,
                   y_ref, kwin_ref, vwin_ref,
                   kprev, kprev_rot, vprev, vprev_rot):
    j = pl.program_id(1)

    @pl.when(j == 0)
    def _():
        for r in (kprev, kprev_rot, vprev, vprev_rot):
            r[...] = jnp.zeros_like(r)

    x = x_ref[0]
    h, qn, kn, v = _qkv(x, g1_ref[...], win_ref, gq_ref[...], gk_ref[...])

    kb = kn.astype(BF16)
    kr = pltpu.roll(kn, HEAD_DIM, 1).astype(BF16)
    vb = v.astype(BF16)
    vr = pltpu.roll(v, HEAD_DIM, 1).astype(BF16)

    n_blocks = x.shape[0] // WINDOW
    blocks = [slice(i * WINDOW, (i + 1) * WINDOW) for i in range(n_blocks)]
    scores, vcats = [], []
    pk, pkr, pv, pvr = kprev[...], kprev_rot[...], vprev[...], vprev_rot[...]
    for i, rows in enumerate(blocks):
        ck, ckr, cv, cvr = kb[rows], kr[rows], vb[rows], vr[rows]
        kcat = _head_variants(pk, pkr, ck, ckr)
        vcats.append(_head_variants(pv, pvr, cv, cvr))
        bias = bias_ref[jnp.where(j == 0, 1, 0)] if i == 0 else bias_ref[0]
        scores.append(_scores(_lane_slabs(qn[rows]), kcat, bias))
        pk, pkr, pv, pvr = ck, ckr, cv, cvr
    kprev[...] = pk
    kprev_rot[...] = pkr
    vprev[...] = pv
    vprev_rot[...] = pvr

    gu, vsn = _sgu_inputs(h, win_ref, lng_ref[...], lnb_ref[...])
    vsb = vsn.astype(BF16)
    mixed = [_dot(wsp_ref[g], jnp.concatenate(
        [vsb[rows, g * SGU_GROUP_DIM:(g + 1) * SGU_GROUP_DIM] for rows in blocks], axis=1))
        for g in range(SGU_GROUPS)]
    ob = jnp.concatenate([
        gu[rows] * (jnp.concatenate(
            [m[:, i * SGU_GROUP_DIM:(i + 1) * SGU_GROUP_DIM] for m in mixed], axis=1) + bsp_ref[...])
        for i, rows in enumerate(blocks)], axis=0)

    sga, sgb = _gates(h, win_ref)

    probs = [_sink_softmax(s, sinks_ref) for s in scores]
    oa = jnp.concatenate(
        [jnp.concatenate(_weighted_values(pr, vc), axis=1) for pr, vc in zip(probs, vcats)], axis=0)
    y_ref[0] = _merge(x, sga, sgb, oa, ob, wa_ref, wb_ref, wout_ref)

    @pl.when(j == pl.num_programs(1) - 1)
    def _():
        kwin_ref[0] = kn[(n_blocks - 1) * WINDOW:]
        vwin_ref[0] = v[(n_blocks - 1) * WINDOW:]


def _const_spec(shape, single_buffer=True):
    nd = len(shape)
    mode = pl.Buffered(1) if single_buffer else None
    return pl.BlockSpec(shape, lambda *_: (0,) * nd, pipeline_mode=mode)


def _smem_spec():
    return pl.BlockSpec(memory_space=pltpu.SMEM)


def _prompt_mixer(x, p):
    b, s, d = x.shape
    ts = PROMPT_TILE
    assert s % ts == 0 and ts % WINDOW == 0
    in_specs = [
        pl.BlockSpec((1, ts, d), lambda bi, j: (bi, j, 0)),
        _const_spec((1, d)), _const_spec((d, D_IN)),
        _const_spec((1, ATTN_WIDTH)), _const_spec((1, KV_WIDTH)),
        _smem_spec(),
        _const_spec((1, SGU_WIDTH)), _const_spec((1, SGU_WIDTH)),
        _const_spec((SGU_GROUPS, CHUNK, CHUNK)), _const_spec((CHUNK, SGU_WIDTH)),
        _const_spec((ATTN_WIDTH, d)), _const_spec((SGU_WIDTH, d)), _const_spec((d, d)),
        _const_spec((2, WINDOW, 4 * WINDOW)),
    ]
    out_specs = [
        pl.BlockSpec((1, ts, d), lambda bi, j: (bi, j, 0)),
        pl.BlockSpec((1, WINDOW, KV_WIDTH), lambda bi, j: (bi, 0, 0)),
        pl.BlockSpec((1, WINDOW, KV_WIDTH), lambda bi, j: (bi, 0, 0)),
    ]
    out_shape = [
        jax.ShapeDtypeStruct((b, s, d), F32),
        jax.ShapeDtypeStruct((b, WINDOW, KV_WIDTH), F32),
        jax.ShapeDtypeStruct((b, WINDOW, KV_WIDTH), F32),
    ]
    return pl.pallas_call(
        _prompt_kernel,
        grid=(b, s // ts),
        in_specs=in_specs, out_specs=out_specs, out_shape=out_shape,
        scratch_shapes=[pltpu.VMEM((WINDOW, KV_WIDTH), BF16)] * 4,
        compiler_params=pltpu.CompilerParams(
            dimension_semantics=("arbitrary", "arbitrary"), vmem_limit_bytes=VMEM_LIMIT),
        name="prompt_mixer",
    )(x, p["g1"], p["w_in"], p["gq"], p["gk"], p["sinks"], p["lng"], p["lnb"],
      p["wsp"], p["bsp"], p["wa"], p["wb"], p["wout"], p["bias_prompt"])


def _sample_in_kernel(x_ref, g1_ref, win_ref, gq_ref, gk_ref, lng_ref, lnb_ref,
                      coef_ref, sbias_ref,
                      q_ref, k_ref, v_ref, ob_ref, vs_ref, sga_ref, sgb_ref):
    x = x_ref[...]
    h, qn, kn, v = _qkv(x, g1_ref[...], win_ref, gq_ref[...], gk_ref[...])
    gu, vsn = _sgu_inputs(h, win_ref, lng_ref[...], lnb_ref[...])
    for p, slab in enumerate(_lane_slabs(qn)):
        q_ref[p] = slab
    k_ref[...] = kn
    v_ref[...] = v
    vs_ref[...] = vsn
    sga_ref[...], sgb_ref[...] = _gates(h, win_ref)
    nb = x.shape[0] // 4
    for t in range(4):
        mixed = sbias_ref[t:t + 1, :]
        for jj in range(t + 1):
            mixed = mixed + coef_ref[4 * t + jj:4 * t + jj + 1, :] * vsn[jj * nb:(jj + 1) * nb]
        ob_ref[t * nb:(t + 1) * nb, :] = gu[t * nb:(t + 1) * nb] * mixed


def _sample_in(x, p):
    n, d = x.shape
    shapes = [(ATTN_WIDTH // LANES, n, LANES), (n, KV_WIDTH), (n, KV_WIDTH), (n, SGU_WIDTH),
              (n, SGU_WIDTH), (n, d), (n, d)]
    return pl.pallas_call(
        _sample_in_kernel,
        grid=(1,),
        in_specs=[_const_spec((n, d)), _const_spec((1, d)), _const_spec((d, D_IN)),
                  _const_spec((1, ATTN_WIDTH)), _const_spec((1, KV_WIDTH)),
                  _const_spec((1, SGU_WIDTH)), _const_spec((1, SGU_WIDTH)),
                  _const_spec((16, SGU_WIDTH)), _const_spec((4, SGU_WIDTH))],
        out_specs=[_const_spec(sh, False) for sh in shapes],
        out_shape=[jax.ShapeDtypeStruct(sh, F32) for sh in shapes],
        compiler_params=pltpu.CompilerParams(
            dimension_semantics=("arbitrary",), vmem_limit_bytes=VMEM_LIMIT),
        name="sample_in",
    )(x, p["g1"], p["w_in"], p["gq"], p["gk"], p["lng"], p["lnb"],
      p["coef_s"], p["bias_sgu_s"])


def _sample_attn_kernel(q_ref, k_ref, v_ref, ck_ref, cv_ref, sinks_ref, bias_ref,
                        oa_ref, nk_ref, nv_ref):
    i = pl.program_id(0)
    n_slabs = q_ref.shape[0]
    half_rows = q_ref.shape[1] // 8
    pad = jnp.zeros((WINDOW - 8, KV_WIDTH), BF16)
    row_half = lax.broadcasted_iota(jnp.int32, (8, LANES), 0) % 2

    work = []
    for bb in range(SAMPLE_BATCH_TILE):
        b_lo = i * SAMPLE_BATCH_TILE + bb
        rows = pl.ds(b_lo, 8, stride=half_rows)
        q8 = [q_ref[p, rows, :] for p in range(n_slabs)]
        k8 = k_ref[rows, :]
        v8 = v_ref[rows, :]
        ck_new = jnp.concatenate([k8.astype(BF16), pad], axis=0)
        ckr_new = jnp.concatenate([pltpu.roll(k8, HEAD_DIM, 1).astype(BF16), pad], axis=0)
        cv_new = jnp.concatenate([v8.astype(BF16), pad], axis=0)
        cvr_new = jnp.concatenate([pltpu.roll(v8, HEAD_DIM, 1).astype(BF16), pad], axis=0)
        for hf in range(2):
            kc = ck_ref[hf, bb]
            vc = cv_ref[hf, bb]
            kcat = _head_variants(kc.astype(BF16), pltpu.roll(kc, HEAD_DIM, 1).astype(BF16),
                                  ck_new, ckr_new)
            vcat = _head_variants(vc.astype(BF16), pltpu.roll(vc, HEAD_DIM, 1).astype(BF16),
                                  cv_new, cvr_new)
            work.append((rows, hf, _scores(q8, kcat, bias_ref[hf]), vcat))
            nk_ref[hf, bb] = pltpu.roll(kc, WINDOW - 4, 0)
            nv_ref[hf, bb] = pltpu.roll(vc, WINDOW - 4, 0)
            for t in range(4):
                nk_ref[hf, bb, WINDOW - 4 + t:WINDOW - 3 + t, :] = k8[2 * t + hf:2 * t + hf + 1, :]
                nv_ref[hf, bb, WINDOW - 4 + t:WINDOW - 3 + t, :] = v8[2 * t + hf:2 * t + hf + 1, :]
    probs = [_sink_softmax(s, sinks_ref) for _, _, s, _ in work]
    outs = [_weighted_values(pr, w[3]) for pr, w in zip(probs, work)]
    for (rows, _, _, _), o_even, o_odd in zip(work[0::2], outs[0::2], outs[1::2]):
        for p in range(n_slabs):
            oa_ref[p, rows, :] = jnp.where(row_half == 0, o_even[p], o_odd[p])


def _sample_attn(q, k, v, ck, cv, p):
    n_slabs, n, _ = q.shape
    nb = n // 4
    half = nb // 2
    bt = SAMPLE_BATCH_TILE
    assert half % bt == 0
    ck4 = ck.reshape(2, half, WINDOW, KV_WIDTH)
    cv4 = cv.reshape(2, half, WINDOW, KV_WIDTH)
    cache_spec = pl.BlockSpec((2, bt, WINDOW, KV_WIDTH), lambda i: (0, i, 0, 0))
    return pl.pallas_call(
        _sample_attn_kernel,
        grid=(half // bt,),
        in_specs=[_const_spec((n_slabs, n, LANES)), _const_spec((n, KV_WIDTH)),
                  _const_spec((n, KV_WIDTH)), cache_spec, cache_spec, _smem_spec(),
                  _const_spec((2, 8, 4 * WINDOW))],
        out_specs=[_const_spec((n_slabs, n, LANES), False), cache_spec, cache_spec],
        out_shape=[jax.ShapeDtypeStruct((n_slabs, n, LANES), F32),
                   jax.ShapeDtypeStruct(ck4.shape, F32), jax.ShapeDtypeStruct(cv4.shape, F32)],
        compiler_params=pltpu.CompilerParams(
            dimension_semantics=("arbitrary",), vmem_limit_bytes=VMEM_LIMIT),
        name="sample_attn",
    )(q, k, v, ck4, cv4, p["sinks"], p["bias_sample"])


def _sample_merge_kernel(x_ref, oa_ref, ob_ref, sga_ref, sgb_ref, wa_ref, wb_ref, wout_ref, y_ref):
    oa = jnp.concatenate([oa_ref[p] for p in range(oa_ref.shape[0])], axis=1)
    y_ref[...] = _merge(x_ref[...], sga_ref[...], sgb_ref[...], oa, ob_ref[...],
                        wa_ref, wb_ref, wout_ref)


def _sample_merge(x, oa, ob, sga, sgb, p):
    n, d = x.shape
    return pl.pallas_call(
        _sample_merge_kernel,
        grid=(1,),
        in_specs=[_const_spec((n, d)), _const_spec(oa.shape), _const_spec((n, SGU_WIDTH)),
                  _const_spec((n, d)), _const_spec((n, d)),
                  _const_spec((ATTN_WIDTH, d)), _const_spec((SGU_WIDTH, d)), _const_spec((d, d))],
        out_specs=_const_spec((n, d), False),
        out_shape=jax.ShapeDtypeStruct((n, d), F32),
        compiler_params=pltpu.CompilerParams(
            dimension_semantics=("arbitrary",), vmem_limit_bytes=VMEM_LIMIT),
        name="sample_merge",
    )(x, oa, ob, sga, sgb, p["wa"], p["wb"], p["wout"])


def _route(logits):
    lane = lax.broadcasted_iota(jnp.int32, logits.shape, 1).astype(F32)
    far = float(ROUTER_LANES)
    glm = jnp.where(lane < N_GROUPS, logits, NEG_INF)
    gmax = jnp.max(glm, axis=-1, keepdims=True)
    gidx = jnp.min(jnp.where(glm == gmax, lane, far), axis=-1, keepdims=True)
    gw = 1.0 / jnp.sum(jnp.exp(glm - gmax), axis=-1, keepdims=True)
    first = EXPERT_LANE0 + EXPERTS_PER_GROUP * gidx
    sel = (lane >= first) & (lane < first + EXPERTS_PER_GROUP)
    el = jnp.where(sel, logits, NEG_INF)
    t1 = jnp.max(el, axis=-1, keepdims=True)
    i1 = jnp.min(jnp.where(el == t1, lane, far), axis=-1, keepdims=True)
    el2 = jnp.where(lane == i1, NEG_INF, el)
    t2 = jnp.max(el2, axis=-1, keepdims=True)
    i2 = jnp.min(jnp.where(el2 == t2, lane, far), axis=-1, keepdims=True)
    e2 = jnp.exp(t2 - t1)
    den = 1.0 + e2
    w1 = (1.0 / den) * gw
    w2 = (e2 / den) * gw
    return gidx, jnp.where(lane == i1 - first, w1, 0.0) + jnp.where(lane == i2 - first, w2, 0.0)


def _split_bf16(x):
    hi = x.astype(BF16)
    lo = (x - hi.astype(F32)).astype(BF16)
    return hi, lo


def _moe_kernel(y_ref, g2_ref, wrc_ref, wrh_ref, br_ref, wg_ref, wu_ref, wd_ref,
                ltri_ref, eye_ref, o_ref, hs_ref, cws_ref, os_ref):
    n_tiles = y_ref.shape[0] // MOE_TILE
    tiles = [slice(t * MOE_TILE, (t + 1) * MOE_TILE) for t in range(n_tiles)]

    @pl.when(pl.program_id(0) == 0)
    def _():
        os_ref[...] = jnp.zeros_like(os_ref)

    h_split = [_split_bf16(_rms(y_ref[r, :], g2_ref[...])) for r in tiles]
    part = [_dot(hb, wrc_ref[...]) for hb, _ in h_split]
    low = [_dot(hl, wrh_ref[...]) for _, hl in h_split]
    routed = [_route(pt[:, :ROUTER_LANES] + (pt[:, ROUTER_LANES:] + lw) + br_ref[...])
              for pt, lw in zip(part, low)]

    lane = lax.broadcasted_iota(jnp.int32, (MOE_TILE, ROUTER_LANES), 1).astype(F32)
    lane1 = lane[0:1]
    onehot = [jnp.where(lane == gidx, 1.0, 0.0) for gidx, _ in routed]
    before = [_dot(ltri_ref[...], oh.astype(BF16)) for oh in onehot]
    ends, pos = [], []
    for oh, bf in zip(onehot, before):
        count = jnp.sum(oh, axis=0, keepdims=True)
        n_chunks = jnp.floor((count + (MOE_CHUNK - 1)) * (1.0 / MOE_CHUNK))
        tile_ends = []
        start_vec = jnp.zeros_like(n_chunks)
        end = jnp.zeros((1, 1), F32)
        for g in range(N_GROUPS):
            start_vec = start_vec + jnp.where(lane1 == g, end * MOE_CHUNK, 0.0)
            end = end + jnp.sum(jnp.where(lane1 == g, n_chunks, 0.0), axis=-1, keepdims=True)
            tile_ends.append(end[0, 0].astype(jnp.int32))
        ends.append(tile_ends)
        pos.append(jnp.sum(oh * (bf + start_vec), axis=-1, keepdims=True))

    pos_t = []
    for ps in pos:
        pos_hi = jnp.floor(ps * (1.0 / SLOT_SPLIT))
        pos_lo = ps - pos_hi * SLOT_SPLIT
        cols = jnp.where(lane == 0.0, pos_hi, jnp.where(lane == 1.0, pos_lo, 0.0)).astype(BF16)
        pos_t.append(_dot_nt(eye_ref[...], cols))
    slot_s = lax.broadcasted_iota(jnp.int32, (MOE_SLOTS, MOE_TILE), 0).astype(F32)
    for t, (pt, (hb, _), (_, cw)) in enumerate(zip(pos_t, h_split, routed)):
        pos_row = pt[0:1] * SLOT_SPLIT + pt[1:2]
        sort = jnp.where(slot_s == pos_row, 1.0, 0.0).astype(BF16)
        cw_hi = cw.astype(BF16).astype(F32)
        cw_parts = (cw_hi + pltpu.roll(cw - cw_hi, LANES // 2, 1)).astype(BF16)
        sorted_rows = _dot(sort, jnp.concatenate([hb, cw_parts], axis=1))
        hs_ref[t] = sorted_rows[:, :D_MODEL].astype(BF16)
        parts = sorted_rows[:, D_MODEL:]
        cws_ref[t] = parts + pltpu.roll(parts, LANES // 2, 1)

    def run_group(t, g, first_chunk, n_rows):
        rows = pl.ds(pl.multiple_of(first_chunk * MOE_CHUNK, MOE_CHUNK), n_rows)
        hsc = hs_ref[t, rows, :]
        cwc = cws_ref[t, rows, :]
        cexp = jnp.concatenate([jnp.broadcast_to(cwc[:, e:e + 1], (n_rows, D_EXPERT))
                                for e in range(EXPERTS_PER_GROUP)], axis=1)
        experts = [EXPERTS_PER_GROUP * g + e for e in range(EXPERTS_PER_GROUP)]
        a = jnp.concatenate([_dot(hsc, wg_ref[e]) for e in experts], axis=1)
        u = jnp.concatenate([_dot(hsc, wu_ref[e]) for e in experts], axis=1)
        hid = (a * jax.nn.sigmoid(a)) * u * cexp
        os_ref[t, rows, :] = _dot(hid.astype(BF16), wd_ref[g]).astype(BF16)

    def group(i, carry):
        t = lax.shift_right_logical(i, GROUP_BITS)
        g = i & (N_GROUPS - 1)
        first, last = jnp.int32(0), jnp.int32(0)
        for tt in range(n_tiles):
            for gg in range(N_GROUPS):
                here = (t == tt) & (g == gg)
                first = jnp.where(here, ends[tt][gg - 1] if gg else 0, first)
                last = jnp.where(here, ends[tt][gg], last)
        def full_pass(state):
            start, left = state
            run_group(t, g, start, MOE_PASS_CHUNKS * MOE_CHUNK)
            return start + MOE_PASS_CHUNKS, left - MOE_PASS_CHUNKS

        first, n = lax.while_loop(lambda st: st[1] > MOE_PASS_CHUNKS, full_pass,
                                  (first, last - first))
        for k in range(1, MOE_PASS_CHUNKS + 1):
            @pl.when(n == k)
            def _(k=k):
                run_group(t, g, first, k * MOE_CHUNK)
        return carry

    lax.fori_loop(0, n_tiles * N_GROUPS, group, 0)

    slot_l = lax.broadcasted_iota(jnp.int32, (MOE_TILE, MOE_SLOTS), 1).astype(F32)
    for t, (r, ps) in enumerate(zip(tiles, pos)):
        unsort = jnp.where(slot_l == ps, 1.0, 0.0).astype(BF16)
        o_ref[r, :] = y_ref[r, :] + _dot(unsort, os_ref[t])


def _moe(y, p):
    n, d = y.shape
    tm = min(MOE_STEP_ROWS, n)
    assert n % tm == 0 and tm % MOE_TILE == 0
    n_tiles = tm // MOE_TILE
    row_spec = pl.BlockSpec((tm, d), lambda i: (i, 0))
    return pl.pallas_call(
        _moe_kernel,
        grid=(n // tm,),
        in_specs=[row_spec, _const_spec((1, d)), _const_spec((d, 2 * ROUTER_LANES)),
                  _const_spec((d, ROUTER_LANES)), _const_spec((1, ROUTER_LANES)),
                  _const_spec((N_EXPERTS, d, D_EXPERT)), _const_spec((N_EXPERTS, d, D_EXPERT)),
                  _const_spec((N_GROUPS, GROUP_HIDDEN, d)),
                  _const_spec((MOE_TILE, MOE_TILE)), _const_spec((LANES, LANES))],
        out_specs=row_spec,
        out_shape=jax.ShapeDtypeStruct((n, d), F32),
        scratch_shapes=[pltpu.VMEM((n_tiles, MOE_SLOTS, d), BF16),
                        pltpu.VMEM((n_tiles, MOE_SLOTS, ROUTER_LANES), F32),
                        pltpu.VMEM((n_tiles, MOE_SLOTS, d), BF16)],
        compiler_params=pltpu.CompilerParams(
            dimension_semantics=("arbitrary",), vmem_limit_bytes=MOE_VMEM_LIMIT),
        name="expert_mixer",
    )(y, p["g2"], p["wr_cat"], p["wr_hi"], p["br"], p["wg"], p["wu"], p["wd"],
      p["ltri"], p["eye"])


def _window_bias(n_q, q_tok, key_prev_ok, key_cur_ok):
    half = np.concatenate([key_prev_ok, key_cur_ok], axis=1)
    ok = np.concatenate([half, half], axis=1)
    return np.where(ok, 0.0, NEG_INF).astype(np.float32)


def _prompt_bias():
    t = np.arange(WINDOW)[:, None]
    s = np.arange(WINDOW)[None, :]
    prev_ok = s > t
    cur_ok = s <= t
    normal = _window_bias(WINDOW, t, prev_ok, cur_ok)
    first = _window_bias(WINDOW, t, np.zeros_like(prev_ok), cur_ok)
    return np.stack([normal, first])


def _sample_bias():
    out = []
    r = np.arange(8)[:, None]
    t = r // 2
    s = np.arange(WINDOW)[None, :]
    prev_ok = s > t
    for hf in range(2):
        c = s
        cur_ok = (c < 8) & (c % 2 == hf) & (c // 2 <= t)
        out.append(_window_bias(8, t, np.broadcast_to(prev_ok, (8, WINDOW)), cur_ok))
    return np.stack(out)


def _prepare(norm1_g, w_in, q_norm_g, k_norm_g, attn_sinks, ln_v_g, ln_v_b, w_spatial, b_spatial,
             w_branch_a, w_branch_b, w_out, norm2_g, w_router_group, b_router_group,
             w_router_expert, b_router_expert, w_exp_gate, w_exp_up, w_exp_down):
    p = {}
    p["g1"] = norm1_g.reshape(1, D_MODEL)
    p["w_in"] = w_in.astype(BF16)
    p["gq"] = (jnp.tile(q_norm_g, N_HEADS) * (HEAD_DIM ** -0.5)).reshape(1, ATTN_WIDTH)
    p["gk"] = jnp.tile(k_norm_g, N_KV_HEADS).reshape(1, KV_WIDTH)
    p["sinks"] = attn_sinks.astype(F32)
    p["lng"] = ln_v_g.reshape(1, SGU_WIDTH)
    p["lnb"] = ln_v_b.reshape(1, SGU_WIDTH)
    tril = jnp.tril(jnp.ones((CHUNK, CHUNK), F32))
    wsp = w_spatial * tril[None]
    p["wsp"] = wsp.astype(BF16)
    p["bsp"] = jnp.repeat(b_spatial.T, SGU_GROUP_DIM, axis=1)
    w4 = wsp[:, :4, :4]
    p["coef_s"] = jnp.repeat(jnp.transpose(w4, (1, 2, 0)).reshape(16, SGU_GROUPS),
                             SGU_GROUP_DIM, axis=1)
    p["bias_sgu_s"] = jnp.repeat(b_spatial[:, :4].T, SGU_GROUP_DIM, axis=1)
    p["wa"] = w_branch_a.astype(BF16)
    p["wb"] = w_branch_b.astype(BF16)
    p["wout"] = w_out.astype(BF16)
    p["g2"] = norm2_g.reshape(1, D_MODEL)
    wr = jnp.concatenate(
        [w_router_group, w_router_expert,
         jnp.zeros((D_MODEL, ROUTER_LANES - N_GROUPS - N_EXPERTS), F32)], axis=1)
    p["wr_hi"], wr_lo = _split_bf16(wr)
    p["wr_cat"] = jnp.concatenate([p["wr_hi"], wr_lo], axis=1)
    p["br"] = jnp.concatenate(
        [b_router_group, b_router_expert,
         jnp.zeros((ROUTER_LANES - N_GROUPS - N_EXPERTS,), F32)]).reshape(1, ROUTER_LANES)
    p["wg"] = w_exp_gate.astype(BF16)
    p["wu"] = w_exp_up.astype(BF16)
    p["wd"] = w_exp_down.reshape(N_GROUPS, GROUP_HIDDEN, D_MODEL).astype(BF16)
    p["ltri"] = jnp.asarray(np.tril(np.ones((MOE_TILE, MOE_TILE), np.float32), -1), BF16)
    p["eye"] = jnp.asarray(np.eye(LANES, dtype=np.float32), BF16)
    p["bias_prompt"] = jnp.asarray(_prompt_bias())
    p["bias_sample"] = jnp.asarray(_sample_bias())
    return p


def kernel(x_prompt, x_sample, cache_k_win, cache_v_win, norm1_g, w_in, q_norm_g, k_norm_g, attn_sinks, ln_v_g, ln_v_b, w_spatial, b_spatial, w_branch_a, w_branch_b, w_out, norm2_g, w_router_group, b_router_group, w_router_expert, b_router_expert, w_exp_gate, w_exp_up, w_exp_down):
    depth = norm1_g.shape[0]
    assert depth == 1
    batch, seq, d = x_prompt.shape
    dec_batch, dec_seq, _ = x_sample.shape
    assert dec_seq == 4 and d == D_MODEL
    p = _prepare(*(a[0] for a in (
        norm1_g, w_in, q_norm_g, k_norm_g, attn_sinks, ln_v_g, ln_v_b, w_spatial, b_spatial,
        w_branch_a, w_branch_b, w_out, norm2_g, w_router_group, b_router_group,
        w_router_expert, b_router_expert, w_exp_gate, w_exp_up, w_exp_down)))

    y1p, kwin, vwin = _prompt_mixer(x_prompt, p)
    yp = _moe(y1p.reshape(batch * seq, d), p).reshape(batch, seq, d)

    xs = jnp.transpose(x_sample, (1, 0, 2)).reshape(dec_seq * dec_batch, d)
    q, k, v, ob, vs, sga, sgb = _sample_in(xs, p)
    ck = cache_k_win[0].reshape(dec_batch, WINDOW, KV_WIDTH)
    cv = cache_v_win[0].reshape(dec_batch, WINDOW, KV_WIDTH)
    oa, nk, nv = _sample_attn(q, k, v, ck, cv, p)
    y1s = _sample_merge(xs, oa, ob, sga, sgb, p)
    ys = _moe(y1s, p)
    ys = jnp.transpose(ys.reshape(dec_seq, dec_batch, d), (1, 0, 2))
    vs_out = jnp.transpose(vs.reshape(dec_seq, dec_batch, SGU_GROUPS, SGU_GROUP_DIM), (1, 0, 2, 3))

    kv_shape = (WINDOW, N_KV_HEADS, HEAD_DIM)
    return (yp, ys,
            kwin.reshape(1, batch, *kv_shape), vwin.reshape(1, batch, *kv_shape),
            nk.reshape(1, dec_batch, *kv_shape), nv.reshape(1, dec_batch, *kv_shape),
            vs_out[None])
```

```python
import numpy as np
import jax
import jax.numpy as jnp
from jax import lax
from jax.experimental import pallas as pl
from jax.experimental.pallas import tpu as pltpu

F32 = jnp.float32
BF16 = jnp.bfloat16

D_MODEL = 1024
N_HEADS = 8
N_KV_HEADS = 2
HEAD_DIM = 64
WINDOW = 128
ATTN_WIDTH = N_HEADS * HEAD_DIM
KV_WIDTH = N_KV_HEADS * HEAD_DIM
SGU_GROUPS = 4
SGU_WIDTH = D_MODEL // 2
SGU_GROUP_DIM = SGU_WIDTH // SGU_GROUPS
CHUNK = 128
N_GROUPS = 4
GROUP_BITS = 2
EXPERTS_PER_GROUP = 4
N_EXPERTS = N_GROUPS * EXPERTS_PER_GROUP
D_EXPERT = 256
GROUP_HIDDEN = EXPERTS_PER_GROUP * D_EXPERT
D_IN = ATTN_WIDTH + 2 * KV_WIDTH + 2 * SGU_WIDTH + 2 * D_MODEL
EPS = 1e-6
NEG_INF = -1e30

C_Q = 0
C_K = C_Q + ATTN_WIDTH
C_V = C_K + KV_WIDTH
C_U = C_V + KV_WIDTH
C_VS = C_U + SGU_WIDTH
C_GA = C_VS + SGU_WIDTH
C_GB = C_GA + D_MODEL

LANES = 128
ROUTER_LANES = LANES
EXPERT_LANE0 = N_GROUPS
PROMPT_TILE = 512
MOE_TILE = 512
MOE_STEP_ROWS = 2 * MOE_TILE
MOE_CHUNK = 16
MOE_PASS_CHUNKS = 12
MOE_MAX_CHUNKS = (MOE_TILE + N_GROUPS * (MOE_CHUNK - 1)) // MOE_CHUNK
MOE_SLOTS = -(-MOE_MAX_CHUNKS * MOE_CHUNK // LANES) * LANES
SLOT_SPLIT = 128
SAMPLE_BATCH_TILE = 8
VMEM_LIMIT = 56 * 1024 * 1024
MOE_VMEM_LIMIT = 62 * 1024 * 1024

_SQRT_2_OVER_PI = np.sqrt(2.0 / np.pi).astype(np.float32)


def _dot(a, b):
    return jnp.dot(a, b, preferred_element_type=F32)


def _dot_nt(a, b):
    return lax.dot_general(a, b, (((1,), (1,)), ((), ())), preferred_element_type=F32)


def _gelu(x):
    cdf = 0.5 * (1.0 + jnp.tanh(_SQRT_2_OVER_PI * (x + 0.044715 * (x * x * x))))
    return x * cdf


def _rms(x, g):
    return x * lax.rsqrt(jnp.mean(x * x, axis=-1, keepdims=True) + EPS) * g


def _head_rms(x, g):
    first = lax.broadcasted_iota(jnp.int32, (x.shape[0], LANES), 1) < HEAD_DIM
    ms = []
    for sq in _lane_slabs(x * x):
        total = jnp.sum(sq, axis=-1, keepdims=True)
        diff = jnp.sum(jnp.where(first, sq, -sq), axis=-1, keepdims=True)
        ms.append(jnp.where(first, total + diff, total - diff) * (0.5 / HEAD_DIM))
    return x * lax.rsqrt(jnp.concatenate(ms, axis=1) + EPS) * g


def _qkv(x, g1, win_ref, gq, gk):
    h = _rms(x, g1).astype(BF16)
    qn = _head_rms(_dot(h, win_ref[:, C_Q:C_K]), gq)
    kv = _dot(h, win_ref[:, C_K:C_U])
    kn = _head_rms(kv[:, :KV_WIDTH], gk)
    v = kv[:, KV_WIDTH:]
    return h, qn, kn, v


def _sgu_inputs(h, win_ref, lng, lnb):
    gu = _gelu(_dot(h, win_ref[:, C_U:C_VS]))
    gv = _gelu(_dot(h, win_ref[:, C_VS:C_GA]))
    mu = jnp.mean(gv, axis=-1, keepdims=True)
    xc = gv - mu
    vsn = xc * lax.rsqrt(jnp.mean(xc * xc, axis=-1, keepdims=True) + EPS) * lng + lnb
    return gu, vsn


def _head_variants(prev, prev_rot, cur, cur_rot):
    a = jnp.concatenate([prev, cur], axis=0)
    r = jnp.concatenate([prev_rot, cur_rot], axis=0)
    lo = lax.broadcasted_iota(jnp.int32, a.shape, 1) < HEAD_DIM
    zero = jnp.zeros_like(a)
    kv0 = jnp.concatenate([jnp.where(lo, a, zero), jnp.where(lo, zero, r)], axis=0)
    kv1 = jnp.concatenate([jnp.where(lo, r, zero), jnp.where(lo, zero, a)], axis=0)
    return kv0, kv1


def _lane_slabs(x):
    return [x[:, p * LANES:(p + 1) * LANES] for p in range(x.shape[1] // LANES)]


def _scores(q_slabs, kcat, bias):
    slabs_per_kv = len(q_slabs) // N_KV_HEADS
    m = q_slabs[0].shape[0]
    out = []
    for kv in range(N_KV_HEADS):
        q = jnp.concatenate(q_slabs[kv * slabs_per_kv:(kv + 1) * slabs_per_kv], axis=0)
        s = _dot_nt(q.astype(BF16), kcat[kv])
        out += [s[i * m:(i + 1) * m] + bias for i in range(slabs_per_kv)]
    return out


def _sink_softmax(scores, sinks_ref):
    out = []
    for p, s in enumerate(scores):
        n_keys = s.shape[1] // 2
        probs = []
        for par in range(2):
            sh = s[:, par * n_keys:(par + 1) * n_keys]
            sink = sinks_ref[2 * p + par]
            m = jnp.maximum(jnp.max(sh, axis=-1, keepdims=True), sink)
            e = jnp.exp(sh - m)
            den = jnp.sum(e, axis=-1, keepdims=True) + jnp.exp(sink - m)
            probs.append((e / den).astype(BF16))
        out.append(jnp.concatenate(probs, axis=1))
    return out


def _weighted_values(probs, vcat):
    slabs_per_kv = len(probs) // N_KV_HEADS
    m = probs[0].shape[0]
    out = []
    for kv in range(N_KV_HEADS):
        pr = jnp.concatenate(probs[kv * slabs_per_kv:(kv + 1) * slabs_per_kv], axis=0)
        o = _dot(pr, vcat[kv])
        out += [o[i * m:(i + 1) * m] for i in range(slabs_per_kv)]
    return out


def _gates(h, win_ref):
    return (jax.nn.sigmoid(_dot(h, win_ref[:, C_GA:C_GB])),
            jax.nn.sigmoid(_dot(h, win_ref[:, C_GB:D_IN])))


def _merge(x, sga, sgb, oa, ob, wa_ref, wb_ref, wout_ref):
    ya = _dot(oa.astype(BF16), wa_ref[...])
    yb = _dot(ob.astype(BF16), wb_ref[...])
    hm = sga * ya + sgb * yb
    return x + _dot(hm.astype(BF16), wout_ref[...])


def _prompt_kernel(x_ref, g1_ref, win_ref, gq_ref, gk_ref, sinks_ref, lng_ref,
                   lnb_ref, wsp_ref, bsp_ref, wa_ref, wb_ref, wout_ref, bias_ref,
                   y_ref, kwin_ref, vwin_ref,
                   kprev, kprev_rot, vprev, vprev_rot):
    j = pl.program_id(1)

    @pl.when(j == 0)
    def _():
        for r in (kprev, kprev_rot, vprev, vprev_rot):
            r[...] = jnp.zeros_like(r)

    x = x_ref[0]
    h, qn, kn, v = _qkv(x, g1_ref[...], win_ref, gq_ref[...], gk_ref[...])

    kb = kn.astype(BF16)
    kr = pltpu.roll(kn, HEAD_DIM, 1).astype(BF16)
    vb = v.astype(BF16)
    vr = pltpu.roll(v, HEAD_DIM, 1).astype(BF16)

    n_blocks = x.shape[0] // WINDOW
    blocks = [slice(i * WINDOW, (i + 1) * WINDOW) for i in range(n_blocks)]
    scores, vcats = [], []
    pk, pkr, pv, pvr = kprev[...], kprev_rot[...], vprev[...], vprev_rot[...]
    for i, rows in enumerate(blocks):
        ck, ckr, cv, cvr = kb[rows], kr[rows], vb[rows], vr[rows]
        kcat = _head_variants(pk, pkr, ck, ckr)
        vcats.append(_head_variants(pv, pvr, cv, cvr))
        bias = bias_ref[jnp.where(j == 0, 1, 0)] if i == 0 else bias_ref[0]
        scores.append(_scores(_lane_slabs(qn[rows]), kcat, bias))
        pk, pkr, pv, pvr = ck, ckr, cv, cvr
    kprev[...] = pk
    kprev_rot[...] = pkr
    vprev[...] = pv
    vprev_rot[...] = pvr

    gu, vsn = _sgu_inputs(h, win_ref, lng_ref[...], lnb_ref[...])
    vsb = vsn.astype(BF16)
    mixed = [_dot(wsp_ref[g], jnp.concatenate(
        [vsb[rows, g * SGU_GROUP_DIM:(g + 1) * SGU_GROUP_DIM] for rows in blocks], axis=1))
        for g in range(SGU_GROUPS)]
    ob = jnp.concatenate([
        gu[rows] * (jnp.concatenate(
            [m[:, i * SGU_GROUP_DIM:(i + 1) * SGU_GROUP_DIM] for m in mixed], axis=1) + bsp_ref[...])
        for i, rows in enumerate(blocks)], axis=0)

    sga, sgb = _gates(h, win_ref)

    probs = [_sink_softmax(s, sinks_ref) for s in scores]
    oa = jnp.concatenate(
        [jnp.concatenate(_weighted_values(pr, vc), axis=1) for pr, vc in zip(probs, vcats)], axis=0)
    y_ref[0] = _merge(x, sga, sgb, oa, ob, wa_ref, wb_ref, wout_ref)

    @pl.when(j == pl.num_programs(1) - 1)
    def _():
        kwin_ref[0] = kn[(n_blocks - 1) * WINDOW:]
        vwin_ref[0] = v[(n_blocks - 1) * WINDOW:]


def _const_spec(shape, single_buffer=True):
    nd = len(shape)
    mode = pl.Buffered(1) if single_buffer else None
    return pl.BlockSpec(shape, lambda *_: (0,) * nd, pipeline_mode=mode)


def _smem_spec():
    return pl.BlockSpec(memory_space=pltpu.SMEM)


def _prompt_mixer(x, p):
    b, s, d = x.shape
    ts = PROMPT_TILE
    assert s % ts == 0 and ts % WINDOW == 0
    in_specs = [
        pl.BlockSpec((1, ts, d), lambda bi, j: (bi, j, 0)),
        _const_spec((1, d)), _const_spec((d, D_IN)),
        _const_spec((1, ATTN_WIDTH)), _const_spec((1, KV_WIDTH)),
        _smem_spec(),
        _const_spec((1, SGU_WIDTH)), _const_spec((1, SGU_WIDTH)),
        _const_spec((SGU_GROUPS, CHUNK, CHUNK)), _const_spec((CHUNK, SGU_WIDTH)),
        _const_spec((ATTN_WIDTH, d)), _const_spec((SGU_WIDTH, d)), _const_spec((d, d)),
        _const_spec((2, WINDOW, 4 * WINDOW)),
    ]
    out_specs = [
        pl.BlockSpec((1, ts, d), lambda bi, j: (bi, j, 0)),
        pl.BlockSpec((1, WINDOW, KV_WIDTH), lambda bi, j: (bi, 0, 0)),
        pl.BlockSpec((1, WINDOW, KV_WIDTH), lambda bi, j: (bi, 0, 0)),
    ]
    out_shape = [
        jax.ShapeDtypeStruct((b, s, d), F32),
        jax.ShapeDtypeStruct((b, WINDOW, KV_WIDTH), F32),
        jax.ShapeDtypeStruct((b, WINDOW, KV_WIDTH), F32),
    ]
    return pl.pallas_call(
        _prompt_kernel,
        grid=(b, s // ts),
        in_specs=in_specs, out_specs=out_specs, out_shape=out_shape,
        scratch_shapes=[pltpu.VMEM((WINDOW, KV_WIDTH), BF16)] * 4,
        compiler_params=pltpu.CompilerParams(
            dimension_semantics=("arbitrary", "arbitrary"), vmem_limit_bytes=VMEM_LIMIT),
        name="prompt_mixer",
    )(x, p["g1"], p["w_in"], p["gq"], p["gk"], p["sinks"], p["lng"], p["lnb"],
      p["wsp"], p["bsp"], p["wa"], p["wb"], p["wout"], p["bias_prompt"])


def _sample_in_kernel(x_ref, g1_ref, win_ref, gq_ref, gk_ref, lng_ref, lnb_ref,
                      coef_ref, sbias_ref,
                      q_ref, k_ref, v_ref, ob_ref, vs_ref, sga_ref, sgb_ref):
    x = x_ref[...]
    h, qn, kn, v = _qkv(x, g1_ref[...], win_ref, gq_ref[...], gk_ref[...])
    gu, vsn = _sgu_inputs(h, win_ref, lng_ref[...], lnb_ref[...])
    for p, slab in enumerate(_lane_slabs(qn)):
        q_ref[p] = slab
    k_ref[...] = kn
    v_ref[...] = v
    vs_ref[...] = vsn
    sga_ref[...], sgb_ref[...] = _gates(h, win_ref)
    nb = x.shape[0] // 4
    for t in range(4):
        mixed = sbias_ref[t:t + 1, :]
        for jj in range(t + 1):
            mixed = mixed + coef_ref[4 * t + jj:4 * t + jj + 1, :] * vsn[jj * nb:(jj + 1) * nb]
        ob_ref[t * nb:(t + 1) * nb, :] = gu[t * nb:(t + 1) * nb] * mixed


def _sample_in(x, p):
    n, d = x.shape
    shapes = [(ATTN_WIDTH // LANES, n, LANES), (n, KV_WIDTH), (n, KV_WIDTH), (n, SGU_WIDTH),
              (n, SGU_WIDTH), (n, d), (n, d)]
    return pl.pallas_call(
        _sample_in_kernel,
        grid=(1,),
        in_specs=[_const_spec((n, d)), _const_spec((1, d)), _const_spec((d, D_IN)),
                  _const_spec((1, ATTN_WIDTH)), _const_spec((1, KV_WIDTH)),
                  _const_spec((1, SGU_WIDTH)), _const_spec((1, SGU_WIDTH)),
                  _const_spec((16, SGU_WIDTH)), _const_spec((4, SGU_WIDTH))],
        out_specs=[_const_spec(sh, False) for sh in shapes],
        out_shape=[jax.ShapeDtypeStruct(sh, F32) for sh in shapes],
        compiler_params=pltpu.CompilerParams(
            dimension_semantics=("arbitrary",), vmem_limit_bytes=VMEM_LIMIT),
        name="sample_in",
    )(x, p["g1"], p["w_in"], p["gq"], p["gk"], p["lng"], p["lnb"],
      p["coef_s"], p["bias_sgu_s"])


def _sample_attn_kernel(q_ref, k_ref, v_ref, ck_ref, cv_ref, sinks_ref, bias_ref,
                        oa_ref, nk_ref, nv_ref):
    i = pl.program_id(0)
    n_slabs = q_ref.shape[0]
    half_rows = q_ref.shape[1] // 8
    pad = jnp.zeros((WINDOW - 8, KV_WIDTH), BF16)
    row_half = lax.broadcasted_iota(jnp.int32, (8, LANES), 0) % 2

    work = []
    for bb in range(SAMPLE_BATCH_TILE):
        b_lo = i * SAMPLE_BATCH_TILE + bb
        rows = pl.ds(b_lo, 8, stride=half_rows)
        q8 = [q_ref[p, rows, :] for p in range(n_slabs)]
        k8 = k_ref[rows, :]
        v8 = v_ref[rows, :]
        ck_new = jnp.concatenate([k8.astype(BF16), pad], axis=0)
        ckr_new = jnp.concatenate([pltpu.roll(k8, HEAD_DIM, 1).astype(BF16), pad], axis=0)
        cv_new = jnp.concatenate([v8.astype(BF16), pad], axis=0)
        cvr_new = jnp.concatenate([pltpu.roll(v8, HEAD_DIM, 1).astype(BF16), pad], axis=0)
        for hf in range(2):
            kc = ck_ref[hf, bb]
            vc = cv_ref[hf, bb]
            kcat = _head_variants(kc.astype(BF16), pltpu.roll(kc, HEAD_DIM, 1).astype(BF16),
                                  ck_new, ckr_new)
            vcat = _head_variants(vc.astype(BF16), pltpu.roll(vc, HEAD_DIM, 1).astype(BF16),
                                  cv_new, cvr_new)
            work.append((rows, hf, _scores(q8, kcat, bias_ref[hf]), vcat))
            nk_ref[hf, bb] = pltpu.roll(kc, WINDOW - 4, 0)
            nv_ref[hf, bb] = pltpu.roll(vc, WINDOW - 4, 0)
            for t in range(4):
                nk_ref[hf, bb, WINDOW - 4 + t:WINDOW - 3 + t, :] = k8[2 * t + hf:2 * t + hf + 1, :]
                nv_ref[hf, bb, WINDOW - 4 + t:WINDOW - 3 + t, :] = v8[2 * t + hf:2 * t + hf + 1, :]
    probs = [_sink_softmax(s, sinks_ref) for _, _, s, _ in work]
    outs = [_weighted_values(pr, w[3]) for pr, w in zip(probs, work)]
    for (rows, _, _, _), o_even, o_odd in zip(work[0::2], outs[0::2], outs[1::2]):
        for p in range(n_slabs):
            oa_ref[p, rows, :] = jnp.where(row_half == 0, o_even[p], o_odd[p])


def _sample_attn(q, k, v, ck, cv, p):
    n_slabs, n, _ = q.shape
    nb = n // 4
    half = nb // 2
    bt = SAMPLE_BATCH_TILE
    assert half % bt == 0
    ck4 = ck.reshape(2, half, WINDOW, KV_WIDTH)
    cv4 = cv.reshape(2, half, WINDOW, KV_WIDTH)
    cache_spec = pl.BlockSpec((2, bt, WINDOW, KV_WIDTH), lambda i: (0, i, 0, 0))
    return pl.pallas_call(
        _sample_attn_kernel,
        grid=(half // bt,),
        in_specs=[_const_spec((n_slabs, n, LANES)), _const_spec((n, KV_WIDTH)),
                  _const_spec((n, KV_WIDTH)), cache_spec, cache_spec, _smem_spec(),
                  _const_spec((2, 8, 4 * WINDOW))],
        out_specs=[_const_spec((n_slabs, n, LANES), False), cache_spec, cache_spec],
        out_shape=[jax.ShapeDtypeStruct((n_slabs, n, LANES), F32),
                   jax.ShapeDtypeStruct(ck4.shape, F32), jax.ShapeDtypeStruct(cv4.shape, F32)],
        compiler_params=pltpu.CompilerParams(
            dimension_semantics=("arbitrary",), vmem_limit_bytes=VMEM_LIMIT),
        name="sample_attn",
    )(q, k, v, ck4, cv4, p["sinks"], p["bias_sample"])


def _sample_merge_kernel(x_ref, oa_ref, ob_ref, sga_ref, sgb_ref, wa_ref, wb_ref, wout_ref, y_ref):
    oa = jnp.concatenate([oa_ref[p] for p in range(oa_ref.shape[0])], axis=1)
    y_ref[...] = _merge(x_ref[...], sga_ref[...], sgb_ref[...], oa, ob_ref[...],
                        wa_ref, wb_ref, wout_ref)


def _sample_merge(x, oa, ob, sga, sgb, p):
    n, d = x.shape
    return pl.pallas_call(
        _sample_merge_kernel,
        grid=(1,),
        in_specs=[_const_spec((n, d)), _const_spec(oa.shape), _const_spec((n, SGU_WIDTH)),
                  _const_spec((n, d)), _const_spec((n, d)),
                  _const_spec((ATTN_WIDTH, d)), _const_spec((SGU_WIDTH, d)), _const_spec((d, d))],
        out_specs=_const_spec((n, d), False),
        out_shape=jax.ShapeDtypeStruct((n, d), F32),
        compiler_params=pltpu.CompilerParams(
            dimension_semantics=("arbitrary",), vmem_limit_bytes=VMEM_LIMIT),
        name="sample_merge",
    )(x, oa, ob, sga, sgb, p["wa"], p["wb"], p["wout"])


def _route(logits):
    lane = lax.broadcasted_iota(jnp.int32, logits.shape, 1).astype(F32)
    far = float(ROUTER_LANES)
    glm = jnp.where(lane < N_GROUPS, logits, NEG_INF)
    gmax = jnp.max(glm, axis=-1, keepdims=True)
    gidx = jnp.min(jnp.where(glm == gmax, lane, far), axis=-1, keepdims=True)
    gw = 1.0 / jnp.sum(jnp.exp(glm - gmax), axis=-1, keepdims=True)
    first = EXPERT_LANE0 + EXPERTS_PER_GROUP * gidx
    sel = (lane >= first) & (lane < first + EXPERTS_PER_GROUP)
    el = jnp.where(sel, logits, NEG_INF)
    t1 = jnp.max(el, axis=-1, keepdims=True)
    i1 = jnp.min(jnp.where(el == t1, lane, far), axis=-1, keepdims=True)
    el2 = jnp.where(lane == i1, NEG_INF, el)
    t2 = jnp.max(el2, axis=-1, keepdims=True)
    i2 = jnp.min(jnp.where(el2 == t2, lane, far), axis=-1, keepdims=True)
    e2 = jnp.exp(t2 - t1)
    den = 1.0 + e2
    w1 = (1.0 / den) * gw
    w2 = (e2 / den) * gw
    return gidx, jnp.where(lane == i1 - first, w1, 0.0) + jnp.where(lane == i2 - first, w2, 0.0)


def _split_bf16(x):
    hi = x.astype(BF16)
    lo = (x - hi.astype(F32)).astype(BF16)
    return hi, lo


def _moe_kernel(y_ref, g2_ref, wrc_ref, wrh_ref, br_ref, wg_ref, wu_ref, wd_ref,
                ltri_ref, eye_ref, o_ref, hs_ref, cws_ref, os_ref):
    n_tiles = y_ref.shape[0] // MOE_TILE
    tiles = [slice(t * MOE_TILE, (t + 1) * MOE_TILE) for t in range(n_tiles)]

    @pl.when(pl.program_id(0) == 0)
    def _():
        os_ref[...] = jnp.zeros_like(os_ref)

    h_split = [_split_bf16(_rms(y_ref[r, :], g2_ref[...])) for r in tiles]
    part = [_dot(hb, wrc_ref[...]) for hb, _ in h_split]
    low = [_dot(hl, wrh_ref[...]) for _, hl in h_split]
    routed = [_route(pt[:, :ROUTER_LANES] + (pt[:, ROUTER_LANES:] + lw) + br_ref[...])
              for pt, lw in zip(part, low)]

    lane = lax.broadcasted_iota(jnp.int32, (MOE_TILE, ROUTER_LANES), 1).astype(F32)
    lane1 = lane[0:1]
    onehot = [jnp.where(lane == gidx, 1.0, 0.0) for gidx, _ in routed]
    before = [_dot(ltri_ref[...], oh.astype(BF16)) for oh in onehot]
    ends, pos = [], []
    for oh, bf in zip(onehot, before):
        count = jnp.sum(oh, axis=0, keepdims=True)
        n_chunks = jnp.floor((count + (MOE_CHUNK - 1)) * (1.0 / MOE_CHUNK))
        tile_ends = []
        start_vec = jnp.zeros_like(n_chunks)
        end = jnp.zeros((1, 1), F32)
        for g in range(N_GROUPS):
            start_vec = start_vec + jnp.where(lane1 == g, end * MOE_CHUNK, 0.0)
            end = end + jnp.sum(jnp.where(lane1 == g, n_chunks, 0.0), axis=-1, keepdims=True)
            tile_ends.append(end[0, 0].astype(jnp.int32))
        ends.append(tile_ends)
        pos.append(jnp.sum(oh * (bf + start_vec), axis=-1, keepdims=True))

    pos_t = []
    for ps in pos:
        pos_hi = jnp.floor(ps * (1.0 / SLOT_SPLIT))
        pos_lo = ps - pos_hi * SLOT_SPLIT
        cols = jnp.where(lane == 0.0, pos_hi, jnp.where(lane == 1.0, pos_lo, 0.0)).astype(BF16)
        pos_t.append(_dot_nt(eye_ref[...], cols))
    slot_s = lax.broadcasted_iota(jnp.int32, (MOE_SLOTS, MOE_TILE), 0).astype(F32)
    for t, (pt, (hb, _), (_, cw)) in enumerate(zip(pos_t, h_split, routed)):
        pos_row = pt[0:1] * SLOT_SPLIT + pt[1:2]
        sort = jnp.where(slot_s == pos_row, 1.0, 0.0).astype(BF16)
        cw_hi = cw.astype(BF16).astype(F32)
        cw_parts = (cw_hi + pltpu.roll(cw - cw_hi, LANES // 2, 1)).astype(BF16)
        sorted_rows = _dot(sort, jnp.concatenate([hb, cw_parts], axis=1))
        hs_ref[t] = sorted_rows[:, :D_MODEL].astype(BF16)
        parts = sorted_rows[:, D_MODEL:]
        cws_ref[t] = parts + pltpu.roll(parts, LANES // 2, 1)

    def run_group(t, g, first_chunk, n_rows):
        rows = pl.ds(pl.multiple_of(first_chunk * MOE_CHUNK, MOE_CHUNK), n_rows)
        hsc = hs_ref[t, rows, :]
        cwc = cws_ref[t, rows, :]
        cexp = jnp.concatenate([jnp.broadcast_to(cwc[:, e:e + 1], (n_rows, D_EXPERT))
                                for e in range(EXPERTS_PER_GROUP)], axis=1)
        experts = [EXPERTS_PER_GROUP * g + e for e in range(EXPERTS_PER_GROUP)]
        a = jnp.concatenate([_dot(hsc, wg_ref[e]) for e in experts], axis=1)
        u = jnp.concatenate([_dot(hsc, wu_ref[e]) for e in experts], axis=1)
        hid = (a * jax.nn.sigmoid(a)) * u * cexp
        os_ref[t, rows, :] = _dot(hid.astype(BF16), wd_ref[g]).astype(BF16)

    def group(i, carry):
        t = lax.shift_right_logical(i, GROUP_BITS)
        g = i & (N_GROUPS - 1)
        first, last = jnp.int32(0), jnp.int32(0)
        for tt in range(n_tiles):
            for gg in range(N_GROUPS):
                here = (t == tt) & (g == gg)
                first = jnp.where(here, ends[tt][gg - 1] if gg else 0, first)
                last = jnp.where(here, ends[tt][gg], last)
        def full_pass(state):
            start, left = state
            run_group(t, g, start, MOE_PASS_CHUNKS * MOE_CHUNK)
            return start + MOE_PASS_CHUNKS, left - MOE_PASS_CHUNKS

        first, n = lax.while_loop(lambda st: st[1] > MOE_PASS_CHUNKS, full_pass,
                                  (first, last - first))
        for k in range(1, MOE_PASS_CHUNKS + 1):
            @pl.when(n == k)
            def _(k=k):
                run_group(t, g, first, k * MOE_CHUNK)
        return carry

    lax.fori_loop(0, n_tiles * N_GROUPS, group, 0)

    slot_l = lax.broadcasted_iota(jnp.int32, (MOE_TILE, MOE_SLOTS), 1).astype(F32)
    for t, (r, ps) in enumerate(zip(tiles, pos)):
        unsort = jnp.where(slot_l == ps, 1.0, 0.0).astype(BF16)
        o_ref[r, :] = y_ref[r, :] + _dot(unsort, os_ref[t])


def _moe(y, p):
    n, d = y.shape
    tm = min(MOE_STEP_ROWS, n)
    assert n % tm == 0 and tm % MOE_TILE == 0
    n_tiles = tm // MOE_TILE
    row_spec = pl.BlockSpec((tm, d), lambda i: (i, 0))
    return pl.pallas_call(
        _moe_kernel,
        grid=(n // tm,),
        in_specs=[row_spec, _const_spec((1, d)), _const_spec((d, 2 * ROUTER_LANES)),
                  _const_spec((d, ROUTER_LANES)), _const_spec((1, ROUTER_LANES)),
                  _const_spec((N_EXPERTS, d, D_EXPERT)), _const_spec((N_EXPERTS, d, D_EXPERT)),
                  _const_spec((N_GROUPS, GROUP_HIDDEN, d)),
                  _const_spec((MOE_TILE, MOE_TILE)), _const_spec((LANES, LANES))],
        out_specs=row_spec,
        out_shape=jax.ShapeDtypeStruct((n, d), F32),
        scratch_shapes=[pltpu.VMEM((n_tiles, MOE_SLOTS, d), BF16),
                        pltpu.VMEM((n_tiles, MOE_SLOTS, ROUTER_LANES), F32),
                        pltpu.VMEM((n_tiles, MOE_SLOTS, d), BF16)],
        compiler_params=pltpu.CompilerParams(
            dimension_semantics=("arbitrary",), vmem_limit_bytes=MOE_VMEM_LIMIT),
        name="expert_mixer",
    )(y, p["g2"], p["wr_cat"], p["wr_hi"], p["br"], p["wg"], p["wu"], p["wd"],
      p["ltri"], p["eye"])


def _window_bias(n_q, q_tok, key_prev_ok, key_cur_ok):
    half = np.concatenate([key_prev_ok, key_cur_ok], axis=1)
    ok = np.concatenate([half, half], axis=1)
    return np.where(ok, 0.0, NEG_INF).astype(np.float32)


def _prompt_bias():
    t = np.arange(WINDOW)[:, None]
    s = np.arange(WINDOW)[None, :]
    prev_ok = s > t
    cur_ok = s <= t
    normal = _window_bias(WINDOW, t, prev_ok, cur_ok)
    first = _window_bias(WINDOW, t, np.zeros_like(prev_ok), cur_ok)
    return np.stack([normal, first])


def _sample_bias():
    out = []
    r = np.arange(8)[:, None]
    t = r // 2
    s = np.arange(WINDOW)[None, :]
    prev_ok = s > t
    for hf in range(2):
        c = s
        cur_ok = (c < 8) & (c % 2 == hf) & (c // 2 <= t)
        out.append(_window_bias(8, t, np.broadcast_to(prev_ok, (8, WINDOW)), cur_ok))
    return np.stack(out)


def _prepare(norm1_g, w_in, q_norm_g, k_norm_g, attn_sinks, ln_v_g, ln_v_b, w_spatial, b_spatial,
             w_branch_a, w_branch_b, w_out, norm2_g, w_router_group, b_router_group,
             w_router_expert, b_router_expert, w_exp_gate, w_exp_up, w_exp_down):
    p = {}
    p["g1"] = norm1_g.reshape(1, D_MODEL)
    p["w_in"] = w_in.astype(BF16)
    p["gq"] = (jnp.tile(q_norm_g, N_HEADS) * (HEAD_DIM ** -0.5)).reshape(1, ATTN_WIDTH)
    p["gk"] = jnp.tile(k_norm_g, N_KV_HEADS).reshape(1, KV_WIDTH)
    p["sinks"] = attn_sinks.astype(F32)
    p["lng"] = ln_v_g.reshape(1, SGU_WIDTH)
    p["lnb"] = ln_v_b.reshape(1, SGU_WIDTH)
    tril = jnp.tril(jnp.ones((CHUNK, CHUNK), F32))
    wsp = w_spatial * tril[None]
    p["wsp"] = wsp.astype(BF16)
    p["bsp"] = jnp.repeat(b_spatial.T, SGU_GROUP_DIM, axis=1)
    w4 = wsp[:, :4, :4]
    p["coef_s"] = jnp.repeat(jnp.transpose(w4, (1, 2, 0)).reshape(16, SGU_GROUPS),
                             SGU_GROUP_DIM, axis=1)
    p["bias_sgu_s"] = jnp.repeat(b_spatial[:, :4].T, SGU_GROUP_DIM, axis=1)
    p["wa"] = w_branch_a.astype(BF16)
    p["wb"] = w_branch_b.astype(BF16)
    p["wout"] = w_out.astype(BF16)
    p["g2"] = norm2_g.reshape(1, D_MODEL)
    wr = jnp.concatenate(
        [w_router_group, w_router_expert,
         jnp.zeros((D_MODEL, ROUTER_LANES - N_GROUPS - N_EXPERTS), F32)], axis=1)
    p["wr_hi"], wr_lo = _split_bf16(wr)
    p["wr_cat"] = jnp.concatenate([p["wr_hi"], wr_lo], axis=1)
    p["br"] = jnp.concatenate(
        [b_router_group, b_router_expert,
         jnp.zeros((ROUTER_LANES - N_GROUPS - N_EXPERTS,), F32)]).reshape(1, ROUTER_LANES)
    p["wg"] = w_exp_gate.astype(BF16)
    p["wu"] = w_exp_up.astype(BF16)
    p["wd"] = w_exp_down.reshape(N_GROUPS, GROUP_HIDDEN, D_MODEL).astype(BF16)
    p["ltri"] = jnp.asarray(np.tril(np.ones((MOE_TILE, MOE_TILE), np.float32), -1), BF16)
    p["eye"] = jnp.asarray(np.eye(LANES, dtype=np.float32), BF16)
    p["bias_prompt"] = jnp.asarray(_prompt_bias())
    p["bias_sample"] = jnp.asarray(_sample_bias())
    return p


def kernel(x_prompt, x_sample, cache_k_win, cache_v_win, norm1_g, w_in, q_norm_g, k_norm_g, attn_sinks, ln_v_g, ln_v_b, w_spatial, b_spatial, w_branch_a, w_branch_b, w_out, norm2_g, w_router_group, b_router_group, w_router_expert, b_router_expert, w_exp_gate, w_exp_up, w_exp_down):
    depth = norm1_g.shape[0]
    assert depth == 1
    batch, seq, d = x_prompt.shape
    dec_batch, dec_seq, _ = x_sample.shape
    assert dec_seq == 4 and d == D_MODEL
    p = _prepare(*(a[0] for a in (
        norm1_g, w_in, q_norm_g, k_norm_g, attn_sinks, ln_v_g, ln_v_b, w_spatial, b_spatial,
        w_branch_a, w_branch_b, w_out, norm2_g, w_router_group, b_router_group,
        w_router_expert, b_router_expert, w_exp_gate, w_exp_up, w_exp_down)))

    y1p, kwin, vwin = _prompt_mixer(x_prompt, p)
    yp = _moe(y1p.reshape(batch * seq, d), p).reshape(batch, seq, d)

    xs = jnp.transpose(x_sample, (1, 0, 2)).reshape(dec_seq * dec_batch, d)
    q, k, v, ob, vs, sga, sgb = _sample_in(xs, p)
    ck = cache_k_win[0].reshape(dec_batch, WINDOW, KV_WIDTH)
    cv = cache_v_win[0].reshape(dec_batch, WINDOW, KV_WIDTH)
    oa, nk, nv = _sample_attn(q, k, v, ck, cv, p)
    y1s = _sample_merge(xs, oa, ob, sga, sgb, p)
    ys = _moe(y1s, p)
    ys = jnp.transpose(ys.reshape(dec_seq, dec_batch, d), (1, 0, 2))
    vs_out = jnp.transpose(vs.reshape(dec_seq, dec_batch, SGU_GROUPS, SGU_GROUP_DIM), (1, 0, 2, 3))

    kv_shape = (WINDOW, N_KV_HEADS, HEAD_DIM)
    return (yp, ys,
            kwin.reshape(1, batch, *kv_shape), vwin.reshape(1, batch, *kv_shape),
            nk.reshape(1, dec_batch, *kv_shape), nv.reshape(1, dec_batch, *kv_shape),
            vs_out[None])
```

```python
import numpy as np
import jax
import jax.numpy as jnp
from jax import lax
from jax.experimental import pallas as pl
from jax.experimental.pallas import tpu as pltpu

F32 = jnp.float32
BF16 = jnp.bfloat16

D_MODEL = 1024
N_HEADS = 8
N_KV_HEADS = 2
HEAD_DIM = 64
WINDOW = 128
ATTN_WIDTH = N_HEADS * HEAD_DIM
KV_WIDTH = N_KV_HEADS * HEAD_DIM
SGU_GROUPS = 4
SGU_WIDTH = D_MODEL // 2
SGU_GROUP_DIM = SGU_WIDTH // SGU_GROUPS
CHUNK = 128
N_GROUPS = 4
GROUP_BITS = 2
EXPERTS_PER_GROUP = 4
N_EXPERTS = N_GROUPS * EXPERTS_PER_GROUP
D_EXPERT = 256
GROUP_HIDDEN = EXPERTS_PER_GROUP * D_EXPERT
D_IN = ATTN_WIDTH + 2 * KV_WIDTH + 2 * SGU_WIDTH + 2 * D_MODEL
EPS = 1e-6
NEG_INF = -1e30

C_Q = 0
C_K = C_Q + ATTN_WIDTH
C_V = C_K + KV_WIDTH
C_U = C_V + KV_WIDTH
C_VS = C_U + SGU_WIDTH
C_GA = C_VS + SGU_WIDTH
C_GB = C_GA + D_MODEL

LANES = 128
ROUTER_LANES = LANES
EXPERT_LANE0 = N_GROUPS
PROMPT_TILE = 1024
MOE_TILE = 512
MOE_STEP_ROWS = 2 * MOE_TILE
MOE_CHUNK = 32
MOE_PASS_CHUNKS = 8
MOE_MAX_CHUNKS = (MOE_TILE + N_GROUPS * (MOE_CHUNK - 1)) // MOE_CHUNK
MOE_SLOTS = -(-MOE_MAX_CHUNKS * MOE_CHUNK // LANES) * LANES
SLOT_SPLIT = 128
SAMPLE_BATCH_TILE = 8
VMEM_LIMIT = 56 * 1024 * 1024
MOE_VMEM_LIMIT = 62 * 1024 * 1024

_SQRT_2_OVER_PI = np.sqrt(2.0 / np.pi).astype(np.float32)


def _dot(a, b):
    return jnp.dot(a, b, preferred_element_type=F32)


def _dot_nt(a, b):
    return lax.dot_general(a, b, (((1,), (1,)), ((), ())), preferred_element_type=F32)


def _gelu(x):
    cdf = 0.5 * (1.0 + jnp.tanh(_SQRT_2_OVER_PI * (x + 0.044715 * (x * x * x))))
    return x * cdf


def _rms(x, g):
    return x * lax.rsqrt(jnp.mean(x * x, axis=-1, keepdims=True) + EPS) * g


def _head_rms(x, blockdiag, g):
    ms = _dot((x * x).astype(BF16), blockdiag)
    return x * lax.rsqrt(ms + EPS) * g


def _qkv(x, g1, win_ref, gq, gk, bq, bk):
    h = _rms(x, g1).astype(BF16)
    qn = _head_rms(_dot(h, win_ref[:, C_Q:C_K]), bq, gq)
    kv = _dot(h, win_ref[:, C_K:C_U])
    kn = _head_rms(kv[:, :KV_WIDTH], bk, gk)
    v = kv[:, KV_WIDTH:]
    return h, qn, kn, v


def _sgu_inputs(h, win_ref, lng, lnb):
    gu = _gelu(_dot(h, win_ref[:, C_U:C_VS]))
    gv = _gelu(_dot(h, win_ref[:, C_VS:C_GA]))
    mu = jnp.mean(gv, axis=-1, keepdims=True)
    xc = gv - mu
    vsn = xc * lax.rsqrt(jnp.mean(xc * xc, axis=-1, keepdims=True) + EPS) * lng + lnb
    return gu, vsn


def _head_variants(prev, prev_rot, cur, cur_rot):
    a = jnp.concatenate([prev, cur], axis=0)
    r = jnp.concatenate([prev_rot, cur_rot], axis=0)
    lo = lax.broadcasted_iota(jnp.int32, a.shape, 1) < HEAD_DIM
    zero = jnp.zeros_like(a)
    kv0 = jnp.concatenate([jnp.where(lo, a, zero), jnp.where(lo, zero, r)], axis=0)
    kv1 = jnp.concatenate([jnp.where(lo, r, zero), jnp.where(lo, zero, a)], axis=0)
    return kv0, kv1


def _lane_slabs(x):
    return [x[:, p * LANES:(p + 1) * LANES] for p in range(x.shape[1] // LANES)]


def _scores(q_slabs, kcat, bias):
    slabs_per_kv = len(q_slabs) // N_KV_HEADS
    m = q_slabs[0].shape[0]
    out = []
    for kv in range(N_KV_HEADS):
        q = jnp.concatenate(q_slabs[kv * slabs_per_kv:(kv + 1) * slabs_per_kv], axis=0)
        s = _dot_nt(q.astype(BF16), kcat[kv])
        out += [s[i * m:(i + 1) * m] + bias for i in range(slabs_per_kv)]
    return out


def _sink_softmax(scores, sinks_ref):
    out = []
    for p, s in enumerate(scores):
        n_keys = s.shape[1] // 2
        weights, recips = [], []
        for par in range(2):
            sh = s[:, par * n_keys:(par + 1) * n_keys]
            sink = sinks_ref[2 * p + par]
            m = jnp.maximum(jnp.max(sh, axis=-1, keepdims=True), sink)
            e = jnp.exp(sh - m)
            recips.append(1.0 / (jnp.sum(e, axis=-1, keepdims=True) + jnp.exp(sink - m)))
            weights.append(e.astype(BF16))
        out.append((jnp.concatenate(weights, axis=1), recips))
    return out


def _weighted_values(probs, vcat):
    slabs_per_kv = len(probs) // N_KV_HEADS
    m = probs[0][0].shape[0]
    first_head = lax.broadcasted_iota(jnp.int32, (m, LANES), 1) < HEAD_DIM
    out = []
    for kv in range(N_KV_HEADS):
        group = probs[kv * slabs_per_kv:(kv + 1) * slabs_per_kv]
        o = _dot(jnp.concatenate([w for w, _ in group], axis=0), vcat[kv])
        out += [o[i * m:(i + 1) * m] * jnp.where(first_head, r[0], r[1])
                for i, (_, r) in enumerate(group)]
    return out


def _gates(h, win_ref):
    return (jax.nn.sigmoid(_dot(h, win_ref[:, C_GA:C_GB])),
            jax.nn.sigmoid(_dot(h, win_ref[:, C_GB:D_IN])))


def _gated_branch(gate, o, w_ref):
    return gate * _dot(o.astype(BF16), w_ref[...])


def _merge(x, branch_a, branch_b, wout_ref):
    return x + _dot((branch_a + branch_b).astype(BF16), wout_ref[...])


def _prompt_kernel(x_ref, g1_ref, win_ref, gq_ref, gk_ref, bq_ref, bk_ref, sinks_ref, lng_ref,
                   lnb_ref, wsp_ref, bsp_ref, wa_ref, wb_ref, wout_ref, bias_ref,
                   y_ref, kwin_ref, vwin_ref,
                   kprev, kprev_rot, vprev, vprev_rot):
    j = pl.program_id(1)

    @pl.when(j == 0)
    def _():
        for r in (kprev, kprev_rot, vprev, vprev_rot):
            r[...] = jnp.zeros_like(r)

    x = x_ref[0]
    h, qn, kn, v = _qkv(x, g1_ref[...], win_ref, gq_ref[...], gk_ref[...], bq_ref[...], bk_ref[...])

    kb = kn.astype(BF16)
    kr = pltpu.roll(kn, HEAD_DIM, 1).astype(BF16)
    vb = v.astype(BF16)
    vr = pltpu.roll(v, HEAD_DIM, 1).astype(BF16)

    n_blocks = x.shape[0] // WINDOW
    blocks = [slice(i * WINDOW, (i + 1) * WINDOW) for i in range(n_blocks)]
    scores, vcats, probs = [], [], []
    pk, pkr, pv, pvr = kprev[...], kprev_rot[...], vprev[...], vprev_rot[...]
    for i, rows in enumerate(blocks):
        ck, ckr, cv, cvr = kb[rows], kr[rows], vb[rows], vr[rows]
        kcat = _head_variants(pk, pkr, ck, ckr)
        vcats.append(_head_variants(pv, pvr, cv, cvr))
        bias = bias_ref[jnp.where(j == 0, 1, 0)] if i == 0 else bias_ref[0]
        scores.append(_scores(_lane_slabs(qn[rows]), kcat, bias))
        if i:
            probs.append(_sink_softmax(scores[i - 1], sinks_ref))
        pk, pkr, pv, pvr = ck, ckr, cv, cvr
    probs.append(_sink_softmax(scores[-1], sinks_ref))
    kprev[...] = pk
    kprev_rot[...] = pkr
    vprev[...] = pv
    vprev_rot[...] = pvr

    gu, vsn = _sgu_inputs(h, win_ref, lng_ref[...], lnb_ref[...])
    vsb = vsn.astype(BF16)
    mixed = [_dot(wsp_ref[g], jnp.concatenate(
        [vsb[rows, g * SGU_GROUP_DIM:(g + 1) * SGU_GROUP_DIM] for rows in blocks], axis=1))
        for g in range(SGU_GROUPS)]
    ob = jnp.concatenate([
        gu[rows] * (jnp.concatenate(
            [m[:, i * SGU_GROUP_DIM:(i + 1) * SGU_GROUP_DIM] for m in mixed], axis=1) + bsp_ref[...])
        for i, rows in enumerate(blocks)], axis=0)

    sga, sgb = _gates(h, win_ref)
    branch_b = _gated_branch(sgb, ob, wb_ref)

    oa = jnp.concatenate(
        [jnp.concatenate(_weighted_values(pr, vc), axis=1) for pr, vc in zip(probs, vcats)], axis=0)
    y_ref[0] = _merge(x, _gated_branch(sga, oa, wa_ref), branch_b, wout_ref)

    @pl.when(j == pl.num_programs(1) - 1)
    def _():
        kwin_ref[0] = kn[(n_blocks - 1) * WINDOW:]
        vwin_ref[0] = v[(n_blocks - 1) * WINDOW:]


def _const_spec(shape, single_buffer=True):
    nd = len(shape)
    mode = pl.Buffered(1) if single_buffer else None
    return pl.BlockSpec(shape, lambda *_: (0,) * nd, pipeline_mode=mode)


def _smem_spec():
    return pl.BlockSpec(memory_space=pltpu.SMEM)


def _prompt_mixer(x, p):
    b, s, d = x.shape
    ts = PROMPT_TILE
    assert s % ts == 0 and ts % WINDOW == 0
    in_specs = [
        pl.BlockSpec((1, ts, d), lambda bi, j: (bi, j, 0)),
        _const_spec((1, d)), _const_spec((d, D_IN)),
        _const_spec((1, ATTN_WIDTH)), _const_spec((1, KV_WIDTH)),
        _const_spec((ATTN_WIDTH, ATTN_WIDTH)), _const_spec((KV_WIDTH, KV_WIDTH)),
        _smem_spec(),
        _const_spec((1, SGU_WIDTH)), _const_spec((1, SGU_WIDTH)),
        _const_spec((SGU_GROUPS, CHUNK, CHUNK)), _const_spec((CHUNK, SGU_WIDTH)),
        _const_spec((ATTN_WIDTH, d)), _const_spec((SGU_WIDTH, d)), _const_spec((d, d)),
        _const_spec((2, WINDOW, 4 * WINDOW)),
    ]
    out_specs = [
        pl.BlockSpec((1, ts, d), lambda bi, j: (bi, j, 0)),
        pl.BlockSpec((1, WINDOW, KV_WIDTH), lambda bi, j: (bi, 0, 0)),
        pl.BlockSpec((1, WINDOW, KV_WIDTH), lambda bi, j: (bi, 0, 0)),
    ]
    out_shape = [
        jax.ShapeDtypeStruct((b, s, d), F32),
        jax.ShapeDtypeStruct((b, WINDOW, KV_WIDTH), F32),
        jax.ShapeDtypeStruct((b, WINDOW, KV_WIDTH), F32),
    ]
    return pl.pallas_call(
        _prompt_kernel,
        grid=(b, s // ts),
        in_specs=in_specs, out_specs=out_specs, out_shape=out_shape,
        scratch_shapes=[pltpu.VMEM((WINDOW, KV_WIDTH), BF16)] * 4,
        compiler_params=pltpu.CompilerParams(
            dimension_semantics=("arbitrary", "arbitrary"), vmem_limit_bytes=VMEM_LIMIT),
        name="prompt_mixer",
    )(x, p["g1"], p["w_in"], p["gq"], p["gk"], p["bq"], p["bk"], p["sinks"], p["lng"], p["lnb"],
      p["wsp"], p["bsp"], p["wa"], p["wb"], p["wout"], p["bias_prompt"])


def _sample_in_kernel(x_ref, g1_ref, win_ref, gq_ref, gk_ref, bq_ref, bk_ref, lng_ref, lnb_ref,
                      coef_ref, sbias_ref,
                      q_ref, k_ref, v_ref, ob_ref, vs_ref, sga_ref, sgb_ref):
    x = x_ref[...]
    h, qn, kn, v = _qkv(x, g1_ref[...], win_ref, gq_ref[...], gk_ref[...], bq_ref[...], bk_ref[...])
    gu, vsn = _sgu_inputs(h, win_ref, lng_ref[...], lnb_ref[...])
    for p, slab in enumerate(_lane_slabs(qn)):
        q_ref[p] = slab
    k_ref[...] = kn
    v_ref[...] = v
    vs_ref[...] = vsn
    sga_ref[...], sgb_ref[...] = _gates(h, win_ref)
    nb = x.shape[0] // 4
    for t in range(4):
        mixed = sbias_ref[t:t + 1, :]
        for jj in range(t + 1):
            mixed = mixed + coef_ref[4 * t + jj:4 * t + jj + 1, :] * vsn[jj * nb:(jj + 1) * nb]
        ob_ref[t * nb:(t + 1) * nb, :] = gu[t * nb:(t + 1) * nb] * mixed


def _sample_in(x, p):
    n, d = x.shape
    shapes = [(ATTN_WIDTH // LANES, n, LANES), (n, KV_WIDTH), (n, KV_WIDTH), (n, SGU_WIDTH),
              (n, SGU_WIDTH), (n, d), (n, d)]
    return pl.pallas_call(
        _sample_in_kernel,
        grid=(1,),
        in_specs=[_const_spec((n, d)), _const_spec((1, d)), _const_spec((d, D_IN)),
                  _const_spec((1, ATTN_WIDTH)), _const_spec((1, KV_WIDTH)),
                  _const_spec((ATTN_WIDTH, ATTN_WIDTH)), _const_spec((KV_WIDTH, KV_WIDTH)),
                  _const_spec((1, SGU_WIDTH)), _const_spec((1, SGU_WIDTH)),
                  _const_spec((16, SGU_WIDTH)), _const_spec((4, SGU_WIDTH))],
        out_specs=[_const_spec(sh, False) for sh in shapes],
        out_shape=[jax.ShapeDtypeStruct(sh, F32) for sh in shapes],
        compiler_params=pltpu.CompilerParams(
            dimension_semantics=("arbitrary",), vmem_limit_bytes=VMEM_LIMIT),
        name="sample_in",
    )(x, p["g1"], p["w_in"], p["gq"], p["gk"], p["bq"], p["bk"], p["lng"], p["lnb"],
      p["coef_s"], p["bias_sgu_s"])


def _sample_attn_kernel(q_ref, k_ref, v_ref, ck_ref, cv_ref, sinks_ref, bias_ref,
                        oa_ref, nk_ref, nv_ref):
    i = pl.program_id(0)
    n_slabs = q_ref.shape[0]
    half_rows = q_ref.shape[1] // 8
    pad = jnp.zeros((WINDOW - 8, KV_WIDTH), BF16)
    row_half = lax.broadcasted_iota(jnp.int32, (8, LANES), 0) % 2

    work = []
    for bb in range(SAMPLE_BATCH_TILE):
        b_lo = i * SAMPLE_BATCH_TILE + bb
        rows = pl.ds(b_lo, 8, stride=half_rows)
        q8 = [q_ref[p, rows, :] for p in range(n_slabs)]
        k8 = k_ref[rows, :]
        v8 = v_ref[rows, :]
        ck_new = jnp.concatenate([k8.astype(BF16), pad], axis=0)
        ckr_new = jnp.concatenate([pltpu.roll(k8, HEAD_DIM, 1).astype(BF16), pad], axis=0)
        cv_new = jnp.concatenate([v8.astype(BF16), pad], axis=0)
        cvr_new = jnp.concatenate([pltpu.roll(v8, HEAD_DIM, 1).astype(BF16), pad], axis=0)
        for hf in range(2):
            kc = ck_ref[hf, bb]
            vc = cv_ref[hf, bb]
            kcat = _head_variants(kc.astype(BF16), pltpu.roll(kc, HEAD_DIM, 1).astype(BF16),
                                  ck_new, ckr_new)
            vcat = _head_variants(vc.astype(BF16), pltpu.roll(vc, HEAD_DIM, 1).astype(BF16),
                                  cv_new, cvr_new)
            work.append((rows, hf, _scores(q8, kcat, bias_ref[hf]), vcat))
            nk_ref[hf, bb] = pltpu.roll(kc, WINDOW - 4, 0)
            nv_ref[hf, bb] = pltpu.roll(vc, WINDOW - 4, 0)
            for t in range(4):
                nk_ref[hf, bb, WINDOW - 4 + t:WINDOW - 3 + t, :] = k8[2 * t + hf:2 * t + hf + 1, :]
                nv_ref[hf, bb, WINDOW - 4 + t:WINDOW - 3 + t, :] = v8[2 * t + hf:2 * t + hf + 1, :]
    probs = [_sink_softmax(s, sinks_ref) for _, _, s, _ in work]
    outs = [_weighted_values(pr, w[3]) for pr, w in zip(probs, work)]
    for (rows, _, _, _), o_even, o_odd in zip(work[0::2], outs[0::2], outs[1::2]):
        for p in range(n_slabs):
            oa_ref[p, rows, :] = jnp.where(row_half == 0, o_even[p], o_odd[p])


def _sample_attn(q, k, v, ck, cv, p):
    n_slabs, n, _ = q.shape
    nb = n // 4
    half = nb // 2
    bt = SAMPLE_BATCH_TILE
    assert half % bt == 0
    ck4 = ck.reshape(2, half, WINDOW, KV_WIDTH)
    cv4 = cv.reshape(2, half, WINDOW, KV_WIDTH)
    cache_spec = pl.BlockSpec((2, bt, WINDOW, KV_WIDTH), lambda i: (0, i, 0, 0))
    return pl.pallas_call(
        _sample_attn_kernel,
        grid=(half // bt,),
        in_specs=[_const_spec((n_slabs, n, LANES)), _const_spec((n, KV_WIDTH)),
                  _const_spec((n, KV_WIDTH)), cache_spec, cache_spec, _smem_spec(),
                  _const_spec((2, 8, 4 * WINDOW))],
        out_specs=[_const_spec((n_slabs, n, LANES), False), cache_spec, cache_spec],
        out_shape=[jax.ShapeDtypeStruct((n_slabs, n, LANES), F32),
                   jax.ShapeDtypeStruct(ck4.shape, F32), jax.ShapeDtypeStruct(cv4.shape, F32)],
        compiler_params=pltpu.CompilerParams(
            dimension_semantics=("arbitrary",), vmem_limit_bytes=VMEM_LIMIT),
        name="sample_attn",
    )(q, k, v, ck4, cv4, p["sinks"], p["bias_sample"])


def _sample_merge_kernel(x_ref, oa_ref, ob_ref, sga_ref, sgb_ref, wa_ref, wb_ref, wout_ref, y_ref):
    oa = jnp.concatenate([oa_ref[p] for p in range(oa_ref.shape[0])], axis=1)
    y_ref[...] = _merge(x_ref[...], _gated_branch(sga_ref[...], oa, wa_ref),
                        _gated_branch(sgb_ref[...], ob_ref[...], wb_ref), wout_ref)


def _sample_merge(x, oa, ob, sga, sgb, p):
    n, d = x.shape
    return pl.pallas_call(
        _sample_merge_kernel,
        grid=(1,),
        in_specs=[_const_spec((n, d)), _const_spec(oa.shape), _const_spec((n, SGU_WIDTH)),
                  _const_spec((n, d)), _const_spec((n, d)),
                  _const_spec((ATTN_WIDTH, d)), _const_spec((SGU_WIDTH, d)), _const_spec((d, d))],
        out_specs=_const_spec((n, d), False),
        out_shape=jax.ShapeDtypeStruct((n, d), F32),
        compiler_params=pltpu.CompilerParams(
            dimension_semantics=("arbitrary",), vmem_limit_bytes=VMEM_LIMIT),
        name="sample_merge",
    )(x, oa, ob, sga, sgb, p["wa"], p["wb"], p["wout"])


def _route(logits):
    lane = lax.broadcasted_iota(jnp.int32, logits.shape, 1).astype(F32)
    far = float(ROUTER_LANES)
    glm = jnp.where(lane < N_GROUPS, logits, NEG_INF)
    gmax = jnp.max(glm, axis=-1, keepdims=True)
    gidx = jnp.min(jnp.where(glm == gmax, lane, far), axis=-1, keepdims=True)
    gw = 1.0 / jnp.sum(jnp.exp(glm - gmax), axis=-1, keepdims=True)
    first = EXPERT_LANE0 + EXPERTS_PER_GROUP * gidx
    sel = (lane >= first) & (lane < first + EXPERTS_PER_GROUP)
    el = jnp.where(sel, logits, NEG_INF)
    t1 = jnp.max(el, axis=-1, keepdims=True)
    i1 = jnp.min(jnp.where(el == t1, lane, far), axis=-1, keepdims=True)
    el2 = jnp.where(lane == i1, NEG_INF, el)
    t2 = jnp.max(el2, axis=-1, keepdims=True)
    i2 = jnp.min(jnp.where(el2 == t2, lane, far), axis=-1, keepdims=True)
    e2 = jnp.exp(t2 - t1)
    den = 1.0 + e2
    w1 = (1.0 / den) * gw
    w2 = (e2 / den) * gw
    return gidx, jnp.where(lane == i1 - first, w1, 0.0) + jnp.where(lane == i2 - first, w2, 0.0)


def _split_bf16(x):
    hi = x.astype(BF16)
    lo = (x - hi.astype(F32)).astype(BF16)
    return hi, lo


def _moe_kernel(y_ref, g2_ref, wrc_ref, wrh_ref, br_ref, wg_ref, wu_ref, wd_ref,
                ltri_ref, eye_ref, o_ref, hs_ref, cws_ref, os_ref):
    n_tiles = y_ref.shape[0] // MOE_TILE
    tiles = [slice(t * MOE_TILE, (t + 1) * MOE_TILE) for t in range(n_tiles)]

    @pl.when(pl.program_id(0) == 0)
    def _():
        os_ref[...] = jnp.zeros_like(os_ref)

    h_split = [_split_bf16(_rms(y_ref[r, :], g2_ref[...])) for r in tiles]
    part = [_dot(hb, wrc_ref[...]) for hb, _ in h_split]
    low = [_dot(hl, wrh_ref[...]) for _, hl in h_split]
    routed = [_route(pt[:, :ROUTER_LANES] + (pt[:, ROUTER_LANES:] + lw) + br_ref[...])
              for pt, lw in zip(part, low)]

    lane = lax.broadcasted_iota(jnp.int32, (MOE_TILE, ROUTER_LANES), 1).astype(F32)
    lane1 = lane[0:1]
    onehot = [jnp.where(lane == gidx, 1.0, 0.0) for gidx, _ in routed]
    before = [_dot(ltri_ref[...], oh.astype(BF16)) for oh in onehot]
    ends, pos = [], []
    for oh, bf in zip(onehot, before):
        count = jnp.sum(oh, axis=0, keepdims=True)
        n_chunks = jnp.floor((count + (MOE_CHUNK - 1)) * (1.0 / MOE_CHUNK))
        tile_ends = []
        start_vec = jnp.zeros_like(n_chunks)
        end = jnp.zeros((1, 1), F32)
        for g in range(N_GROUPS):
            start_vec = start_vec + jnp.where(lane1 == g, end * MOE_CHUNK, 0.0)
            end = end + jnp.sum(jnp.where(lane1 == g, n_chunks, 0.0), axis=-1, keepdims=True)
            tile_ends.append(end[0, 0].astype(jnp.int32))
        ends.append(tile_ends)
        pos.append(jnp.sum(oh * (bf + start_vec), axis=-1, keepdims=True))

    pos_t = []
    for ps in pos:
        pos_hi = jnp.floor(ps * (1.0 / SLOT_SPLIT))
        pos_lo = ps - pos_hi * SLOT_SPLIT
        cols = jnp.where(lane == 0.0, pos_hi, jnp.where(lane == 1.0, pos_lo, 0.0)).astype(BF16)
        pos_t.append(_dot_nt(eye_ref[...], cols))
    slot_s = lax.broadcasted_iota(jnp.int32, (MOE_SLOTS, MOE_TILE), 0).astype(F32)
    for t, (pt, (hb, _), (_, cw)) in enumerate(zip(pos_t, h_split, routed)):
        pos_row = pt[0:1] * SLOT_SPLIT + pt[1:2]
        sort = jnp.where(slot_s == pos_row, 1.0, 0.0).astype(BF16)
        cw_hi = cw.astype(BF16).astype(F32)
        cw_parts = (cw_hi + pltpu.roll(cw - cw_hi, LANES // 2, 1)).astype(BF16)
        sorted_rows = _dot(sort, jnp.concatenate([hb, cw_parts], axis=1))
        hs_ref[t] = sorted_rows[:, :D_MODEL].astype(BF16)
        parts = sorted_rows[:, D_MODEL:]
        cws_ref[t] = parts + pltpu.roll(parts, LANES // 2, 1)

    def run_group(t, g, first_chunk, n_rows):
        rows = pl.ds(pl.multiple_of(first_chunk * MOE_CHUNK, MOE_CHUNK), n_rows)
        hsc = hs_ref[t, rows, :]
        cwc = cws_ref[t, rows, :]
        cexp = jnp.concatenate([jnp.broadcast_to(cwc[:, e:e + 1], (n_rows, D_EXPERT))
                                for e in range(EXPERTS_PER_GROUP)], axis=1)
        experts = [EXPERTS_PER_GROUP * g + e for e in range(EXPERTS_PER_GROUP)]
        a = jnp.concatenate([_dot(hsc, wg_ref[e]) for e in experts], axis=1)
        u = jnp.concatenate([_dot(hsc, wu_ref[e]) for e in experts], axis=1)
        hid = (a * jax.nn.sigmoid(a)) * u * cexp
        os_ref[t, rows, :] = _dot(hid.astype(BF16), wd_ref[g]).astype(BF16)

    def group(i, carry):
        t = lax.shift_right_logical(i, GROUP_BITS)
        g = i & (N_GROUPS - 1)
        first, last = jnp.int32(0), jnp.int32(0)
        for tt in range(n_tiles):
            for gg in range(N_GROUPS):
                here = (t == tt) & (g == gg)
                first = jnp.where(here, ends[tt][gg - 1] if gg else 0, first)
                last = jnp.where(here, ends[tt][gg], last)
        def full_pass(state):
            start, left = state
            run_group(t, g, start, MOE_PASS_CHUNKS * MOE_CHUNK)
            return start + MOE_PASS_CHUNKS, left - MOE_PASS_CHUNKS

        first, n = lax.while_loop(lambda st: st[1] > MOE_PASS_CHUNKS, full_pass,
                                  (first, last - first))
        for k in range(1, MOE_PASS_CHUNKS + 1):
            @pl.when(n == k)
            def _(k=k):
                run_group(t, g, first, k * MOE_CHUNK)
        return carry

    lax.fori_loop(0, n_tiles * N_GROUPS, group, 0)

    slot_l = lax.broadcasted_iota(jnp.int32, (MOE_TILE, MOE_SLOTS), 1).astype(F32)
    for t, (r, ps) in enumerate(zip(tiles, pos)):
        unsort = jnp.where(slot_l == ps, 1.0, 0.0).astype(BF16)
        o_ref[r, :] = y_ref[r, :] + _dot(unsort, os_ref[t])


def _moe(y, p):
    n, d = y.shape
    tm = min(MOE_STEP_ROWS, n)
    assert n % tm == 0 and tm % MOE_TILE == 0
    n_tiles = tm // MOE_TILE
    row_spec = pl.BlockSpec((tm, d), lambda i: (i, 0))
    return pl.pallas_call(
        _moe_kernel,
        grid=(n // tm,),
        in_specs=[row_spec, _const_spec((1, d)), _const_spec((d, 2 * ROUTER_LANES)),
                  _const_spec((d, ROUTER_LANES)), _const_spec((1, ROUTER_LANES)),
                  _const_spec((N_EXPERTS, d, D_EXPERT)), _const_spec((N_EXPERTS, d, D_EXPERT)),
                  _const_spec((N_GROUPS, GROUP_HIDDEN, d)),
                  _const_spec((MOE_TILE, MOE_TILE)), _const_spec((LANES, LANES))],
        out_specs=row_spec,
        out_shape=jax.ShapeDtypeStruct((n, d), F32),
        scratch_shapes=[pltpu.VMEM((n_tiles, MOE_SLOTS, d), BF16),
                        pltpu.VMEM((n_tiles, MOE_SLOTS, ROUTER_LANES), F32),
                        pltpu.VMEM((n_tiles, MOE_SLOTS, d), BF16)],
        compiler_params=pltpu.CompilerParams(
            dimension_semantics=("arbitrary",), vmem_limit_bytes=MOE_VMEM_LIMIT),
        name="expert_mixer",
    )(y, p["g2"], p["wr_cat"], p["wr_hi"], p["br"], p["wg"], p["wu"], p["wd"],
      p["ltri"], p["eye"])


def _window_bias(n_q, q_tok, key_prev_ok, key_cur_ok):
    half = np.concatenate([key_prev_ok, key_cur_ok], axis=1)
    ok = np.concatenate([half, half], axis=1)
    return np.where(ok, 0.0, NEG_INF).astype(np.float32)


def _prompt_bias():
    t = np.arange(WINDOW)[:, None]
    s = np.arange(WINDOW)[None, :]
    prev_ok = s > t
    cur_ok = s <= t
    normal = _window_bias(WINDOW, t, prev_ok, cur_ok)
    first = _window_bias(WINDOW, t, np.zeros_like(prev_ok), cur_ok)
    return np.stack([normal, first])


def _sample_bias():
    out = []
    r = np.arange(8)[:, None]
    t = r // 2
    s = np.arange(WINDOW)[None, :]
    prev_ok = s > t
    for hf in range(2):
        c = s
        cur_ok = (c < 8) & (c % 2 == hf) & (c // 2 <= t)
        out.append(_window_bias(8, t, np.broadcast_to(prev_ok, (8, WINDOW)), cur_ok))
    return np.stack(out)


def _block_diag_mean(width):
    idx = np.arange(width) // HEAD_DIM
    return (idx[:, None] == idx[None, :]).astype(np.float32) / HEAD_DIM


def _prepare(norm1_g, w_in, q_norm_g, k_norm_g, attn_sinks, ln_v_g, ln_v_b, w_spatial, b_spatial,
             w_branch_a, w_branch_b, w_out, norm2_g, w_router_group, b_router_group,
             w_router_expert, b_router_expert, w_exp_gate, w_exp_up, w_exp_down):
    p = {}
    p["g1"] = norm1_g.reshape(1, D_MODEL)
    p["w_in"] = w_in.astype(BF16)
    p["gq"] = (jnp.tile(q_norm_g, N_HEADS) * (HEAD_DIM ** -0.5)).reshape(1, ATTN_WIDTH)
    p["gk"] = jnp.tile(k_norm_g, N_KV_HEADS).reshape(1, KV_WIDTH)
    p["bq"] = jnp.asarray(_block_diag_mean(ATTN_WIDTH), BF16)
    p["bk"] = jnp.asarray(_block_diag_mean(KV_WIDTH), BF16)
    p["sinks"] = attn_sinks.astype(F32)
    p["lng"] = ln_v_g.reshape(1, SGU_WIDTH)
    p["lnb"] = ln_v_b.reshape(1, SGU_WIDTH)
    tril = jnp.tril(jnp.ones((CHUNK, CHUNK), F32))
    wsp = w_spatial * tril[None]
    p["wsp"] = wsp.astype(BF16)
    p["bsp"] = jnp.repeat(b_spatial.T, SGU_GROUP_DIM, axis=1)
    w4 = wsp[:, :4, :4]
    p["coef_s"] = jnp.repeat(jnp.transpose(w4, (1, 2, 0)).reshape(16, SGU_GROUPS),
                             SGU_GROUP_DIM, axis=1)
    p["bias_sgu_s"] = jnp.repeat(b_spatial[:, :4].T, SGU_GROUP_DIM, axis=1)
    p["wa"] = w_branch_a.astype(BF16)
    p["wb"] = w_branch_b.astype(BF16)
    p["wout"] = w_out.astype(BF16)
    p["g2"] = norm2_g.reshape(1, D_MODEL)
    wr = jnp.concatenate(
        [w_router_group, w_router_expert,
         jnp.zeros((D_MODEL, ROUTER_LANES - N_GROUPS - N_EXPERTS), F32)], axis=1)
    p["wr_hi"], wr_lo = _split_bf16(wr)
    p["wr_cat"] = jnp.concatenate([p["wr_hi"], wr_lo], axis=1)
    p["br"] = jnp.concatenate(
        [b_router_group, b_router_expert,
         jnp.zeros((ROUTER_LANES - N_GROUPS - N_EXPERTS,), F32)]).reshape(1, ROUTER_LANES)
    p["wg"] = w_exp_gate.astype(BF16)
    p["wu"] = w_exp_up.astype(BF16)
    p["wd"] = w_exp_down.reshape(N_GROUPS, GROUP_HIDDEN, D_MODEL).astype(BF16)
    p["ltri"] = jnp.asarray(np.tril(np.ones((MOE_TILE, MOE_TILE), np.float32), -1), BF16)
    p["eye"] = jnp.asarray(np.eye(LANES, dtype=np.float32), BF16)
    p["bias_prompt"] = jnp.asarray(_prompt_bias())
    p["bias_sample"] = jnp.asarray(_sample_bias())
    return p


def kernel(x_prompt, x_sample, cache_k_win, cache_v_win, norm1_g, w_in, q_norm_g, k_norm_g, attn_sinks, ln_v_g, ln_v_b, w_spatial, b_spatial, w_branch_a, w_branch_b, w_out, norm2_g, w_router_group, b_router_group, w_router_expert, b_router_expert, w_exp_gate, w_exp_up, w_exp_down):
    depth = norm1_g.shape[0]
    assert depth == 1
    batch, seq, d = x_prompt.shape
    dec_batch, dec_seq, _ = x_sample.shape
    assert dec_seq == 4 and d == D_MODEL
    p = _prepare(*(a[0] for a in (
        norm1_g, w_in, q_norm_g, k_norm_g, attn_sinks, ln_v_g, ln_v_b, w_spatial, b_spatial,
        w_branch_a, w_branch_b, w_out, norm2_g, w_router_group, b_router_group,
        w_router_expert, b_router_expert, w_exp_gate, w_exp_up, w_exp_down)))

    y1p, kwin, vwin = _prompt_mixer(x_prompt, p)
    yp = _moe(y1p.reshape(batch * seq, d), p).reshape(batch, seq, d)

    xs = jnp.transpose(x_sample, (1, 0, 2)).reshape(dec_seq * dec_batch, d)
    q, k, v, ob, vs, sga, sgb = _sample_in(xs, p)
    ck = cache_k_win[0].reshape(dec_batch, WINDOW, KV_WIDTH)
    cv = cache_v_win[0].reshape(dec_batch, WINDOW, KV_WIDTH)
    oa, nk, nv = _sample_attn(q, k, v, ck, cv, p)
    y1s = _sample_merge(xs, oa, ob, sga, sgb, p)
    ys = _moe(y1s, p)
    ys = jnp.transpose(ys.reshape(dec_seq, dec_batch, d), (1, 0, 2))
    vs_out = jnp.transpose(vs.reshape(dec_seq, dec_batch, SGU_GROUPS, SGU_GROUP_DIM), (1, 0, 2, 3))

    kv_shape = (WINDOW, N_KV_HEADS, HEAD_DIM)
    return (yp, ys,
            kwin.reshape(1, batch, *kv_shape), vwin.reshape(1, batch, *kv_shape),
            nk.reshape(1, dec_batch, *kv_shape), nv.reshape(1, dec_batch, *kv_shape),
            vs_out[None])
```

```python
import numpy as np
import jax
import jax.numpy as jnp
from jax import lax
from jax.experimental import pallas as pl
from jax.experimental.pallas import tpu as pltpu

F32 = jnp.float32
BF16 = jnp.bfloat16

D_MODEL = 1024
N_HEADS = 8
N_KV_HEADS = 2
HEAD_DIM = 64
WINDOW = 128
ATTN_WIDTH = N_HEADS * HEAD_DIM
KV_WIDTH = N_KV_HEADS * HEAD_DIM
SGU_GROUPS = 4
SGU_WIDTH = D_MODEL // 2
SGU_GROUP_DIM = SGU_WIDTH // SGU_GROUPS
CHUNK = 128
N_GROUPS = 4
GROUP_BITS = 2
EXPERTS_PER_GROUP = 4
N_EXPERTS = N_GROUPS * EXPERTS_PER_GROUP
D_EXPERT = 256
GROUP_HIDDEN = EXPERTS_PER_GROUP * D_EXPERT
D_IN = ATTN_WIDTH + 2 * KV_WIDTH + 2 * SGU_WIDTH + 2 * D_MODEL
EPS = 1e-6
NEG_INF = -1e30

C_Q = 0
C_K = C_Q + ATTN_WIDTH
C_V = C_K + KV_WIDTH
C_U = C_V + KV_WIDTH
C_VS = C_U + SGU_WIDTH
C_GA = C_VS + SGU_WIDTH
C_GB = C_GA + D_MODEL

LANES = 128
ROUTER_LANES = LANES
EXPERT_LANE0 = N_GROUPS
PROMPT_TILE = 1024
MOE_TILE = 512
MOE_STEP_ROWS = 2 * MOE_TILE
MOE_CHUNK = 32
MOE_PASS_CHUNKS = 6
MOE_MAX_CHUNKS = (MOE_TILE + N_GROUPS * (MOE_CHUNK - 1)) // MOE_CHUNK
MOE_SLOTS = -(-MOE_MAX_CHUNKS * MOE_CHUNK // LANES) * LANES
SLOT_SPLIT = 128
SAMPLE_BATCH_TILE = 8
VMEM_LIMIT = 56 * 1024 * 1024
MOE_VMEM_LIMIT = 62 * 1024 * 1024

_SQRT_2_OVER_PI = np.sqrt(2.0 / np.pi).astype(np.float32)


def _dot(a, b):
    return jnp.dot(a, b, preferred_element_type=F32)


def _dot_nt(a, b):
    return lax.dot_general(a, b, (((1,), (1,)), ((), ())), preferred_element_type=F32)


def _gelu(x):
    cdf = 0.5 * (1.0 + jnp.tanh(_SQRT_2_OVER_PI * (x + 0.044715 * (x * x * x))))
    return x * cdf


def _rms(x, g):
    return x * lax.rsqrt(jnp.mean(x * x, axis=-1, keepdims=True) + EPS) * g


def _head_rms(x, blockdiag, g):
    ms = _dot((x * x).astype(BF16), blockdiag)
    return x * lax.rsqrt(ms + EPS) * g


def _qkv(x, g1, win_ref, gq, gk, bq, bk):
    h = _rms(x, g1).astype(BF16)
    qn = _head_rms(_dot(h, win_ref[:, C_Q:C_K]), bq, gq)
    kv = _dot(h, win_ref[:, C_K:C_U])
    kn = _head_rms(kv[:, :KV_WIDTH], bk, gk)
    v = kv[:, KV_WIDTH:]
    return h, qn, kn, v


def _sgu_inputs(h, win_ref, lng, lnb):
    gu = _gelu(_dot(h, win_ref[:, C_U:C_VS]))
    gv = _gelu(_dot(h, win_ref[:, C_VS:C_GA]))
    mu = jnp.mean(gv, axis=-1, keepdims=True)
    xc = gv - mu
    vsn = xc * lax.rsqrt(jnp.mean(xc * xc, axis=-1, keepdims=True) + EPS) * lng + lnb
    return gu, vsn


def _head_variants(prev, prev_rot, cur, cur_rot):
    a = jnp.concatenate([prev, cur], axis=0)
    r = jnp.concatenate([prev_rot, cur_rot], axis=0)
    lo = lax.broadcasted_iota(jnp.int32, a.shape, 1) < HEAD_DIM
    zero = jnp.zeros_like(a)
    kv0 = jnp.concatenate([jnp.where(lo, a, zero), jnp.where(lo, zero, r)], axis=0)
    kv1 = jnp.concatenate([jnp.where(lo, r, zero), jnp.where(lo, zero, a)], axis=0)
    return kv0, kv1


def _lane_slabs(x):
    return [x[:, p * LANES:(p + 1) * LANES] for p in range(x.shape[1] // LANES)]


def _scores(q_slabs, kcat, bias):
    slabs_per_kv = len(q_slabs) // N_KV_HEADS
    m = q_slabs[0].shape[0]
    out = []
    for kv in range(N_KV_HEADS):
        q = jnp.concatenate(q_slabs[kv * slabs_per_kv:(kv + 1) * slabs_per_kv], axis=0)
        s = _dot_nt(q.astype(BF16), kcat[kv])
        out += [s[i * m:(i + 1) * m] + bias for i in range(slabs_per_kv)]
    return out


def _sink_softmax(scores, sinks_ref):
    out = []
    for p, s in enumerate(scores):
        n_keys = s.shape[1] // 2
        probs = []
        for par in range(2):
            sh = s[:, par * n_keys:(par + 1) * n_keys]
            sink = sinks_ref[2 * p + par]
            m = jnp.maximum(jnp.max(sh, axis=-1, keepdims=True), sink)
            e = jnp.exp(sh - m)
            den = jnp.sum(e, axis=-1, keepdims=True) + jnp.exp(sink - m)
            probs.append((e / den).astype(BF16))
        out.append(jnp.concatenate(probs, axis=1))
    return out


def _weighted_values(probs, vcat):
    slabs_per_kv = len(probs) // N_KV_HEADS
    m = probs[0].shape[0]
    out = []
    for kv in range(N_KV_HEADS):
        pr = jnp.concatenate(probs[kv * slabs_per_kv:(kv + 1) * slabs_per_kv], axis=0)
        o = _dot(pr, vcat[kv])
        out += [o[i * m:(i + 1) * m] for i in range(slabs_per_kv)]
    return out


def _gates(h, win_ref):
    return (jax.nn.sigmoid(_dot(h, win_ref[:, C_GA:C_GB])),
            jax.nn.sigmoid(_dot(h, win_ref[:, C_GB:D_IN])))


def _merge(x, sga, sgb, oa, ob, wa_ref, wb_ref, wout_ref):
    ya = _dot(oa.astype(BF16), wa_ref[...])
    yb = _dot(ob.astype(BF16), wb_ref[...])
    hm = sga * ya + sgb * yb
    return x + _dot(hm.astype(BF16), wout_ref[...])


def _prompt_kernel(x_ref, g1_ref, win_ref, gq_ref, gk_ref, bq_ref, bk_ref, sinks_ref, lng_ref,
                   lnb_ref, wsp_ref, bsp_ref, wa_ref, wb_ref, wout_ref, bias_ref,
                   y_ref, kwin_ref, vwin_ref,
                   kprev, kprev_rot, vprev, vprev_rot):
    j = pl.program_id(1)

    @pl.when(j == 0)
    def _():
        for r in (kprev, kprev_rot, vprev, vprev_rot):
            r[...] = jnp.zeros_like(r)

    x = x_ref[0]
    h, qn, kn, v = _qkv(x, g1_ref[...], win_ref, gq_ref[...], gk_ref[...], bq_ref[...], bk_ref[...])

    kb = kn.astype(BF16)
    kr = pltpu.roll(kn, HEAD_DIM, 1).astype(BF16)
    vb = v.astype(BF16)
    vr = pltpu.roll(v, HEAD_DIM, 1).astype(BF16)

    n_blocks = x.shape[0] // WINDOW
    blocks = [slice(i * WINDOW, (i + 1) * WINDOW) for i in range(n_blocks)]
    scores, vcats = [], []
    pk, pkr, pv, pvr = kprev[...], kprev_rot[...], vprev[...], vprev_rot[...]
    for i, rows in enumerate(blocks):
        ck, ckr, cv, cvr = kb[rows], kr[rows], vb[rows], vr[rows]
        kcat = _head_variants(pk, pkr, ck, ckr)
        vcats.append(_head_variants(pv, pvr, cv, cvr))
        bias = bias_ref[jnp.where(j == 0, 1, 0)] if i == 0 else bias_ref[0]
        scores.append(_scores(_lane_slabs(qn[rows]), kcat, bias))
        pk, pkr, pv, pvr = ck, ckr, cv, cvr
    kprev[...] = pk
    kprev_rot[...] = pkr
    vprev[...] = pv
    vprev_rot[...] = pvr

    gu, vsn = _sgu_inputs(h, win_ref, lng_ref[...], lnb_ref[...])
    vsb = vsn.astype(BF16)
    mixed = [_dot(wsp_ref[g], jnp.concatenate(
        [vsb[rows, g * SGU_GROUP_DIM:(g + 1) * SGU_GROUP_DIM] for rows in blocks], axis=1))
        for g in range(SGU_GROUPS)]
    ob = jnp.concatenate([
        gu[rows] * (jnp.concatenate(
            [m[:, i * SGU_GROUP_DIM:(i + 1) * SGU_GROUP_DIM] for m in mixed], axis=1) + bsp_ref[...])
        for i, rows in enumerate(blocks)], axis=0)

    sga, sgb = _gates(h, win_ref)

    probs = [_sink_softmax(s, sinks_ref) for s in scores]
    oa = jnp.concatenate(
        [jnp.concatenate(_weighted_values(pr, vc), axis=1) for pr, vc in zip(probs, vcats)], axis=0)
    y_ref[0] = _merge(x, sga, sgb, oa, ob, wa_ref, wb_ref, wout_ref)

    @pl.when(j == pl.num_programs(1) - 1)
    def _():
        kwin_ref[0] = kn[(n_blocks - 1) * WINDOW:]
        vwin_ref[0] = v[(n_blocks - 1) * WINDOW:]


def _const_spec(shape, single_buffer=True):
    nd = len(shape)
    mode = pl.Buffered(1) if single_buffer else None
    return pl.BlockSpec(shape, lambda *_: (0,) * nd, pipeline_mode=mode)


def _smem_spec():
    return pl.BlockSpec(memory_space=pltpu.SMEM)


def _prompt_mixer(x, p):
    b, s, d = x.shape
    ts = PROMPT_TILE
    assert s % ts == 0 and ts % WINDOW == 0
    in_specs = [
        pl.BlockSpec((1, ts, d), lambda bi, j: (bi, j, 0)),
        _const_spec((1, d)), _const_spec((d, D_IN)),
        _const_spec((1, ATTN_WIDTH)), _const_spec((1, KV_WIDTH)),
        _const_spec((ATTN_WIDTH, ATTN_WIDTH)), _const_spec((KV_WIDTH, KV_WIDTH)),
        _smem_spec(),
        _const_spec((1, SGU_WIDTH)), _const_spec((1, SGU_WIDTH)),
        _const_spec((SGU_GROUPS, CHUNK, CHUNK)), _const_spec((CHUNK, SGU_WIDTH)),
        _const_spec((ATTN_WIDTH, d)), _const_spec((SGU_WIDTH, d)), _const_spec((d, d)),
        _const_spec((2, WINDOW, 4 * WINDOW)),
    ]
    out_specs = [
        pl.BlockSpec((1, ts, d), lambda bi, j: (bi, j, 0)),
        pl.BlockSpec((1, WINDOW, KV_WIDTH), lambda bi, j: (bi, 0, 0)),
        pl.BlockSpec((1, WINDOW, KV_WIDTH), lambda bi, j: (bi, 0, 0)),
    ]
    out_shape = [
        jax.ShapeDtypeStruct((b, s, d), F32),
        jax.ShapeDtypeStruct((b, WINDOW, KV_WIDTH), F32),
        jax.ShapeDtypeStruct((b, WINDOW, KV_WIDTH), F32),
    ]
    return pl.pallas_call(
        _prompt_kernel,
        grid=(b, s // ts),
        in_specs=in_specs, out_specs=out_specs, out_shape=out_shape,
        scratch_shapes=[pltpu.VMEM((WINDOW, KV_WIDTH), BF16)] * 4,
        compiler_params=pltpu.CompilerParams(
            dimension_semantics=("arbitrary", "arbitrary"), vmem_limit_bytes=VMEM_LIMIT),
        name="prompt_mixer",
    )(x, p["g1"], p["w_in"], p["gq"], p["gk"], p["bq"], p["bk"], p["sinks"], p["lng"], p["lnb"],
      p["wsp"], p["bsp"], p["wa"], p["wb"], p["wout"], p["bias_prompt"])


def _sample_in_kernel(x_ref, g1_ref, win_ref, gq_ref, gk_ref, bq_ref, bk_ref, lng_ref, lnb_ref,
                      coef_ref, sbias_ref,
                      q_ref, k_ref, v_ref, ob_ref, vs_ref, sga_ref, sgb_ref):
    x = x_ref[...]
    h, qn, kn, v = _qkv(x, g1_ref[...], win_ref, gq_ref[...], gk_ref[...], bq_ref[...], bk_ref[...])
    gu, vsn = _sgu_inputs(h, win_ref, lng_ref[...], lnb_ref[...])
    for p, slab in enumerate(_lane_slabs(qn)):
        q_ref[p] = slab
    k_ref[...] = kn
    v_ref[...] = v
    vs_ref[...] = vsn
    sga_ref[...], sgb_ref[...] = _gates(h, win_ref)
    nb = x.shape[0] // 4
    for t in range(4):
        mixed = sbias_ref[t:t + 1, :]
        for jj in range(t + 1):
            mixed = mixed + coef_ref[4 * t + jj:4 * t + jj + 1, :] * vsn[jj * nb:(jj + 1) * nb]
        ob_ref[t * nb:(t + 1) * nb, :] = gu[t * nb:(t + 1) * nb] * mixed


def _sample_in(x, p):
    n, d = x.shape
    shapes = [(ATTN_WIDTH // LANES, n, LANES), (n, KV_WIDTH), (n, KV_WIDTH), (n, SGU_WIDTH),
              (n, SGU_WIDTH), (n, d), (n, d)]
    return pl.pallas_call(
        _sample_in_kernel,
        grid=(1,),
        in_specs=[_const_spec((n, d)), _const_spec((1, d)), _const_spec((d, D_IN)),
                  _const_spec((1, ATTN_WIDTH)), _const_spec((1, KV_WIDTH)),
                  _const_spec((ATTN_WIDTH, ATTN_WIDTH)), _const_spec((KV_WIDTH, KV_WIDTH)),
                  _const_spec((1, SGU_WIDTH)), _const_spec((1, SGU_WIDTH)),
                  _const_spec((16, SGU_WIDTH)), _const_spec((4, SGU_WIDTH))],
        out_specs=[_const_spec(sh, False) for sh in shapes],
        out_shape=[jax.ShapeDtypeStruct(sh, F32) for sh in shapes],
        compiler_params=pltpu.CompilerParams(
            dimension_semantics=("arbitrary",), vmem_limit_bytes=VMEM_LIMIT),
        name="sample_in",
    )(x, p["g1"], p["w_in"], p["gq"], p["gk"], p["bq"], p["bk"], p["lng"], p["lnb"],
      p["coef_s"], p["bias_sgu_s"])


def _sample_attn_kernel(q_ref, k_ref, v_ref, ck_ref, cv_ref, sinks_ref, bias_ref,
                        oa_ref, nk_ref, nv_ref):
    i = pl.program_id(0)
    n_slabs = q_ref.shape[0]
    half_rows = q_ref.shape[1] // 8
    pad = jnp.zeros((WINDOW - 8, KV_WIDTH), BF16)
    row_half = lax.broadcasted_iota(jnp.int32, (8, LANES), 0) % 2

    work = []
    for bb in range(SAMPLE_BATCH_TILE):
        b_lo = i * SAMPLE_BATCH_TILE + bb
        rows = pl.ds(b_lo, 8, stride=half_rows)
        q8 = [q_ref[p, rows, :] for p in range(n_slabs)]
        k8 = k_ref[rows, :]
        v8 = v_ref[rows, :]
        ck_new = jnp.concatenate([k8.astype(BF16), pad], axis=0)
        ckr_new = jnp.concatenate([pltpu.roll(k8, HEAD_DIM, 1).astype(BF16), pad], axis=0)
        cv_new = jnp.concatenate([v8.astype(BF16), pad], axis=0)
        cvr_new = jnp.concatenate([pltpu.roll(v8, HEAD_DIM, 1).astype(BF16), pad], axis=0)
        for hf in range(2):
            kc = ck_ref[hf, bb]
            vc = cv_ref[hf, bb]
            kcat = _head_variants(kc.astype(BF16), pltpu.roll(kc, HEAD_DIM, 1).astype(BF16),
                                  ck_new, ckr_new)
            vcat = _head_variants(vc.astype(BF16), pltpu.roll(vc, HEAD_DIM, 1).astype(BF16),
                                  cv_new, cvr_new)
            work.append((rows, hf, _scores(q8, kcat, bias_ref[hf]), vcat))
            nk_ref[hf, bb] = pltpu.roll(kc, WINDOW - 4, 0)
            nv_ref[hf, bb] = pltpu.roll(vc, WINDOW - 4, 0)
            for t in range(4):
                nk_ref[hf, bb, WINDOW - 4 + t:WINDOW - 3 + t, :] = k8[2 * t + hf:2 * t + hf + 1, :]
                nv_ref[hf, bb, WINDOW - 4 + t:WINDOW - 3 + t, :] = v8[2 * t + hf:2 * t + hf + 1, :]
    probs = [_sink_softmax(s, sinks_ref) for _, _, s, _ in work]
    outs = [_weighted_values(pr, w[3]) for pr, w in zip(probs, work)]
    for (rows, _, _, _), o_even, o_odd in zip(work[0::2], outs[0::2], outs[1::2]):
        for p in range(n_slabs):
            oa_ref[p, rows, :] = jnp.where(row_half == 0, o_even[p], o_odd[p])


def _sample_attn(q, k, v, ck, cv, p):
    n_slabs, n, _ = q.shape
    nb = n // 4
    half = nb // 2
    bt = SAMPLE_BATCH_TILE
    assert half % bt == 0
    ck4 = ck.reshape(2, half, WINDOW, KV_WIDTH)
    cv4 = cv.reshape(2, half, WINDOW, KV_WIDTH)
    cache_spec = pl.BlockSpec((2, bt, WINDOW, KV_WIDTH), lambda i: (0, i, 0, 0))
    return pl.pallas_call(
        _sample_attn_kernel,
        grid=(half // bt,),
        in_specs=[_const_spec((n_slabs, n, LANES)), _const_spec((n, KV_WIDTH)),
                  _const_spec((n, KV_WIDTH)), cache_spec, cache_spec, _smem_spec(),
                  _const_spec((2, 8, 4 * WINDOW))],
        out_specs=[_const_spec((n_slabs, n, LANES), False), cache_spec, cache_spec],
        out_shape=[jax.ShapeDtypeStruct((n_slabs, n, LANES), F32),
                   jax.ShapeDtypeStruct(ck4.shape, F32), jax.ShapeDtypeStruct(cv4.shape, F32)],
        compiler_params=pltpu.CompilerParams(
            dimension_semantics=("arbitrary",), vmem_limit_bytes=VMEM_LIMIT),
        name="sample_attn",
    )(q, k, v, ck4, cv4, p["sinks"], p["bias_sample"])


def _sample_merge_kernel(x_ref, oa_ref, ob_ref, sga_ref, sgb_ref, wa_ref, wb_ref, wout_ref, y_ref):
    oa = jnp.concatenate([oa_ref[p] for p in range(oa_ref.shape[0])], axis=1)
    y_ref[...] = _merge(x_ref[...], sga_ref[...], sgb_ref[...], oa, ob_ref[...],
                        wa_ref, wb_ref, wout_ref)


def _sample_merge(x, oa, ob, sga, sgb, p):
    n, d = x.shape
    return pl.pallas_call(
        _sample_merge_kernel,
        grid=(1,),
        in_specs=[_const_spec((n, d)), _const_spec(oa.shape), _const_spec((n, SGU_WIDTH)),
                  _const_spec((n, d)), _const_spec((n, d)),
                  _const_spec((ATTN_WIDTH, d)), _const_spec((SGU_WIDTH, d)), _const_spec((d, d))],
        out_specs=_const_spec((n, d), False),
        out_shape=jax.ShapeDtypeStruct((n, d), F32),
        compiler_params=pltpu.CompilerParams(
            dimension_semantics=("arbitrary",), vmem_limit_bytes=VMEM_LIMIT),
        name="sample_merge",
    )(x, oa, ob, sga, sgb, p["wa"], p["wb"], p["wout"])


def _route(logits):
    lane = lax.broadcasted_iota(jnp.int32, logits.shape, 1).astype(F32)
    far = float(ROUTER_LANES)
    glm = jnp.where(lane < N_GROUPS, logits, NEG_INF)
    gmax = jnp.max(glm, axis=-1, keepdims=True)
    gidx = jnp.min(jnp.where(glm == gmax, lane, far), axis=-1, keepdims=True)
    gw = 1.0 / jnp.sum(jnp.exp(glm - gmax), axis=-1, keepdims=True)
    first = EXPERT_LANE0 + EXPERTS_PER_GROUP * gidx
    sel = (lane >= first) & (lane < first + EXPERTS_PER_GROUP)
    el = jnp.where(sel, logits, NEG_INF)
    t1 = jnp.max(el, axis=-1, keepdims=True)
    i1 = jnp.min(jnp.where(el == t1, lane, far), axis=-1, keepdims=True)
    el2 = jnp.where(lane == i1, NEG_INF, el)
    t2 = jnp.max(el2, axis=-1, keepdims=True)
    i2 = jnp.min(jnp.where(el2 == t2, lane, far), axis=-1, keepdims=True)
    e2 = jnp.exp(t2 - t1)
    den = 1.0 + e2
    w1 = (1.0 / den) * gw
    w2 = (e2 / den) * gw
    return gidx, jnp.where(lane == i1 - first, w1, 0.0) + jnp.where(lane == i2 - first, w2, 0.0)


def _split_bf16(x):
    hi = x.astype(BF16)
    lo = (x - hi.astype(F32)).astype(BF16)
    return hi, lo


def _moe_kernel(y_ref, g2_ref, wrc_ref, wrh_ref, br_ref, wg_hbm, wu_hbm, wd_hbm,
                ltri_ref, eye_ref, o_ref, hs_ref, cws_ref, os_ref, wg_ref, wu_ref, wd_ref, wsem):
    n_tiles = y_ref.shape[0] // MOE_TILE
    tiles = [slice(t * MOE_TILE, (t + 1) * MOE_TILE) for t in range(n_tiles)]

    weight_copies = [pltpu.make_async_copy(src, dst, wsem.at[i]) for i, (src, dst) in
                     enumerate(((wg_hbm, wg_ref), (wu_hbm, wu_ref), (wd_hbm, wd_ref)))]

    @pl.when(pl.program_id(0) == 0)
    def _():
        for c in weight_copies:
            c.start()
        os_ref[...] = jnp.zeros_like(os_ref)

    h_split = [_split_bf16(_rms(y_ref[r, :], g2_ref[...])) for r in tiles]
    part = [_dot(hb, wrc_ref[...]) for hb, _ in h_split]
    low = [_dot(hl, wrh_ref[...]) for _, hl in h_split]
    routed = [_route(pt[:, :ROUTER_LANES] + (pt[:, ROUTER_LANES:] + lw) + br_ref[...])
              for pt, lw in zip(part, low)]

    lane = lax.broadcasted_iota(jnp.int32, (MOE_TILE, ROUTER_LANES), 1).astype(F32)
    lane1 = lane[0:1]
    onehot = [jnp.where(lane == gidx, 1.0, 0.0) for gidx, _ in routed]
    before = [_dot(ltri_ref[...], oh.astype(BF16)) for oh in onehot]
    ends, pos = [], []
    for oh, bf in zip(onehot, before):
        count = jnp.sum(oh, axis=0, keepdims=True)
        n_chunks = jnp.floor((count + (MOE_CHUNK - 1)) * (1.0 / MOE_CHUNK))
        tile_ends = []
        start_vec = jnp.zeros_like(n_chunks)
        end = jnp.zeros((1, 1), F32)
        for g in range(N_GROUPS):
            start_vec = start_vec + jnp.where(lane1 == g, end * MOE_CHUNK, 0.0)
            end = end + jnp.sum(jnp.where(lane1 == g, n_chunks, 0.0), axis=-1, keepdims=True)
            tile_ends.append(end[0, 0].astype(jnp.int32))
        ends.append(tile_ends)
        pos.append(jnp.sum(oh * (bf + start_vec), axis=-1, keepdims=True))

    pos_t = []
    for ps in pos:
        pos_hi = jnp.floor(ps * (1.0 / SLOT_SPLIT))
        pos_lo = ps - pos_hi * SLOT_SPLIT
        cols = jnp.where(lane == 0.0, pos_hi, jnp.where(lane == 1.0, pos_lo, 0.0)).astype(BF16)
        pos_t.append(_dot_nt(eye_ref[...], cols))
    slot_s = lax.broadcasted_iota(jnp.int32, (MOE_SLOTS, MOE_TILE), 0).astype(F32)
    for t, (pt, (hb, _), (_, cw)) in enumerate(zip(pos_t, h_split, routed)):
        pos_row = pt[0:1] * SLOT_SPLIT + pt[1:2]
        sort = jnp.where(slot_s == pos_row, 1.0, 0.0).astype(BF16)
        cw_hi = cw.astype(BF16).astype(F32)
        cw_parts = (cw_hi + pltpu.roll(cw - cw_hi, LANES // 2, 1)).astype(BF16)
        sorted_rows = _dot(sort, jnp.concatenate([hb, cw_parts], axis=1))
        hs_ref[t] = sorted_rows[:, :D_MODEL].astype(BF16)
        parts = sorted_rows[:, D_MODEL:]
        cws_ref[t] = parts + pltpu.roll(parts, LANES // 2, 1)

    def run_group(t, g, first_chunk, n_rows):
        rows = pl.ds(pl.multiple_of(first_chunk * MOE_CHUNK, MOE_CHUNK), n_rows)
        hsc = hs_ref[t, rows, :]
        cwc = cws_ref[t, rows, :]
        cexp = jnp.concatenate([jnp.broadcast_to(cwc[:, e:e + 1], (n_rows, D_EXPERT))
                                for e in range(EXPERTS_PER_GROUP)], axis=1)
        experts = [EXPERTS_PER_GROUP * g + e for e in range(EXPERTS_PER_GROUP)]
        a = jnp.concatenate([_dot(hsc, wg_ref[e]) for e in experts], axis=1)
        u = jnp.concatenate([_dot(hsc, wu_ref[e]) for e in experts], axis=1)
        hid = (a * jax.nn.sigmoid(a)) * u * cexp
        os_ref[t, rows, :] = _dot(hid.astype(BF16), wd_ref[g]).astype(BF16)

    def group(i, carry):
        t = lax.shift_right_logical(i, GROUP_BITS)
        g = i & (N_GROUPS - 1)
        first, last = jnp.int32(0), jnp.int32(0)
        for tt in range(n_tiles):
            for gg in range(N_GROUPS):
                here = (t == tt) & (g == gg)
                first = jnp.where(here, ends[tt][gg - 1] if gg else 0, first)
                last = jnp.where(here, ends[tt][gg], last)
        def full_pass(state):
            start, left = state
            run_group(t, g, start, MOE_PASS_CHUNKS * MOE_CHUNK)
            return start + MOE_PASS_CHUNKS, left - MOE_PASS_CHUNKS

        first, n = lax.while_loop(lambda st: st[1] > MOE_PASS_CHUNKS, full_pass,
                                  (first, last - first))
        for k in range(1, MOE_PASS_CHUNKS + 1):
            @pl.when(n == k)
            def _(k=k):
                run_group(t, g, first, k * MOE_CHUNK)
        return carry

    @pl.when(pl.program_id(0) == 0)
    def _():
        for c in weight_copies:
            c.wait()

    lax.fori_loop(0, n_tiles * N_GROUPS, group, 0)

    slot_l = lax.broadcasted_iota(jnp.int32, (MOE_TILE, MOE_SLOTS), 1).astype(F32)
    for t, (r, ps) in enumerate(zip(tiles, pos)):
        unsort = jnp.where(slot_l == ps, 1.0, 0.0).astype(BF16)
        o_ref[r, :] = y_ref[r, :] + _dot(unsort, os_ref[t])


def _moe(y, p):
    n, d = y.shape
    tm = min(MOE_STEP_ROWS, n)
    assert n % tm == 0 and tm % MOE_TILE == 0
    n_tiles = tm // MOE_TILE
    row_spec = pl.BlockSpec((tm, d), lambda i: (i, 0))
    return pl.pallas_call(
        _moe_kernel,
        grid=(n // tm,),
        in_specs=[row_spec, _const_spec((1, d)), _const_spec((d, 2 * ROUTER_LANES)),
                  _const_spec((d, ROUTER_LANES)), _const_spec((1, ROUTER_LANES)),
                  pl.BlockSpec(memory_space=pl.ANY), pl.BlockSpec(memory_space=pl.ANY),
                  pl.BlockSpec(memory_space=pl.ANY),
                  _const_spec((MOE_TILE, MOE_TILE)), _const_spec((LANES, LANES))],
        out_specs=row_spec,
        out_shape=jax.ShapeDtypeStruct((n, d), F32),
        scratch_shapes=[pltpu.VMEM((n_tiles, MOE_SLOTS, d), BF16),
                        pltpu.VMEM((n_tiles, MOE_SLOTS, ROUTER_LANES), F32),
                        pltpu.VMEM((n_tiles, MOE_SLOTS, d), BF16),
                        pltpu.VMEM((N_EXPERTS, d, D_EXPERT), BF16),
                        pltpu.VMEM((N_EXPERTS, d, D_EXPERT), BF16),
                        pltpu.VMEM((N_GROUPS, GROUP_HIDDEN, d), BF16),
                        pltpu.SemaphoreType.DMA((3,))],
        compiler_params=pltpu.CompilerParams(
            dimension_semantics=("arbitrary",), vmem_limit_bytes=MOE_VMEM_LIMIT),
        name="expert_mixer",
    )(y, p["g2"], p["wr_cat"], p["wr_hi"], p["br"], p["wg"], p["wu"], p["wd"],
      p["ltri"], p["eye"])


def _window_bias(n_q, q_tok, key_prev_ok, key_cur_ok):
    half = np.concatenate([key_prev_ok, key_cur_ok], axis=1)
    ok = np.concatenate([half, half], axis=1)
    return np.where(ok, 0.0, NEG_INF).astype(np.float32)


def _prompt_bias():
    t = np.arange(WINDOW)[:, None]
    s = np.arange(WINDOW)[None, :]
    prev_ok = s > t
    cur_ok = s <= t
    normal = _window_bias(WINDOW, t, prev_ok, cur_ok)
    first = _window_bias(WINDOW, t, np.zeros_like(prev_ok), cur_ok)
    return np.stack([normal, first])


def _sample_bias():
    out = []
    r = np.arange(8)[:, None]
    t = r // 2
    s = np.arange(WINDOW)[None, :]
    prev_ok = s > t
    for hf in range(2):
        c = s
        cur_ok = (c < 8) & (c % 2 == hf) & (c // 2 <= t)
        out.append(_window_bias(8, t, np.broadcast_to(prev_ok, (8, WINDOW)), cur_ok))
    return np.stack(out)


def _block_diag_mean(width):
    idx = np.arange(width) // HEAD_DIM
    return (idx[:, None] == idx[None, :]).astype(np.float32) / HEAD_DIM


def _prepare(norm1_g, w_in, q_norm_g, k_norm_g, attn_sinks, ln_v_g, ln_v_b, w_spatial, b_spatial,
             w_branch_a, w_branch_b, w_out, norm2_g, w_router_group, b_router_group,
             w_router_expert, b_router_expert, w_exp_gate, w_exp_up, w_exp_down):
    p = {}
    p["g1"] = norm1_g.reshape(1, D_MODEL)
    p["w_in"] = w_in.astype(BF16)
    p["gq"] = (jnp.tile(q_norm_g, N_HEADS) * (HEAD_DIM ** -0.5)).reshape(1, ATTN_WIDTH)
    p["gk"] = jnp.tile(k_norm_g, N_KV_HEADS).reshape(1, KV_WIDTH)
    p["bq"] = jnp.asarray(_block_diag_mean(ATTN_WIDTH), BF16)
    p["bk"] = jnp.asarray(_block_diag_mean(KV_WIDTH), BF16)
    p["sinks"] = attn_sinks.astype(F32)
    p["lng"] = ln_v_g.reshape(1, SGU_WIDTH)
    p["lnb"] = ln_v_b.reshape(1, SGU_WIDTH)
    tril = jnp.tril(jnp.ones((CHUNK, CHUNK), F32))
    wsp = w_spatial * tril[None]
    p["wsp"] = wsp.astype(BF16)
    p["bsp"] = jnp.repeat(b_spatial.T, SGU_GROUP_DIM, axis=1)
    w4 = wsp[:, :4, :4]
    p["coef_s"] = jnp.repeat(jnp.transpose(w4, (1, 2, 0)).reshape(16, SGU_GROUPS),
                             SGU_GROUP_DIM, axis=1)
    p["bias_sgu_s"] = jnp.repeat(b_spatial[:, :4].T, SGU_GROUP_DIM, axis=1)
    p["wa"] = w_branch_a.astype(BF16)
    p["wb"] = w_branch_b.astype(BF16)
    p["wout"] = w_out.astype(BF16)
    p["g2"] = norm2_g.reshape(1, D_MODEL)
    wr = jnp.concatenate(
        [w_router_group, w_router_expert,
         jnp.zeros((D_MODEL, ROUTER_LANES - N_GROUPS - N_EXPERTS), F32)], axis=1)
    p["wr_hi"], wr_lo = _split_bf16(wr)
    p["wr_cat"] = jnp.concatenate([p["wr_hi"], wr_lo], axis=1)
    p["br"] = jnp.concatenate(
        [b_router_group, b_router_expert,
         jnp.zeros((ROUTER_LANES - N_GROUPS - N_EXPERTS,), F32)]).reshape(1, ROUTER_LANES)
    p["wg"] = w_exp_gate.astype(BF16)
    p["wu"] = w_exp_up.astype(BF16)
    p["wd"] = w_exp_down.reshape(N_GROUPS, GROUP_HIDDEN, D_MODEL).astype(BF16)
    p["ltri"] = jnp.asarray(np.tril(np.ones((MOE_TILE, MOE_TILE), np.float32), -1), BF16)
    p["eye"] = jnp.asarray(np.eye(LANES, dtype=np.float32), BF16)
    p["bias_prompt"] = jnp.asarray(_prompt_bias())
    p["bias_sample"] = jnp.asarray(_sample_bias())
    return p


def kernel(x_prompt, x_sample, cache_k_win, cache_v_win, norm1_g, w_in, q_norm_g, k_norm_g, attn_sinks, ln_v_g, ln_v_b, w_spatial, b_spatial, w_branch_a, w_branch_b, w_out, norm2_g, w_router_group, b_router_group, w_router_expert, b_router_expert, w_exp_gate, w_exp_up, w_exp_down):
    depth = norm1_g.shape[0]
    assert depth == 1
    batch, seq, d = x_prompt.shape
    dec_batch, dec_seq, _ = x_sample.shape
    assert dec_seq == 4 and d == D_MODEL
    p = _prepare(*(a[0] for a in (
        norm1_g, w_in, q_norm_g, k_norm_g, attn_sinks, ln_v_g, ln_v_b, w_spatial, b_spatial,
        w_branch_a, w_branch_b, w_out, norm2_g, w_router_group, b_router_group,
        w_router_expert, b_router_expert, w_exp_gate, w_exp_up, w_exp_down)))

    y1p, kwin, vwin = _prompt_mixer(x_prompt, p)
    yp = _moe(y1p.reshape(batch * seq, d), p).reshape(batch, seq, d)

    xs = jnp.transpose(x_sample, (1, 0, 2)).reshape(dec_seq * dec_batch, d)
    q, k, v, ob, vs, sga, sgb = _sample_in(xs, p)
    ck = cache_k_win[0].reshape(dec_batch, WINDOW, KV_WIDTH)
    cv = cache_v_win[0].reshape(dec_batch, WINDOW, KV_WIDTH)
    oa, nk, nv = _sample_attn(q, k, v, ck, cv, p)
    y1s = _sample_merge(xs, oa, ob, sga, sgb, p)
    ys = _moe(y1s, p)
    ys = jnp.transpose(ys.reshape(dec_seq, dec_batch, d), (1, 0, 2))
    vs_out = jnp.transpose(vs.reshape(dec_seq, dec_batch, SGU_GROUPS, SGU_GROUP_DIM), (1, 0, 2, 3))

    kv_shape = (WINDOW, N_KV_HEADS, HEAD_DIM)
    return (yp, ys,
            kwin.reshape(1, batch, *kv_shape), vwin.reshape(1, batch, *kv_shape),
            nk.reshape(1, dec_batch, *kv_shape), nv.reshape(1, dec_batch, *kv_shape),
            vs_out[None])
```

```python
import numpy as np
import jax
import jax.numpy as jnp
from jax import lax
from jax.experimental import pallas as pl
from jax.experimental.pallas import tpu as pltpu

F32 = jnp.float32
BF16 = jnp.bfloat16

D_MODEL = 1024
N_HEADS = 8
N_KV_HEADS = 2
HEAD_DIM = 64
WINDOW = 128
ATTN_WIDTH = N_HEADS * HEAD_DIM
KV_WIDTH = N_KV_HEADS * HEAD_DIM
SGU_GROUPS = 4
SGU_WIDTH = D_MODEL // 2
SGU_GROUP_DIM = SGU_WIDTH // SGU_GROUPS
CHUNK = 128
N_GROUPS = 4
GROUP_BITS = 2
EXPERTS_PER_GROUP = 4
N_EXPERTS = N_GROUPS * EXPERTS_PER_GROUP
D_EXPERT = 256
GROUP_HIDDEN = EXPERTS_PER_GROUP * D_EXPERT
D_IN = ATTN_WIDTH + 2 * KV_WIDTH + 2 * SGU_WIDTH + 2 * D_MODEL
EPS = 1e-6
NEG_INF = -1e30

C_Q = 0
C_K = C_Q + ATTN_WIDTH
C_V = C_K + KV_WIDTH
C_U = C_V + KV_WIDTH
C_VS = C_U + SGU_WIDTH
C_GA = C_VS + SGU_WIDTH
C_GB = C_GA + D_MODEL

LANES = 128
ROUTER_LANES = LANES
EXPERT_LANE0 = N_GROUPS
PROMPT_TILE = 1024
MOE_TILE = 512
MOE_STEP_ROWS = 2 * MOE_TILE
MOE_CHUNK = 32
MOE_PASS_CHUNKS = 6
MOE_MAX_CHUNKS = (MOE_TILE + N_GROUPS * (MOE_CHUNK - 1)) // MOE_CHUNK
MOE_SLOTS = -(-MOE_MAX_CHUNKS * MOE_CHUNK // LANES) * LANES
SLOT_SPLIT = 128
SAMPLE_BATCH_TILE = 8
VMEM_LIMIT = 56 * 1024 * 1024
MOE_VMEM_LIMIT = 62 * 1024 * 1024

_SQRT_2_OVER_PI = np.sqrt(2.0 / np.pi).astype(np.float32)


def _dot(a, b):
    return jnp.dot(a, b, preferred_element_type=F32)


def _dot_nt(a, b):
    return lax.dot_general(a, b, (((1,), (1,)), ((), ())), preferred_element_type=F32)


def _gelu(x):
    cdf = 0.5 * (1.0 + jnp.tanh(_SQRT_2_OVER_PI * (x + 0.044715 * (x * x * x))))
    return x * cdf


def _rms(x, g):
    return x * lax.rsqrt(jnp.mean(x * x, axis=-1, keepdims=True) + EPS) * g


def _head_rms(x, blockdiag, g):
    ms = _dot((x * x).astype(BF16), blockdiag)
    return x * lax.rsqrt(ms + EPS) * g


def _qkv(x, g1, win_ref, gq, gk, bq, bk):
    h = _rms(x, g1).astype(BF16)
    qn = _head_rms(_dot(h, win_ref[:, C_Q:C_K]), bq, gq)
    kv = _dot(h, win_ref[:, C_K:C_U])
    kn = _head_rms(kv[:, :KV_WIDTH], bk, gk)
    v = kv[:, KV_WIDTH:]
    return h, qn, kn, v


def _sgu_inputs(h, win_ref, lng, lnb):
    gu = _gelu(_dot(h, win_ref[:, C_U:C_VS]))
    gv = _gelu(_dot(h, win_ref[:, C_VS:C_GA]))
    mu = jnp.mean(gv, axis=-1, keepdims=True)
    xc = gv - mu
    vsn = xc * lax.rsqrt(jnp.mean(xc * xc, axis=-1, keepdims=True) + EPS) * lng + lnb
    return gu, vsn


def _head_variants(prev, prev_rot, cur, cur_rot):
    a = jnp.concatenate([prev, cur], axis=0)
    r = jnp.concatenate([prev_rot, cur_rot], axis=0)
    lo = lax.broadcasted_iota(jnp.int32, a.shape, 1) < HEAD_DIM
    zero = jnp.zeros_like(a)
    kv0 = jnp.concatenate([jnp.where(lo, a, zero), jnp.where(lo, zero, r)], axis=0)
    kv1 = jnp.concatenate([jnp.where(lo, r, zero), jnp.where(lo, zero, a)], axis=0)
    return kv0, kv1


def _lane_slabs(x):
    return [x[:, p * LANES:(p + 1) * LANES] for p in range(x.shape[1] // LANES)]


def _scores(q_slabs, kcat, bias):
    slabs_per_kv = len(q_slabs) // N_KV_HEADS
    m = q_slabs[0].shape[0]
    out = []
    for kv in range(N_KV_HEADS):
        q = jnp.concatenate(q_slabs[kv * slabs_per_kv:(kv + 1) * slabs_per_kv], axis=0)
        s = _dot_nt(q.astype(BF16), kcat[kv])
        out += [s[i * m:(i + 1) * m] + bias for i in range(slabs_per_kv)]
    return out


def _sink_softmax(scores, sinks_ref):
    out = []
    for p, s in enumerate(scores):
        n_keys = s.shape[1] // 2
        probs = []
        for par in range(2):
            sh = s[:, par * n_keys:(par + 1) * n_keys]
            sink = sinks_ref[2 * p + par]
            m = jnp.maximum(jnp.max(sh, axis=-1, keepdims=True), sink)
            e = jnp.exp(sh - m)
            den = jnp.sum(e, axis=-1, keepdims=True) + jnp.exp(sink - m)
            probs.append((e / den).astype(BF16))
        out.append(jnp.concatenate(probs, axis=1))
    return out


def _weighted_values(probs, vcat):
    slabs_per_kv = len(probs) // N_KV_HEADS
    m = probs[0].shape[0]
    out = []
    for kv in range(N_KV_HEADS):
        pr = jnp.concatenate(probs[kv * slabs_per_kv:(kv + 1) * slabs_per_kv], axis=0)
        o = _dot(pr, vcat[kv])
        out += [o[i * m:(i + 1) * m] for i in range(slabs_per_kv)]
    return out


def _gates(h, win_ref):
    return (jax.nn.sigmoid(_dot(h, win_ref[:, C_GA:C_GB])),
            jax.nn.sigmoid(_dot(h, win_ref[:, C_GB:D_IN])))


def _merge(x, sga, sgb, oa, ob, wa_ref, wb_ref, wout_ref):
    ya = _dot(oa.astype(BF16), wa_ref[...])
    yb = _dot(ob.astype(BF16), wb_ref[...])
    hm = sga * ya + sgb * yb
    return x + _dot(hm.astype(BF16), wout_ref[...])


def _prompt_kernel(x_ref, g1_ref, win_ref, gq_ref, gk_ref, bq_ref, bk_ref, sinks_ref, lng_ref,
                   lnb_ref, wsp_ref, bsp_ref, wa_ref, wb_ref, wout_ref, bias_ref,
                   y_ref, kwin_ref, vwin_ref,
                   kprev, kprev_rot, vprev, vprev_rot):
    j = pl.program_id(1)

    @pl.when(j == 0)
    def _():
        for r in (kprev, kprev_rot, vprev, vprev_rot):
            r[...] = jnp.zeros_like(r)

    x = x_ref[0]
    h, qn, kn, v = _qkv(x, g1_ref[...], win_ref, gq_ref[...], gk_ref[...], bq_ref[...], bk_ref[...])

    kb = kn.astype(BF16)
    kr = pltpu.roll(kn, HEAD_DIM, 1).astype(BF16)
    vb = v.astype(BF16)
    vr = pltpu.roll(v, HEAD_DIM, 1).astype(BF16)

    n_blocks = x.shape[0] // WINDOW
    blocks = [slice(i * WINDOW, (i + 1) * WINDOW) for i in range(n_blocks)]
    scores, vcats = [], []
    pk, pkr, pv, pvr = kprev[...], kprev_rot[...], vprev[...], vprev_rot[...]
    for i, rows in enumerate(blocks):
        ck, ckr, cv, cvr = kb[rows], kr[rows], vb[rows], vr[rows]
        kcat = _head_variants(pk, pkr, ck, ckr)
        vcats.append(_head_variants(pv, pvr, cv, cvr))
        bias = bias_ref[jnp.where(j == 0, 1, 0)] if i == 0 else bias_ref[0]
        scores.append(_scores(_lane_slabs(qn[rows]), kcat, bias))
        pk, pkr, pv, pvr = ck, ckr, cv, cvr
    kprev[...] = pk
    kprev_rot[...] = pkr
    vprev[...] = pv
    vprev_rot[...] = pvr

    gu, vsn = _sgu_inputs(h, win_ref, lng_ref[...], lnb_ref[...])
    vsb = vsn.astype(BF16)
    mixed = [_dot(wsp_ref[g], jnp.concatenate(
        [vsb[rows, g * SGU_GROUP_DIM:(g + 1) * SGU_GROUP_DIM] for rows in blocks], axis=1))
        for g in range(SGU_GROUPS)]
    ob = jnp.concatenate([
        gu[rows] * (jnp.concatenate(
            [m[:, i * SGU_GROUP_DIM:(i + 1) * SGU_GROUP_DIM] for m in mixed], axis=1) + bsp_ref[...])
        for i, rows in enumerate(blocks)], axis=0)

    sga, sgb = _gates(h, win_ref)

    probs = [_sink_softmax(s, sinks_ref) for s in scores]
    oa = jnp.concatenate(
        [jnp.concatenate(_weighted_values(pr, vc), axis=1) for pr, vc in zip(probs, vcats)], axis=0)
    y_ref[0] = _merge(x, sga, sgb, oa, ob, wa_ref, wb_ref, wout_ref)

    @pl.when(j == pl.num_programs(1) - 1)
    def _():
        kwin_ref[0] = kn[(n_blocks - 1) * WINDOW:]
        vwin_ref[0] = v[(n_blocks - 1) * WINDOW:]


def _const_spec(shape, single_buffer=True):
    nd = len(shape)
    mode = pl.Buffered(1) if single_buffer else None
    return pl.BlockSpec(shape, lambda *_: (0,) * nd, pipeline_mode=mode)


def _smem_spec():
    return pl.BlockSpec(memory_space=pltpu.SMEM)


def _prompt_mixer(x, p):
    b, s, d = x.shape
    ts = PROMPT_TILE
    assert s % ts == 0 and ts % WINDOW == 0
    in_specs = [
        pl.BlockSpec((1, ts, d), lambda bi, j: (bi, j, 0)),
        _const_spec((1, d)), _const_spec((d, D_IN)),
        _const_spec((1, ATTN_WIDTH)), _const_spec((1, KV_WIDTH)),
        _const_spec((ATTN_WIDTH, ATTN_WIDTH)), _const_spec((KV_WIDTH, KV_WIDTH)),
        _smem_spec(),
        _const_spec((1, SGU_WIDTH)), _const_spec((1, SGU_WIDTH)),
        _const_spec((SGU_GROUPS, CHUNK, CHUNK)), _const_spec((CHUNK, SGU_WIDTH)),
        _const_spec((ATTN_WIDTH, d)), _const_spec((SGU_WIDTH, d)), _const_spec((d, d)),
        _const_spec((2, WINDOW, 4 * WINDOW)),
    ]
    out_specs = [
        pl.BlockSpec((1, ts, d), lambda bi, j: (bi, j, 0)),
        pl.BlockSpec((1, WINDOW, KV_WIDTH), lambda bi, j: (bi, 0, 0)),
        pl.BlockSpec((1, WINDOW, KV_WIDTH), lambda bi, j: (bi, 0, 0)),
    ]
    out_shape = [
        jax.ShapeDtypeStruct((b, s, d), F32),
        jax.ShapeDtypeStruct((b, WINDOW, KV_WIDTH), F32),
        jax.ShapeDtypeStruct((b, WINDOW, KV_WIDTH), F32),
    ]
    return pl.pallas_call(
        _prompt_kernel,
        grid=(b, s // ts),
        in_specs=in_specs, out_specs=out_specs, out_shape=out_shape,
        scratch_shapes=[pltpu.VMEM((WINDOW, KV_WIDTH), BF16)] * 4,
        compiler_params=pltpu.CompilerParams(
            dimension_semantics=("arbitrary", "arbitrary"), vmem_limit_bytes=VMEM_LIMIT,
            allow_input_fusion=[i in (2, 12, 13, 14) for i in range(len(in_specs))]),
        name="prompt_mixer",
    )(x, p["g1"], p["w_in"], p["gq"], p["gk"], p["bq"], p["bk"], p["sinks"], p["lng"], p["lnb"],
      p["wsp"], p["bsp"], p["wa"], p["wb"], p["wout"], p["bias_prompt"])


def _sample_in_kernel(x_ref, g1_ref, win_ref, gq_ref, gk_ref, bq_ref, bk_ref, lng_ref, lnb_ref,
                      coef_ref, sbias_ref,
                      q_ref, k_ref, v_ref, ob_ref, vs_ref, sga_ref, sgb_ref):
    x = x_ref[...]
    h, qn, kn, v = _qkv(x, g1_ref[...], win_ref, gq_ref[...], gk_ref[...], bq_ref[...], bk_ref[...])
    gu, vsn = _sgu_inputs(h, win_ref, lng_ref[...], lnb_ref[...])
    for p, slab in enumerate(_lane_slabs(qn)):
        q_ref[p] = slab
    k_ref[...] = kn
    v_ref[...] = v
    vs_ref[...] = vsn
    sga_ref[...], sgb_ref[...] = _gates(h, win_ref)
    nb = x.shape[0] // 4
    for t in range(4):
        mixed = sbias_ref[t:t + 1, :]
        for jj in range(t + 1):
            mixed = mixed + coef_ref[4 * t + jj:4 * t + jj + 1, :] * vsn[jj * nb:(jj + 1) * nb]
        ob_ref[t * nb:(t + 1) * nb, :] = gu[t * nb:(t + 1) * nb] * mixed


def _sample_in(x, p):
    n, d = x.shape
    shapes = [(ATTN_WIDTH // LANES, n, LANES), (n, KV_WIDTH), (n, KV_WIDTH), (n, SGU_WIDTH),
              (n, SGU_WIDTH), (n, d), (n, d)]
    return pl.pallas_call(
        _sample_in_kernel,
        grid=(1,),
        in_specs=[_const_spec((n, d)), _const_spec((1, d)), _const_spec((d, D_IN)),
                  _const_spec((1, ATTN_WIDTH)), _const_spec((1, KV_WIDTH)),
                  _const_spec((ATTN_WIDTH, ATTN_WIDTH)), _const_spec((KV_WIDTH, KV_WIDTH)),
                  _const_spec((1, SGU_WIDTH)), _const_spec((1, SGU_WIDTH)),
                  _const_spec((16, SGU_WIDTH)), _const_spec((4, SGU_WIDTH))],
        out_specs=[_const_spec(sh, False) for sh in shapes],
        out_shape=[jax.ShapeDtypeStruct(sh, F32) for sh in shapes],
        compiler_params=pltpu.CompilerParams(
            dimension_semantics=("arbitrary",), vmem_limit_bytes=VMEM_LIMIT,
            allow_input_fusion=[i == 2 for i in range(11)]),
        name="sample_in",
    )(x, p["g1"], p["w_in"], p["gq"], p["gk"], p["bq"], p["bk"], p["lng"], p["lnb"],
      p["coef_s"], p["bias_sgu_s"])


def _sample_attn_kernel(q_ref, k_ref, v_ref, ck_ref, cv_ref, sinks_ref, bias_ref,
                        oa_ref, nk_ref, nv_ref):
    i = pl.program_id(0)
    n_slabs = q_ref.shape[0]
    half_rows = q_ref.shape[1] // 8
    pad = jnp.zeros((WINDOW - 8, KV_WIDTH), BF16)
    row_half = lax.broadcasted_iota(jnp.int32, (8, LANES), 0) % 2

    work = []
    for bb in range(SAMPLE_BATCH_TILE):
        b_lo = i * SAMPLE_BATCH_TILE + bb
        rows = pl.ds(b_lo, 8, stride=half_rows)
        q8 = [q_ref[p, rows, :] for p in range(n_slabs)]
        k8 = k_ref[rows, :]
        v8 = v_ref[rows, :]
        ck_new = jnp.concatenate([k8.astype(BF16), pad], axis=0)
        ckr_new = jnp.concatenate([pltpu.roll(k8, HEAD_DIM, 1).astype(BF16), pad], axis=0)
        cv_new = jnp.concatenate([v8.astype(BF16), pad], axis=0)
        cvr_new = jnp.concatenate([pltpu.roll(v8, HEAD_DIM, 1).astype(BF16), pad], axis=0)
        for hf in range(2):
            kc = ck_ref[hf, bb]
            vc = cv_ref[hf, bb]
            kcat = _head_variants(kc.astype(BF16), pltpu.roll(kc, HEAD_DIM, 1).astype(BF16),
                                  ck_new, ckr_new)
            vcat = _head_variants(vc.astype(BF16), pltpu.roll(vc, HEAD_DIM, 1).astype(BF16),
                                  cv_new, cvr_new)
            work.append((rows, hf, _scores(q8, kcat, bias_ref[hf]), vcat))
            nk_ref[hf, bb] = pltpu.roll(kc, WINDOW - 4, 0)
            nv_ref[hf, bb] = pltpu.roll(vc, WINDOW - 4, 0)
            for t in range(4):
                nk_ref[hf, bb, WINDOW - 4 + t:WINDOW - 3 + t, :] = k8[2 * t + hf:2 * t + hf + 1, :]
                nv_ref[hf, bb, WINDOW - 4 + t:WINDOW - 3 + t, :] = v8[2 * t + hf:2 * t + hf + 1, :]
    probs = [_sink_softmax(s, sinks_ref) for _, _, s, _ in work]
    outs = [_weighted_values(pr, w[3]) for pr, w in zip(probs, work)]
    for (rows, _, _, _), o_even, o_odd in zip(work[0::2], outs[0::2], outs[1::2]):
        for p in range(n_slabs):
            oa_ref[p, rows, :] = jnp.where(row_half == 0, o_even[p], o_odd[p])


def _sample_attn(q, k, v, ck, cv, p):
    n_slabs, n, _ = q.shape
    nb = n // 4
    half = nb // 2
    bt = SAMPLE_BATCH_TILE
    assert half % bt == 0
    ck4 = ck.reshape(2, half, WINDOW, KV_WIDTH)
    cv4 = cv.reshape(2, half, WINDOW, KV_WIDTH)
    cache_spec = pl.BlockSpec((2, bt, WINDOW, KV_WIDTH), lambda i: (0, i, 0, 0))
    return pl.pallas_call(
        _sample_attn_kernel,
        grid=(half // bt,),
        in_specs=[_const_spec((n_slabs, n, LANES)), _const_spec((n, KV_WIDTH)),
                  _const_spec((n, KV_WIDTH)), cache_spec, cache_spec, _smem_spec(),
                  _const_spec((2, 8, 4 * WINDOW))],
        out_specs=[_const_spec((n_slabs, n, LANES), False), cache_spec, cache_spec],
        out_shape=[jax.ShapeDtypeStruct((n_slabs, n, LANES), F32),
                   jax.ShapeDtypeStruct(ck4.shape, F32), jax.ShapeDtypeStruct(cv4.shape, F32)],
        compiler_params=pltpu.CompilerParams(
            dimension_semantics=("arbitrary",), vmem_limit_bytes=VMEM_LIMIT),
        name="sample_attn",
    )(q, k, v, ck4, cv4, p["sinks"], p["bias_sample"])


def _sample_merge_kernel(x_ref, oa_ref, ob_ref, sga_ref, sgb_ref, wa_ref, wb_ref, wout_ref, y_ref):
    oa = jnp.concatenate([oa_ref[p] for p in range(oa_ref.shape[0])], axis=1)
    y_ref[...] = _merge(x_ref[...], sga_ref[...], sgb_ref[...], oa, ob_ref[...],
                        wa_ref, wb_ref, wout_ref)


def _sample_merge(x, oa, ob, sga, sgb, p):
    n, d = x.shape
    return pl.pallas_call(
        _sample_merge_kernel,
        grid=(1,),
        in_specs=[_const_spec((n, d)), _const_spec(oa.shape), _const_spec((n, SGU_WIDTH)),
                  _const_spec((n, d)), _const_spec((n, d)),
                  _const_spec((ATTN_WIDTH, d)), _const_spec((SGU_WIDTH, d)), _const_spec((d, d))],
        out_specs=_const_spec((n, d), False),
        out_shape=jax.ShapeDtypeStruct((n, d), F32),
        compiler_params=pltpu.CompilerParams(
            dimension_semantics=("arbitrary",), vmem_limit_bytes=VMEM_LIMIT,
            allow_input_fusion=[i >= 5 for i in range(8)]),
        name="sample_merge",
    )(x, oa, ob, sga, sgb, p["wa"], p["wb"], p["wout"])


def _route(logits):
    lane = lax.broadcasted_iota(jnp.int32, logits.shape, 1).astype(F32)
    far = float(ROUTER_LANES)
    glm = jnp.where(lane < N_GROUPS, logits, NEG_INF)
    gmax = jnp.max(glm, axis=-1, keepdims=True)
    gidx = jnp.min(jnp.where(glm == gmax, lane, far), axis=-1, keepdims=True)
    gw = 1.0 / jnp.sum(jnp.exp(glm - gmax), axis=-1, keepdims=True)
    first = EXPERT_LANE0 + EXPERTS_PER_GROUP * gidx
    sel = (lane >= first) & (lane < first + EXPERTS_PER_GROUP)
    el = jnp.where(sel, logits, NEG_INF)
    t1 = jnp.max(el, axis=-1, keepdims=True)
    i1 = jnp.min(jnp.where(el == t1, lane, far), axis=-1, keepdims=True)
    el2 = jnp.where(lane == i1, NEG_INF, el)
    t2 = jnp.max(el2, axis=-1, keepdims=True)
    i2 = jnp.min(jnp.where(el2 == t2, lane, far), axis=-1, keepdims=True)
    e2 = jnp.exp(t2 - t1)
    den = 1.0 + e2
    w1 = (1.0 / den) * gw
    w2 = (e2 / den) * gw
    return gidx, jnp.where(lane == i1 - first, w1, 0.0) + jnp.where(lane == i2 - first, w2, 0.0)


def _split_bf16(x):
    hi = x.astype(BF16)
    lo = (x - hi.astype(F32)).astype(BF16)
    return hi, lo


def _moe_kernel(y_ref, g2_ref, wrc_ref, wrh_ref, br_ref, wg_hbm, wu_hbm, wd_hbm,
                ltri_ref, eye_ref, o_ref, hs_ref, cws_ref, os_ref, wg_ref, wu_ref, wd_ref, wsem):
    n_tiles = y_ref.shape[0] // MOE_TILE
    tiles = [slice(t * MOE_TILE, (t + 1) * MOE_TILE) for t in range(n_tiles)]

    weight_copies = [pltpu.make_async_copy(src, dst, wsem.at[i]) for i, (src, dst) in
                     enumerate(((wg_hbm, wg_ref), (wu_hbm, wu_ref), (wd_hbm, wd_ref)))]

    @pl.when(pl.program_id(0) == 0)
    def _():
        for c in weight_copies:
            c.start()
        os_ref[...] = jnp.zeros_like(os_ref)

    h_split = [_split_bf16(_rms(y_ref[r, :], g2_ref[...])) for r in tiles]
    part = [_dot(hb, wrc_ref[...]) for hb, _ in h_split]
    low = [_dot(hl, wrh_ref[...]) for _, hl in h_split]
    routed = [_route(pt[:, :ROUTER_LANES] + (pt[:, ROUTER_LANES:] + lw) + br_ref[...])
              for pt, lw in zip(part, low)]

    lane = lax.broadcasted_iota(jnp.int32, (MOE_TILE, ROUTER_LANES), 1).astype(F32)
    lane1 = lane[0:1]
    onehot = [jnp.where(lane == gidx, 1.0, 0.0) for gidx, _ in routed]
    before = [_dot(ltri_ref[...], oh.astype(BF16)) for oh in onehot]
    ends, pos = [], []
    for oh, bf in zip(onehot, before):
        count = jnp.sum(oh, axis=0, keepdims=True)
        n_chunks = jnp.floor((count + (MOE_CHUNK - 1)) * (1.0 / MOE_CHUNK))
        tile_ends = []
        start_vec = jnp.zeros_like(n_chunks)
        end = jnp.zeros((1, 1), F32)
        for g in range(N_GROUPS):
            start_vec = start_vec + jnp.where(lane1 == g, end * MOE_CHUNK, 0.0)
            end = end + jnp.sum(jnp.where(lane1 == g, n_chunks, 0.0), axis=-1, keepdims=True)
            tile_ends.append(end[0, 0].astype(jnp.int32))
        ends.append(tile_ends)
        pos.append(jnp.sum(oh * (bf + start_vec), axis=-1, keepdims=True))

    pos_t = []
    for ps in pos:
        pos_hi = jnp.floor(ps * (1.0 / SLOT_SPLIT))
        pos_lo = ps - pos_hi * SLOT_SPLIT
        cols = jnp.where(lane == 0.0, pos_hi, jnp.where(lane == 1.0, pos_lo, 0.0)).astype(BF16)
        pos_t.append(_dot_nt(eye_ref[...], cols))
    slot_s = lax.broadcasted_iota(jnp.int32, (MOE_SLOTS, MOE_TILE), 0).astype(F32)
    for t, (pt, (hb, _), (_, cw)) in enumerate(zip(pos_t, h_split, routed)):
        pos_row = pt[0:1] * SLOT_SPLIT + pt[1:2]
        sort = jnp.where(slot_s == pos_row, 1.0, 0.0).astype(BF16)
        cw_hi = cw.astype(BF16).astype(F32)
        cw_parts = (cw_hi + pltpu.roll(cw - cw_hi, LANES // 2, 1)).astype(BF16)
        sorted_rows = _dot(sort, jnp.concatenate([hb, cw_parts], axis=1))
        hs_ref[t] = sorted_rows[:, :D_MODEL].astype(BF16)
        parts = sorted_rows[:, D_MODEL:]
        cws_ref[t] = parts + pltpu.roll(parts, LANES // 2, 1)

    def run_group(t, g, first_chunk, n_rows):
        rows = pl.ds(pl.multiple_of(first_chunk * MOE_CHUNK, MOE_CHUNK), n_rows)
        hsc = hs_ref[t, rows, :]
        cwc = cws_ref[t, rows, :]
        cexp = jnp.concatenate([jnp.broadcast_to(cwc[:, e:e + 1], (n_rows, D_EXPERT))
                                for e in range(EXPERTS_PER_GROUP)], axis=1)
        experts = [EXPERTS_PER_GROUP * g + e for e in range(EXPERTS_PER_GROUP)]
        a = jnp.concatenate([_dot(hsc, wg_ref[e]) for e in experts], axis=1)
        u = jnp.concatenate([_dot(hsc, wu_ref[e]) for e in experts], axis=1)
        hid = (a * jax.nn.sigmoid(a)) * u * cexp
        os_ref[t, rows, :] = _dot(hid.astype(BF16), wd_ref[g]).astype(BF16)

    def group(i, carry):
        t = lax.shift_right_logical(i, GROUP_BITS)
        g = i & (N_GROUPS - 1)
        first, last = jnp.int32(0), jnp.int32(0)
        for tt in range(n_tiles):
            for gg in range(N_GROUPS):
                here = (t == tt) & (g == gg)
                first = jnp.where(here, ends[tt][gg - 1] if gg else 0, first)
                last = jnp.where(here, ends[tt][gg], last)
        def full_pass(state):
            start, left = state
            run_group(t, g, start, MOE_PASS_CHUNKS * MOE_CHUNK)
            return start + MOE_PASS_CHUNKS, left - MOE_PASS_CHUNKS

        first, n = lax.while_loop(lambda st: st[1] > MOE_PASS_CHUNKS, full_pass,
                                  (first, last - first))
        for k in range(1, MOE_PASS_CHUNKS + 1):
            @pl.when(n == k)
            def _(k=k):
                run_group(t, g, first, k * MOE_CHUNK)
        return carry

    @pl.when(pl.program_id(0) == 0)
    def _():
        for c in weight_copies:
            c.wait()

    lax.fori_loop(0, n_tiles * N_GROUPS, group, 0)

    slot_l = lax.broadcasted_iota(jnp.int32, (MOE_TILE, MOE_SLOTS), 1).astype(F32)
    for t, (r, ps) in enumerate(zip(tiles, pos)):
        unsort = jnp.where(slot_l == ps, 1.0, 0.0).astype(BF16)
        o_ref[r, :] = y_ref[r, :] + _dot(unsort, os_ref[t])


def _moe(y, p):
    n, d = y.shape
    tm = min(MOE_STEP_ROWS, n)
    assert n % tm == 0 and tm % MOE_TILE == 0
    n_tiles = tm // MOE_TILE
    row_spec = pl.BlockSpec((tm, d), lambda i: (i, 0))
    return pl.pallas_call(
        _moe_kernel,
        grid=(n // tm,),
        in_specs=[row_spec, _const_spec((1, d)), _const_spec((d, 2 * ROUTER_LANES)),
                  _const_spec((d, ROUTER_LANES)), _const_spec((1, ROUTER_LANES)),
                  pl.BlockSpec(memory_space=pl.ANY), pl.BlockSpec(memory_space=pl.ANY),
                  pl.BlockSpec(memory_space=pl.ANY),
                  _const_spec((MOE_TILE, MOE_TILE)), _const_spec((LANES, LANES))],
        out_specs=row_spec,
        out_shape=jax.ShapeDtypeStruct((n, d), F32),
        scratch_shapes=[pltpu.VMEM((n_tiles, MOE_SLOTS, d), BF16),
                        pltpu.VMEM((n_tiles, MOE_SLOTS, ROUTER_LANES), F32),
                        pltpu.VMEM((n_tiles, MOE_SLOTS, d), BF16),
                        pltpu.VMEM((N_EXPERTS, d, D_EXPERT), BF16),
                        pltpu.VMEM((N_EXPERTS, d, D_EXPERT), BF16),
                        pltpu.VMEM((N_GROUPS, GROUP_HIDDEN, d), BF16),
                        pltpu.SemaphoreType.DMA((3,))],
        compiler_params=pltpu.CompilerParams(
            dimension_semantics=("arbitrary",), vmem_limit_bytes=MOE_VMEM_LIMIT),
        name="expert_mixer",
    )(y, p["g2"], p["wr_cat"], p["wr_hi"], p["br"], p["wg"], p["wu"], p["wd"],
      p["ltri"], p["eye"])


def _window_bias(n_q, q_tok, key_prev_ok, key_cur_ok):
    half = np.concatenate([key_prev_ok, key_cur_ok], axis=1)
    ok = np.concatenate([half, half], axis=1)
    return np.where(ok, 0.0, NEG_INF).astype(np.float32)


def _prompt_bias():
    t = np.arange(WINDOW)[:, None]
    s = np.arange(WINDOW)[None, :]
    prev_ok = s > t
    cur_ok = s <= t
    normal = _window_bias(WINDOW, t, prev_ok, cur_ok)
    first = _window_bias(WINDOW, t, np.zeros_like(prev_ok), cur_ok)
    return np.stack([normal, first])


def _sample_bias():
    out = []
    r = np.arange(8)[:, None]
    t = r // 2
    s = np.arange(WINDOW)[None, :]
    prev_ok = s > t
    for hf in range(2):
        c = s
        cur_ok = (c < 8) & (c % 2 == hf) & (c // 2 <= t)
        out.append(_window_bias(8, t, np.broadcast_to(prev_ok, (8, WINDOW)), cur_ok))
    return np.stack(out)


def _block_diag_mean(width):
    idx = np.arange(width) // HEAD_DIM
    return (idx[:, None] == idx[None, :]).astype(np.float32) / HEAD_DIM


def _prepare(norm1_g, w_in, q_norm_g, k_norm_g, attn_sinks, ln_v_g, ln_v_b, w_spatial, b_spatial,
             w_branch_a, w_branch_b, w_out, norm2_g, w_router_group, b_router_group,
             w_router_expert, b_router_expert, w_exp_gate, w_exp_up, w_exp_down):
    p = {}
    p["g1"] = norm1_g.reshape(1, D_MODEL)
    p["w_in"] = w_in.astype(BF16)
    p["gq"] = (jnp.tile(q_norm_g, N_HEADS) * (HEAD_DIM ** -0.5)).reshape(1, ATTN_WIDTH)
    p["gk"] = jnp.tile(k_norm_g, N_KV_HEADS).reshape(1, KV_WIDTH)
    p["bq"] = jnp.asarray(_block_diag_mean(ATTN_WIDTH), BF16)
    p["bk"] = jnp.asarray(_block_diag_mean(KV_WIDTH), BF16)
    p["sinks"] = attn_sinks.astype(F32)
    p["lng"] = ln_v_g.reshape(1, SGU_WIDTH)
    p["lnb"] = ln_v_b.reshape(1, SGU_WIDTH)
    tril = jnp.tril(jnp.ones((CHUNK, CHUNK), F32))
    wsp = w_spatial * tril[None]
    p["wsp"] = wsp.astype(BF16)
    p["bsp"] = jnp.repeat(b_spatial.T, SGU_GROUP_DIM, axis=1)
    w4 = wsp[:, :4, :4]
    p["coef_s"] = jnp.repeat(jnp.transpose(w4, (1, 2, 0)).reshape(16, SGU_GROUPS),
                             SGU_GROUP_DIM, axis=1)
    p["bias_sgu_s"] = jnp.repeat(b_spatial[:, :4].T, SGU_GROUP_DIM, axis=1)
    p["wa"] = w_branch_a.astype(BF16)
    p["wb"] = w_branch_b.astype(BF16)
    p["wout"] = w_out.astype(BF16)
    p["g2"] = norm2_g.reshape(1, D_MODEL)
    wr = jnp.concatenate(
        [w_router_group, w_router_expert,
         jnp.zeros((D_MODEL, ROUTER_LANES - N_GROUPS - N_EXPERTS), F32)], axis=1)
    p["wr_hi"], wr_lo = _split_bf16(wr)
    p["wr_cat"] = jnp.concatenate([p["wr_hi"], wr_lo], axis=1)
    p["br"] = jnp.concatenate(
        [b_router_group, b_router_expert,
         jnp.zeros((ROUTER_LANES - N_GROUPS - N_EXPERTS,), F32)]).reshape(1, ROUTER_LANES)
    p["wg"] = w_exp_gate.astype(BF16)
    p["wu"] = w_exp_up.astype(BF16)
    p["wd"] = w_exp_down.reshape(N_GROUPS, GROUP_HIDDEN, D_MODEL).astype(BF16)
    p["ltri"] = jnp.asarray(np.tril(np.ones((MOE_TILE, MOE_TILE), np.float32), -1), BF16)
    p["eye"] = jnp.asarray(np.eye(LANES, dtype=np.float32), BF16)
    p["bias_prompt"] = jnp.asarray(_prompt_bias())
    p["bias_sample"] = jnp.asarray(_sample_bias())
    return p


def kernel(x_prompt, x_sample, cache_k_win, cache_v_win, norm1_g, w_in, q_norm_g, k_norm_g, attn_sinks, ln_v_g, ln_v_b, w_spatial, b_spatial, w_branch_a, w_branch_b, w_out, norm2_g, w_router_group, b_router_group, w_router_expert, b_router_expert, w_exp_gate, w_exp_up, w_exp_down):
    depth = norm1_g.shape[0]
    assert depth == 1
    batch, seq, d = x_prompt.shape
    dec_batch, dec_seq, _ = x_sample.shape
    assert dec_seq == 4 and d == D_MODEL
    p = _prepare(*(a[0] for a in (
        norm1_g, w_in, q_norm_g, k_norm_g, attn_sinks, ln_v_g, ln_v_b, w_spatial, b_spatial,
        w_branch_a, w_branch_b, w_out, norm2_g, w_router_group, b_router_group,
        w_router_expert, b_router_expert, w_exp_gate, w_exp_up, w_exp_down)))

    y1p, kwin, vwin = _prompt_mixer(x_prompt, p)
    yp = _moe(y1p.reshape(batch * seq, d), p).reshape(batch, seq, d)

    xs = jnp.transpose(x_sample, (1, 0, 2)).reshape(dec_seq * dec_batch, d)
    q, k, v, ob, vs, sga, sgb = _sample_in(xs, p)
    ck = cache_k_win[0].reshape(dec_batch, WINDOW, KV_WIDTH)
    cv = cache_v_win[0].reshape(dec_batch, WINDOW, KV_WIDTH)
    oa, nk, nv = _sample_attn(q, k, v, ck, cv, p)
    y1s = _sample_merge(xs, oa, ob, sga, sgb, p)
    ys = _moe(y1s, p)
    ys = jnp.transpose(ys.reshape(dec_seq, dec_batch, d), (1, 0, 2))
    vs_out = jnp.transpose(vs.reshape(dec_seq, dec_batch, SGU_GROUPS, SGU_GROUP_DIM), (1, 0, 2, 3))

    kv_shape = (WINDOW, N_KV_HEADS, HEAD_DIM)
    return (yp, ys,
            kwin.reshape(1, batch, *kv_shape), vwin.reshape(1, batch, *kv_shape),
            nk.reshape(1, dec_batch, *kv_shape), nv.reshape(1, dec_batch, *kv_shape),
            vs_out[None])
```
